```python
import math
import jax
import jax.numpy as jnp
from jax import lax
import numpy as np

D_MODEL = 1024
BATCH = 4
SEQ = 4096
DEPTH = 2

GRID_W = 64
CTX_LEN = 256
N_BRANCH = 4
RMS_EPS = 1e-6
CONV_K = 3

MLSTM_HEADS = 4
MLSTM_HEAD_DIM = 128
MLSTM_WIDTH = MLSTM_HEADS * MLSTM_HEAD_DIM
MLSTM_CHUNK = 128
SGU_GROUPS = 4
SGU_GROUP_DIM = 128
SGU_WIDTH = SGU_GROUPS * SGU_GROUP_DIM
SGU_CHUNK = 128
S5_GROUP_DIM = 16
S5_GROUPS = 24
S5_WIDTH = S5_GROUPS * S5_GROUP_DIM
S5_STATE = 64
S5_DT_MIN = 1e-3
S5_DT_MAX = 1e-1
SCONV_WIDTH = 512
FFN_HIDDEN = ((8 * D_MODEL + 3 * 256 - 1) // (3 * 256)) * 256

STATE_WIDTHS = (MLSTM_WIDTH, MLSTM_WIDTH, MLSTM_WIDTH, 4 * MLSTM_HEADS, S5_WIDTH)
OUT_WIDTHS = (MLSTM_WIDTH, SGU_WIDTH, SGU_WIDTH, SCONV_WIDTH, SCONV_WIDTH, SCONV_WIDTH, N_BRANCH * D_MODEL)
STATE_DIM = sum(STATE_WIDTHS)
IN_DIM = STATE_DIM + sum(OUT_WIDTHS)

kernel_name = 'hybrid_mlstm_sgu_s5_conv_dit_trunk'


def rmsnorm(x, g):
    xf = x.astype(jnp.float32)
    y = xf * lax.rsqrt(jnp.mean(xf * xf, axis=-1, keepdims=True) + RMS_EPS)
    return (y * g.astype(jnp.float32)).astype(x.dtype)


def modulate(x, g, shift, scale):
    return rmsnorm(x, g) * (1 + scale) + shift


def swiglu(h, w_gate, w_up, w_down):
    return (jax.nn.silu(h @ w_gate) * (h @ w_up)) @ w_down


def split_cols(z, widths):
    parts, off = [], 0
    for w in widths:
        parts.append(z[..., off:off + w])
        off += w
    return parts


def conv_seq(x, w):
    t = x.shape[1]
    pad = CONV_K // 2
    xp = jnp.pad(x, ((0, 0), (pad, pad), (0, 0)))
    out = xp[:, 0:t] * w[0]
    for j in range(1, CONV_K):
        out = out + xp[:, j:j + t] * w[j]
    return out


def conv_grid_rows(x, w):
    b, t, ch = x.shape
    rows = t // GRID_W
    return conv_seq(x.reshape(b * rows, GRID_W, ch), w).reshape(b, t, ch)


def mlstm_zero_state(b):
    return (jnp.zeros((b, MLSTM_HEADS, MLSTM_HEAD_DIM, MLSTM_HEAD_DIM), jnp.float32),
            jnp.zeros((b, MLSTM_HEADS, MLSTM_HEAD_DIM), jnp.float32),
            jnp.zeros((b, MLSTM_HEADS), jnp.float32))


def mlstm_inputs(q_raw, k_raw, v_raw, gate_raw, w_conv_qk, b_gates, conv_fn):
    b, t, _ = q_raw.shape

    def heads(a):
        return a.astype(jnp.float32).reshape(b, t, MLSTM_HEADS, MLSTM_HEAD_DIM).transpose(0, 2, 1, 3)
    q = heads(jax.nn.silu(conv_fn(q_raw, w_conv_qk[0])))
    k = heads(jax.nn.silu(conv_fn(k_raw, w_conv_qk[1]))) * (MLSTM_HEAD_DIM ** -0.5)
    v = heads(v_raw)
    g = (gate_raw + b_gates).astype(jnp.float32).reshape(b, t, 4, MLSTM_HEADS).transpose(2, 0, 3, 1)
    return q, k, v, g


def mlstm_chunked(q, k, v, i_pre, f_pre, state, with_out):
    bsz, nh, t, dh = q.shape
    lc = MLSTM_CHUNK
    nc = t // lc

    def chunks(a):
        return jnp.moveaxis(a.reshape(bsz, nh, nc, lc, *a.shape[3:]), 2, 0)
    xs = (chunks(q), chunks(k), chunks(v), chunks(i_pre), chunks(jax.nn.log_sigmoid(f_pre)))
    causal = jnp.tril(jnp.ones((lc, lc), dtype=bool))

    def step(carry, inp):
        c_mat, n_vec, m = carry
        qc, kc, vc, ic, lfc = inp
        bcum = jnp.cumsum(lfc, axis=-1)
        b_last = bcum[..., -1]
        log_s = b_last[..., None] - bcum + ic
        m_new = jnp.maximum(b_last + m, jnp.max(log_s, axis=-1))
        w_s = jnp.exp(log_s - m_new[..., None])
        decay = jnp.exp(b_last + m - m_new)
        c_new = decay[..., None, None] * c_mat + jnp.einsum('bhs,bhsd,bhse->bhde', w_s, kc, vc)
        n_new = decay[..., None] * n_vec + jnp.einsum('bhs,bhsd->bhd', w_s, kc)
        if not with_out:
            return (c_new, n_new, m_new), None
        log_w = jnp.where(causal, bcum[..., :, None] - bcum[..., None, :] + ic[..., None, :], -jnp.inf)
        log_inter = bcum + m[..., None]
        m_t = jnp.maximum(log_inter, jnp.max(log_w, axis=-1))
        inter = jnp.exp(log_inter - m_t)
        s = jnp.einsum('bhtd,bhsd->bhts', qc, kc) * jnp.exp(log_w - m_t[..., None])
        num = inter[..., None] * jnp.einsum('bhtd,bhde->bhte', qc, c_mat) + jnp.einsum('bhts,bhse->bhte', s, vc)
        den = inter * jnp.einsum('bhtd,bhd->bht', qc, n_vec) + jnp.sum(s, axis=-1)
        h_t = num / jnp.maximum(jnp.abs(den), jnp.exp(-m_t))[..., None]
        return (c_new, n_new, m_new), h_t

    state, hs = lax.scan(step, state, xs)
    if not with_out:
        return None, state
    return jnp.moveaxis(hs, 0, 2).reshape(bsz, nh, t, dh), state


def mlstm_bidir(q, k, v, g, init, with_out):
    def fl(a):
        return jnp.flip(a, axis=2)
    h_f, st_f = mlstm_chunked(q, k, v, g[0], g[1], init[0], with_out)
    h_b, st_b = mlstm_chunked(fl(q), fl(k), fl(v), fl(g[2]), fl(g[3]), init[1], with_out)
    h = h_f + fl(h_b) if with_out else None
    return h, (st_f, st_b)


def spatial_gating(u, v, g, w_s, b_s):
    bsz, t, wd = v.shape
    vf = v.astype(jnp.float32)
    mu = jnp.mean(vf, axis=-1, keepdims=True)
    var = jnp.mean(jnp.square(vf - mu), axis=-1, keepdims=True)
    vn = ((vf - mu) * lax.rsqrt(var + RMS_EPS) * g.astype(jnp.float32)).astype(v.dtype)
    vg = vn.reshape(bsz, t // SGU_CHUNK, SGU_CHUNK, SGU_GROUPS, SGU_GROUP_DIM)
    mixed = jnp.einsum('gts,bnsgd->bntgd', w_s, vg) + jnp.swapaxes(b_s, 0, 1)[None, None, :, :, None]
    return u * mixed.reshape(bsz, t, wd)


def s5_discretise(a_re, a_im, log_dt, b_re, b_im):
    a_re = a_re.astype(jnp.float32)
    a_im = a_im.astype(jnp.float32)
    dt = jnp.exp(log_dt.astype(jnp.float32))[:, None]
    la_re, la_im = dt * a_re, dt * a_im
    mag = jnp.exp(la_re)
    ab_re, ab_im = mag * jnp.cos(la_im), mag * jnp.sin(la_im)
    nr, ni = ab_re - 1.0, ab_im
    den = a_re * a_re + a_im * a_im
    f_re = ((nr * a_re + ni * a_im) / den)[..., None]
    f_im = ((ni * a_re - nr * a_im) / den)[..., None]
    b_re = b_re.astype(jnp.float32)
    b_im = b_im.astype(jnp.float32)
    bb_re = f_re * b_re - f_im * b_im
    bb_im = f_re * b_im + f_im * b_re
    return la_re, la_im, ab_re, ab_im, bb_re, bb_im


def s5_drive(u, bb_re, bb_im):
    bsz, t, _ = u.shape
    ug = u.reshape(bsz, t, S5_GROUPS, S5_GROUP_DIM)
    return jnp.einsum('btgj,gpj->btgp', ug, bb_re), jnp.einsum('btgj,gpj->btgp', ug, bb_im)


def s5_combine(e1, e2):
    a1r, a1i, b1r, b1i = e1
    a2r, a2i, b2r, b2i = e2
    return (a2r * a1r - a2i * a1i, a2r * a1i + a2i * a1r,
            a2r * b1r - a2i * b1i + b2r, a2r * b1i + a2i * b1r + b2i)


def s5_states(u, disc, x0, reverse):
    _, _, ab_re, ab_im, bb_re, bb_im = disc
    bu_re, bu_im = s5_drive(u, bb_re, bb_im)
    if reverse:
        bu_re, bu_im = jnp.flip(bu_re, axis=1), jnp.flip(bu_im, axis=1)
    if x0 is not None:
        x0_re, x0_im = x0
        bu_re = bu_re.at[:, 0].add(ab_re * x0_re - ab_im * x0_im)
        bu_im = bu_im.at[:, 0].add(ab_re * x0_im + ab_im * x0_re)
    t = u.shape[1]
    a_re_t = jnp.broadcast_to(ab_re, (1, t) + ab_re.shape)
    a_im_t = jnp.broadcast_to(ab_im, (1, t) + ab_im.shape)
    _, _, xr, xi = lax.associative_scan(s5_combine, (a_re_t, a_im_t, bu_re, bu_im), axis=1)
    if reverse:
        xr, xi = jnp.flip(xr, axis=1), jnp.flip(xi, axis=1)
    return xr, xi


def s5_final_state(u, disc, reverse):
    la_re, la_im, _, _, bb_re, bb_im = disc
    bu_re, bu_im = s5_drive(u, bb_re, bb_im)
    t = u.shape[1]
    pos = jnp.arange(t, dtype=jnp.float32)
    steps = pos if reverse else (t - 1) - pos
    mag = jnp.exp(steps[:, None, None] * la_re)
    ang = steps[:, None, None] * la_im
    p_re, p_im = mag * jnp.cos(ang), mag * jnp.sin(ang)
    x_re = jnp.einsum('tgp,btgp->bgp', p_re, bu_re) - jnp.einsum('tgp,btgp->bgp', p_im, bu_im)
    x_im = jnp.einsum('tgp,btgp->bgp', p_re, bu_im) + jnp.einsum('tgp,btgp->bgp', p_im, bu_re)
    return x_re, x_im


def s5_out(xr, xi, u, c_re, c_im, d, w_glu, b_glu, dtype):
    bsz, t, _ = u.shape
    y = (jnp.einsum('gjp,btgp->btgj', c_re.astype(jnp.float32), xr)
         - jnp.einsum('gjp,btgp->btgj', c_im.astype(jnp.float32), xi))
    y = y.reshape(bsz, t, S5_WIDTH) + d.astype(jnp.float32) * u
    y = jax.nn.gelu(y).astype(dtype)
    return y * jax.nn.sigmoid(y @ w_glu + b_glu)


def merge_branches(o_cols, h_mlstm, y_s5, conv_fn, g_mh, g_sgu, w_sgu, b_sgu, w_sconv,
                   w_up_mlstm, w_up_sgu, w_up_s5, w_up_sconv, w_out):
    o_raw, su, sv, cb, cc, cx, gate_raw = o_cols
    bsz, t, _ = o_raw.shape
    hm = jnp.transpose(h_mlstm, (0, 2, 1, 3))
    hm = hm * lax.rsqrt(jnp.mean(hm * hm, axis=-1, keepdims=True) + RMS_EPS)
    hm = (hm.reshape(bsz, t, MLSTM_WIDTH) * g_mh.astype(jnp.float32)).astype(o_raw.dtype)
    y_a = hm * jax.nn.sigmoid(o_raw)
    y_b = spatial_gating(jax.nn.gelu(su), jax.nn.gelu(sv), g_sgu, w_sgu, b_sgu)
    y_d = cb * conv_fn(cc * cx, w_sconv)
    gates = jax.nn.sigmoid(gate_raw.reshape(bsz, t, N_BRANCH, D_MODEL))
    merged = (gates[:, :, 0] * (y_a @ w_up_mlstm) + gates[:, :, 1] * (y_b @ w_up_sgu)
              + gates[:, :, 2] * (y_s5 @ w_up_s5) + gates[:, :, 3] * (y_d @ w_up_sconv))
    return merged @ w_out


def token_mixer(hx, hc, need_ctx, w_in, b_gates, w_conv_qk, g_mh, g_sgu, w_sgu, b_sgu,
                s5_a_re, s5_a_im, s5_log_dt, s5_b_re, s5_b_im, s5_c_re, s5_c_im, s5_d,
                w_glu, b_glu, w_sconv, w_up_mlstm, w_up_sgu, w_up_s5, w_up_sconv, w_out):
    bsz = hx.shape[0]
    zx = hx @ w_in
    zc = hc @ (w_in if need_ctx else w_in[:, :STATE_DIM])
    qx, kx, vx, gx, ux = split_cols(zx[..., :STATE_DIM], STATE_WIDTHS)
    qc, kc, vc, gc, uc = split_cols(zc[..., :STATE_DIM], STATE_WIDTHS)

    mc = mlstm_inputs(qc, kc, vc, gc, w_conv_qk, b_gates, conv_seq)
    h_mc, st_ctx = mlstm_bidir(*mc, (mlstm_zero_state(bsz), mlstm_zero_state(bsz)), need_ctx)
    mx = mlstm_inputs(qx, kx, vx, gx, w_conv_qk, b_gates, conv_grid_rows)
    h_mx, _ = mlstm_bidir(*mx, st_ctx, True)

    discs = [s5_discretise(s5_a_re[d], s5_a_im[d], s5_log_dt[d], s5_b_re, s5_b_im) for d in range(2)]
    uc32 = uc.astype(jnp.float32)
    ux32 = ux.astype(jnp.float32)
    if need_ctx:
        sc_f = s5_states(uc32, discs[0], None, False)
        sc_b = s5_states(uc32, discs[1], None, True)
        x0s = ((sc_f[0][:, -1], sc_f[1][:, -1]), (sc_b[0][:, 0], sc_b[1][:, 0]))
        y_s5c = s5_out(sc_f[0] + sc_b[0], sc_f[1] + sc_b[1], uc32, s5_c_re, s5_c_im, s5_d, w_glu, b_glu, hc.dtype)
    else:
        x0s = (s5_final_state(uc32, discs[0], False), s5_final_state(uc32, discs[1], True))
    sx_f = s5_states(ux32, discs[0], x0s[0], False)
    sx_b = s5_states(ux32, discs[1], x0s[1], True)
    y_s5x = s5_out(sx_f[0] + sx_b[0], sx_f[1] + sx_b[1], ux32, s5_c_re, s5_c_im, s5_d, w_glu, b_glu, hx.dtype)

    shared = (g_mh, g_sgu, w_sgu, b_sgu, w_sconv, w_up_mlstm, w_up_sgu, w_up_s5, w_up_sconv, w_out)
    yx = merge_branches(split_cols(zx[..., STATE_DIM:], OUT_WIDTHS), h_mx, y_s5x, conv_grid_rows, *shared)
    yc = None
    if need_ctx:
        yc = merge_branches(split_cols(zc[..., STATE_DIM:], OUT_WIDTHS), h_mc, y_s5c, conv_seq, *shared)
    return yx, yc


def setup_inputs(seed: int = 0) -> dict:
    key = jax.random.key(seed)
    ks = iter(jax.random.split(key, 48))

    def nrm(shape, scale):
        return jax.random.normal(next(ks), shape, jnp.float32) * scale
    L, D = DEPTH, D_MODEL
    f_bias = jnp.linspace(3.0, 6.0, MLSTM_HEADS, dtype=jnp.float32)
    i_bias = jnp.zeros((MLSTM_HEADS,), jnp.float32)
    gate_bias = jnp.concatenate([i_bias, f_bias, i_bias, f_bias])
    n_idx = jnp.arange(S5_STATE, dtype=jnp.float32)
    return {
        'x': nrm((BATCH, SEQ, D), 1.0),
        'c': nrm((BATCH, D), 1.0),
        'ctx': nrm((BATCH, CTX_LEN, D), 1.0),
        'c_ctx': nrm((D,), 1.0),
        'w_mod': nrm((L, D, 6 * D), 0.5 * D ** -0.5),
        'b_mod': nrm((L, 6 * D), 0.02),
        'g_norm_mix': 1.0 + nrm((L, D), 0.02),
        'g_norm_ffn': 1.0 + nrm((L, D), 0.02),
        'w_in': nrm((L, D, IN_DIM), D ** -0.5),
        'b_gates': gate_bias[None, :] + nrm((L, 4 * MLSTM_HEADS), 0.1),
        'w_conv_qk': nrm((L, 2, CONV_K, MLSTM_WIDTH), CONV_K ** -0.5),
        'g_mh': 1.0 + nrm((L, MLSTM_WIDTH), 0.02),
        'g_sgu': 1.0 + nrm((L, SGU_WIDTH), 0.02),
        'w_sgu': nrm((L, SGU_GROUPS, SGU_CHUNK, SGU_CHUNK), SGU_CHUNK ** -0.5),
        'b_sgu': 1.0 + nrm((L, SGU_GROUPS, SGU_CHUNK), 0.02),
        's5_a_re': -0.5 + nrm((L, 2, S5_GROUPS, S5_STATE), 0.01),
        's5_a_im': math.pi * n_idx + nrm((L, 2, S5_GROUPS, S5_STATE), 0.01),
        's5_log_dt': jax.random.uniform(next(ks), (L, 2, S5_GROUPS), jnp.float32,
                                        math.log(S5_DT_MIN), math.log(S5_DT_MAX)),
        's5_b_re': nrm((L, S5_GROUPS, S5_STATE, S5_GROUP_DIM), (2 * S5_GROUP_DIM) ** -0.5),
        's5_b_im': nrm((L, S5_GROUPS, S5_STATE, S5_GROUP_DIM), (2 * S5_GROUP_DIM) ** -0.5),
        's5_c_re': nrm((L, S5_GROUPS, S5_GROUP_DIM, S5_STATE), S5_STATE ** -0.5),
        's5_c_im': nrm((L, S5_GROUPS, S5_GROUP_DIM, S5_STATE), S5_STATE ** -0.5),
        's5_d': nrm((L, S5_WIDTH), 1.0),
        'w_glu': nrm((L, S5_WIDTH, S5_WIDTH), S5_WIDTH ** -0.5),
        'b_glu': nrm((L, S5_WIDTH), 0.02),
        'w_sconv': nrm((L, CONV_K, SCONV_WIDTH), CONV_K ** -0.5),
        'w_up_mlstm': nrm((L, MLSTM_WIDTH, D), MLSTM_WIDTH ** -0.5),
        'w_up_sgu': nrm((L, SGU_WIDTH, D), SGU_WIDTH ** -0.5),
        'w_up_s5': nrm((L, S5_WIDTH, D), S5_WIDTH ** -0.5),
        'w_up_sconv': nrm((L, SCONV_WIDTH, D), SCONV_WIDTH ** -0.5),
        'w_out': nrm((L, D, D), D ** -0.5),
        'w_ffn_gate': nrm((L, D, FFN_HIDDEN), D ** -0.5),
        'w_ffn_up': nrm((L, D, FFN_HIDDEN), D ** -0.5),
        'w_ffn_down': nrm((L, FFN_HIDDEN, D), FFN_HIDDEN ** -0.5),
        'g_final': 1.0 + nrm((D,), 0.02),
    }


def reference(x, c, ctx, c_ctx, w_mod, b_mod, g_norm_mix, g_norm_ffn, w_in, b_gates,
              w_conv_qk, g_mh, g_sgu, w_sgu, b_sgu, s5_a_re, s5_a_im, s5_log_dt,
              s5_b_re, s5_b_im, s5_c_re, s5_c_im, s5_d, w_glu, b_glu, w_sconv,
              w_up_mlstm, w_up_sgu, w_up_s5, w_up_sconv, w_out,
              w_ffn_gate, w_ffn_up, w_ffn_down, g_final):
    D = D_MODEL
    xc = ctx
    sc_x = jax.nn.silu(c)
    sc_c = jax.nn.silu(c_ctx)
    for l in range(DEPTH):
        last = l == DEPTH - 1
        mod_x = jnp.split((sc_x @ w_mod[l] + b_mod[l])[:, None, :], 6, axis=-1)
        n_mod = 2 if last else 6
        mod_c = jnp.split((sc_c @ w_mod[l][:, :n_mod * D] + b_mod[l][:n_mod * D])[None, None, :], n_mod, axis=-1)
        hx = modulate(x, g_norm_mix[l], mod_x[0], mod_x[1])
        hc = modulate(xc, g_norm_mix[l], mod_c[0], mod_c[1])
        yx, yc = token_mixer(hx, hc, not last, w_in[l], b_gates[l], w_conv_qk[l], g_mh[l], g_sgu[l],
                             w_sgu[l], b_sgu[l], s5_a_re[l], s5_a_im[l], s5_log_dt[l], s5_b_re[l],
                             s5_b_im[l], s5_c_re[l], s5_c_im[l], s5_d[l], w_glu[l], b_glu[l],
                             w_sconv[l], w_up_mlstm[l], w_up_sgu[l], w_up_s5[l], w_up_sconv[l], w_out[l])
        x = x + mod_x[2] * yx
        x = x + mod_x[5] * swiglu(modulate(x, g_norm_ffn[l], mod_x[3], mod_x[4]),
                                  w_ffn_gate[l], w_ffn_up[l], w_ffn_down[l])
        if not last:
            xc = xc + mod_c[2] * yc
            xc = xc + mod_c[5] * swiglu(modulate(xc, g_norm_ffn[l], mod_c[3], mod_c[4]),
                                        w_ffn_gate[l], w_ffn_up[l], w_ffn_down[l])
    return rmsnorm(x, g_final)
```

```python
import functools
import math

import jax
import jax.numpy as jnp
from jax import lax
from jax.experimental import pallas as pl
from jax.experimental.pallas import tpu as pltpu

F32 = jnp.float32
BF16 = jnp.bfloat16

D_MODEL = 1024
GRID_W = 64
N_BRANCH = 4
RMS_EPS = 1e-6
CONV_K = 3
HEADS = 4
HEAD_DIM = 128
MLSTM_WIDTH = HEADS * HEAD_DIM
MLSTM_CHUNK = 128
SGU_GROUPS = 4
SGU_GROUP_DIM = 128
SGU_WIDTH = SGU_GROUPS * SGU_GROUP_DIM
SGU_CHUNK = 128
S5_GROUP_DIM = 16
S5_GROUPS = 24
S5_WIDTH = S5_GROUPS * S5_GROUP_DIM
S5_STATE = 64
SCONV_WIDTH = 512
STATE_DIM = 3 * MLSTM_WIDTH + 4 * HEADS + S5_WIDTH
GATE_COL = 3 * MLSTM_WIDTH

LANES = 128
TOKEN_TILE = 256
S5_CHUNK = 16
S5_ROW = S5_CHUNK * S5_GROUP_DIM
S5_PAIR = 2 * S5_STATE
S5_SCAN_LANES = 1024
STATE_COLS = 3 * MLSTM_WIDTH + 2 * LANES + S5_WIDTH
U_COL = 3 * MLSTM_WIDTH + 2 * LANES
VMEM_LIMIT = 52 * 1024 * 1024


def _resident(shape):
    zeros = (0,) * len(shape)
    return pl.BlockSpec(shape, lambda *_: zeros, pipeline_mode=pl.Buffered(1))


def _params(*semantics):
    return pltpu.CompilerParams(dimension_semantics=semantics, vmem_limit_bytes=VMEM_LIMIT)


def _modulated_norm(x, gain, shift, scale):
    y = x * lax.rsqrt(jnp.mean(x * x, axis=-1, keepdims=True) + RMS_EPS) * gain
    return y * (1.0 + scale) + shift


def _split3(a):
    hi = a.astype(BF16)
    r1 = a - hi.astype(F32)
    mid = r1.astype(BF16)
    lo = (r1 - mid.astype(F32)).astype(BF16)
    return hi, mid, lo


def _conv3(a, w, first, last):
    n = a.shape[0]
    prev = jnp.where(first, 0.0, pltpu.roll(a, 1, 0))
    nxt = jnp.where(last, 0.0, pltpu.roll(a, n - 1, 0))
    return prev * w[0:1] + a * w[1:2] + nxt * w[2:3]


def _row_edges(tile_idx, n_rows, ctx_len):
    period = jnp.where(tile_idx == 0, ctx_len, GRID_W)
    pos = lax.broadcasted_iota(jnp.int32, (n_rows, 1), 0) & (period - 1)
    return pos == 0, pos == period - 1


def _mod_kernel(c_ref, w_ref, b_ref, o_ref):
    a = c_ref[...]
    a = a * jax.nn.sigmoid(a)
    a_hi = a.astype(BF16)
    a_lo = (a - a_hi.astype(F32)).astype(BF16)
    w = w_ref[0]
    w_hi = w.astype(BF16)
    w_lo = (w - w_hi.astype(F32)).astype(BF16)
    acc = jnp.dot(a_hi, w_hi, preferred_element_type=F32)
    acc = acc + jnp.dot(a_lo, w_hi, preferred_element_type=F32)
    acc = acc + jnp.dot(a_hi, w_lo, preferred_element_type=F32)
    o_ref[0] = acc + b_ref[0]


def _modulation(c_rows, w_mod, b_mod):
    depth, d, n = w_mod.shape
    tn = 1536
    return pl.pallas_call(
        _mod_kernel,
        out_shape=jax.ShapeDtypeStruct((depth, 8, n), F32),
        grid=(depth, n // tn),
        in_specs=[
            pl.BlockSpec((8, d), lambda l, j: (0, 0)),
            pl.BlockSpec((1, d, tn), lambda l, j: (l, 0, j)),
            pl.BlockSpec((1, 1, tn), lambda l, j: (l, 0, j)),
        ],
        out_specs=pl.BlockSpec((1, 8, tn), lambda l, j: (l, 0, j)),
        compiler_params=_params("arbitrary", "arbitrary"),
        name="modulation",
    )(c_rows, w_mod, b_mod.reshape(depth, 1, n))


def _state_proj_kernel(x_ref, mod_ref, gain_ref, w_ref, bg_ref, wc_ref,
                       q_ref, k_ref, v_ref, g_ref, u_ref, *, ctx_len):
    t = pl.program_id(1)
    mod = mod_ref[0, 0]
    h = _modulated_norm(x_ref[0], gain_ref[...], mod[0:1], mod[1:2])
    z = jnp.dot(h.astype(BF16), w_ref[...], preferred_element_type=F32)
    first, last = _row_edges(t, z.shape[0], ctx_len)
    wc = wc_ref[...]
    w = MLSTM_WIDTH
    q = _conv3(z[:, 0:w], wc[0:3], first, last)
    k = _conv3(z[:, w:2 * w], wc[3:6], first, last)
    q_ref[0] = (q * jax.nn.sigmoid(q)).astype(BF16)
    k_ref[0] = ((k * jax.nn.sigmoid(k)) * (HEAD_DIM ** -0.5)).astype(BF16)
    v_ref[0] = z[:, 2 * w:3 * w].astype(BF16)
    g_ref[0] = z[:, GATE_COL:GATE_COL + 2 * LANES] + bg_ref[...]
    u_ref[0] = z[:, U_COL:U_COL + S5_WIDTH].astype(BF16)


def _state_proj(xs, mods, gain, w_state, b_gates, w_conv, ctx_len):
    b, s, d = xs.shape
    tm = TOKEN_TILE
    tok = lambda width: pl.BlockSpec((1, tm, width), lambda i, t: (i, t, 0))
    return pl.pallas_call(
        functools.partial(_state_proj_kernel, ctx_len=ctx_len),
        out_shape=(
            jax.ShapeDtypeStruct((b, s, MLSTM_WIDTH), BF16),
            jax.ShapeDtypeStruct((b, s, MLSTM_WIDTH), BF16),
            jax.ShapeDtypeStruct((b, s, MLSTM_WIDTH), BF16),
            jax.ShapeDtypeStruct((b, s, 2 * LANES), F32),
            jax.ShapeDtypeStruct((b, s, S5_WIDTH), BF16),
        ),
        grid=(b, s // tm),
        in_specs=[
            tok(d),
            pl.BlockSpec((1, 1, 6, d), lambda i, t: (i, jnp.minimum(t, 1), 0, 0)),
            _resident((1, d)),
            _resident((d, STATE_COLS)),
            _resident((1, 2 * LANES)),
            _resident((2 * CONV_K, MLSTM_WIDTH)),
        ],
        out_specs=(tok(MLSTM_WIDTH), tok(MLSTM_WIDTH), tok(MLSTM_WIDTH), tok(2 * LANES), tok(S5_WIDTH)),
        compiler_params=_params("arbitrary", "arbitrary"),
        name="state_proj",
    )(xs, mods, gain, w_state, b_gates, w_conv)


def _mlstm_kernel(q_ref, k_ref, v_ref, g_ref, h_ref, c_ref, n_ref, m_ref):
    direction = pl.program_id(1)
    step = pl.program_id(2)
    lc = MLSTM_CHUNK

    @pl.when(step == 0)
    def _():
        c_ref[...] = jnp.zeros_like(c_ref)
        n_ref[...] = jnp.zeros_like(n_ref)
        m_ref[...] = jnp.zeros_like(m_ref)

    row = lax.broadcasted_iota(jnp.int32, (lc, lc), 0)
    col = lax.broadcasted_iota(jnp.int32, (lc, lc), 1)
    tri = (row - col) * (1 - 2 * direction) >= 0
    tri_b = jnp.where(tri, 1.0, 0.0).astype(BF16)

    g = g_ref[0]
    lf = jax.nn.log_sigmoid(g)
    hi, mid, lo = _split3(lf)
    bcum = (jnp.dot(tri_b, hi, preferred_element_type=F32)
            + jnp.dot(tri_b, mid, preferred_element_type=F32)
            + jnp.dot(tri_b, lo, preferred_element_type=F32))
    total = jnp.sum(lf, axis=0, keepdims=True)
    g_t = g.T
    bcum_t = bcum.T

    q = q_ref[0]
    k = k_ref[0]
    v = v_ref[0]
    for hd in range(HEADS):
        lanes = slice(hd * HEAD_DIM, (hd + 1) * HEAD_DIM)
        qh, kh, vh = q[:, lanes], k[:, lanes], v[:, lanes]
        i_col = g[:, hd:hd + 1]
        b_col = bcum[:, HEADS + hd:HEADS + hd + 1]
        i_row = g_t[hd:hd + 1, :]
        b_row = bcum_t[HEADS + hd:HEADS + hd + 1, :]
        b_last = total[:, HEADS + hd:HEADS + hd + 1]
        m = m_ref[hd][:, 0:1]
        c_mat = c_ref[hd]
        n_vec = n_ref[hd]

        log_s = b_last - b_col + i_col
        m_new = jnp.maximum(b_last + m, jnp.max(log_s, axis=0, keepdims=True))
        w_s = jnp.exp(log_s - m_new)
        decay = jnp.exp(b_last + m - m_new)

        log_w = jnp.where(tri, b_col - b_row + i_row, -jnp.inf)
        log_inter = b_col + m
        m_t = jnp.maximum(log_inter, jnp.max(log_w, axis=1, keepdims=True))
        inter = jnp.exp(log_inter - m_t)
        qk = lax.dot_general(qh, kh, (((1,), (1,)), ((), ())), preferred_element_type=F32)
        s_mat = qk * jnp.exp(log_w - m_t)
        num = inter * jnp.dot(qh, c_mat.astype(BF16), preferred_element_type=F32)
        num = num + jnp.dot(s_mat.astype(BF16), vh, preferred_element_type=F32)
        qn = jnp.sum(qh.astype(F32) * n_vec, axis=1, keepdims=True)
        den = inter * qn + jnp.sum(s_mat, axis=1, keepdims=True)
        h_ref[0, 0, :, lanes] = num / jnp.maximum(jnp.abs(den), jnp.exp(-m_t))

        wk = w_s * kh.astype(F32)
        c_ref[hd] = decay * c_mat + jnp.dot(wk.T.astype(BF16), vh, preferred_element_type=F32)
        n_ref[hd] = decay * n_vec + jnp.sum(wk, axis=0, keepdims=True)
        m_ref[hd] = jnp.broadcast_to(m_new, (1, LANES))


def _mlstm(q, k, v, gates, ctx_len):
    b, s, w = q.shape
    lc = MLSTM_CHUNK
    n_chunks = s // lc
    n_ctx = ctx_len // lc

    def chunk(d, j):
        back = jnp.where(j < n_ctx, n_ctx - 1 - j, n_chunks + n_ctx - 1 - j)
        return jnp.where(d == 0, j, back)

    tok = pl.BlockSpec((1, lc, w), lambda i, d, j: (i, chunk(d, j), 0))
    return pl.pallas_call(
        _mlstm_kernel,
        out_shape=jax.ShapeDtypeStruct((2, b, s, w), F32),
        grid=(b, 2, n_chunks),
        in_specs=[tok, tok, tok,
                  pl.BlockSpec((1, lc, LANES), lambda i, d, j: (i, chunk(d, j), d))],
        out_specs=pl.BlockSpec((1, 1, lc, w), lambda i, d, j: (d, i, chunk(d, j), 0)),
        scratch_shapes=[
            pltpu.VMEM((HEADS, HEAD_DIM, HEAD_DIM), F32),
            pltpu.VMEM((HEADS, 1, HEAD_DIM), F32),
            pltpu.VMEM((HEADS, 1, LANES), F32),
        ],
        compiler_params=_params("arbitrary", "arbitrary", "arbitrary"),
        name="mlstm",
    )(q, k, v, gates)


def _s5_operators(a_re, a_im, log_dt, b_re, b_im, c_re, c_im, d_skip):
    hp = lax.Precision.HIGHEST
    lch = S5_CHUNK
    a_re = a_re.astype(F32)
    a_im = a_im.astype(F32)
    dt = jnp.exp(log_dt.astype(F32))[..., None]
    la_re, la_im = dt * a_re, dt * a_im
    mag = jnp.exp(la_re)
    ab_re, ab_im = mag * jnp.cos(la_im), mag * jnp.sin(la_im)
    nr, ni = ab_re - 1.0, ab_im
    den = a_re * a_re + a_im * a_im
    f_re = ((nr * a_re + ni * a_im) / den)[..., None]
    f_im = ((ni * a_re - nr * a_im) / den)[..., None]
    b_re = b_re.astype(F32)
    b_im = b_im.astype(F32)
    bb_re = f_re * b_re - f_im * b_im
    bb_im = f_re * b_im + f_im * b_re
    c_re = c_re.astype(F32)
    c_im = c_im.astype(F32)

    def a_pow(n):
        n = n.astype(F32)[:, None, None, None]
        mg = jnp.exp(n * la_re)
        return mg * jnp.cos(n * la_im), mg * jnp.sin(n * la_im)

    steps = jnp.arange(lch + 1)
    p_re, p_im = a_pow(steps)
    ab_b_re = p_re[..., None] * bb_re - p_im[..., None] * bb_im
    ab_b_im = p_re[..., None] * bb_im + p_im[..., None] * bb_re
    resp = (jnp.einsum('gjp,ndgpk->ndgjk', c_re, ab_b_re, precision=hp)
            - jnp.einsum('gjp,ndgpk->ndgjk', c_im, ab_b_im, precision=hp))
    src = jnp.arange(lch)[:, None]
    tgt = jnp.arange(lch)[None, :]
    lag_f = jnp.clip(tgt - src, 0, lch)
    lag_b = jnp.clip(src - tgt, 0, lch)
    k_f = jnp.where((tgt >= src)[..., None, None, None], resp[lag_f, 0], 0.0)
    k_b = jnp.where((src >= tgt)[..., None, None, None], resp[lag_b, 1], 0.0)
    skip = (jnp.eye(lch, dtype=F32)[:, :, None, None, None]
            * (d_skip.astype(F32).reshape(S5_GROUPS, S5_GROUP_DIM)[:, :, None]
               * jnp.eye(S5_GROUP_DIM, dtype=F32)[None]))
    toeplitz = jnp.transpose(k_f + k_b + skip, (2, 0, 4, 1, 3))
    toeplitz = toeplitz.reshape(S5_GROUPS, S5_ROW, S5_ROW)

    def to_state(re, im):
        re = jnp.transpose(re, (1, 0, 3, 2)).reshape(S5_GROUPS, S5_ROW, S5_STATE)
        im = jnp.transpose(im, (1, 0, 3, 2)).reshape(S5_GROUPS, S5_ROW, S5_STATE)
        return jnp.concatenate([re, im], axis=-1)
    w_f = to_state(ab_b_re[lch - 1::-1, 0][:lch], ab_b_im[lch - 1::-1, 0][:lch])
    w_b = to_state(ab_b_re[:lch, 1], ab_b_im[:lch, 1])
    w_in = jnp.concatenate([w_f, w_b], axis=-1)

    def from_state(pr, pi):
        ca_re = c_re[None] * pr[:, :, None, :] - c_im[None] * pi[:, :, None, :]
        ca_im = c_re[None] * pi[:, :, None, :] + c_im[None] * pr[:, :, None, :]
        re = jnp.transpose(ca_re, (1, 3, 0, 2)).reshape(S5_GROUPS, S5_STATE, S5_ROW)
        im = jnp.transpose(-ca_im, (1, 3, 0, 2)).reshape(S5_GROUPS, S5_STATE, S5_ROW)
        return jnp.concatenate([re, im], axis=1)
    v_f = from_state(p_re[1:, 0], p_im[1:, 0])
    v_b = from_state(p_re[lch:0:-1, 1], p_im[lch:0:-1, 1])
    v_out = jnp.stack([v_f, v_b])

    al_re, al_im = p_re[lch], p_im[lch]
    a_same = jnp.concatenate([al_re, al_re], axis=-1).reshape(2, 1, S5_GROUPS * S5_PAIR)
    a_cross = jnp.concatenate([-al_im, al_im], axis=-1).reshape(2, 1, S5_GROUPS * S5_PAIR)
    a_step = jnp.concatenate([a_same, a_cross], axis=1)
    return toeplitz.astype(BF16), w_in.astype(BF16), v_out.astype(BF16), a_step


def _s5_chunk_state_kernel(u_ref, w_ref, xf_ref, xb_ref):
    x = jnp.dot(u_ref[0], w_ref[0], preferred_element_type=F32)
    xf_ref[...] = x[:, :S5_PAIR]
    xb_ref[...] = x[:, S5_PAIR:]


def _s5_chunk_states(u_rows, w_in):
    g, rows, _ = u_rows.shape
    out = jax.ShapeDtypeStruct((rows, g * S5_PAIR), F32)
    col = pl.BlockSpec((rows, S5_PAIR), lambda i: (0, i))
    return pl.pallas_call(
        _s5_chunk_state_kernel,
        out_shape=(out, out),
        grid=(g,),
        in_specs=[pl.BlockSpec((1, rows, S5_ROW), lambda i: (i, 0, 0)),
                  pl.BlockSpec((1, S5_ROW, 2 * S5_PAIR), lambda i: (i, 0, 0))],
        out_specs=(col, col),
        compiler_params=_params("arbitrary"),
        name="s5_chunk_states",
    )(u_rows, w_in)


def _s5_scan_kernel(xf_ref, xb_ref, a_ref, of_ref, ob_ref, sf_ref, sb_ref, *, n_ctx):
    n_chunks, n_rows, n_lanes = xf_ref.shape
    half = S5_STATE
    lane = lax.broadcasted_iota(jnp.int32, (n_rows, n_lanes), 1)
    low_half = (lane & (S5_PAIR - 1)) < half

    def swap_halves(a):
        return jnp.where(low_half, pltpu.roll(a, n_lanes - half, 1), pltpu.roll(a, half, 1))

    def swap_body(i, carry):
        sf_ref[i] = swap_halves(xf_ref[i])
        sb_ref[i] = swap_halves(xb_ref[i])
        return carry

    lax.fori_loop(0, n_chunks, swap_body, 0)
    af_same, af_cross = a_ref[0, 0:1], a_ref[0, 1:2]
    ab_same, ab_cross = a_ref[1, 0:1], a_ref[1, 1:2]
    zero = jnp.zeros(xf_ref.shape[1:], F32)

    def body(i, carry):
        xf, xf_sw, xb, xb_sw = carry
        cb = jnp.where(i < n_ctx, n_ctx - 1 - i, n_chunks + n_ctx - 1 - i)
        of_ref[i] = xf
        ob_ref[cb] = xb
        nf = af_same * xf + af_cross * xf_sw + xf_ref[i]
        nf_sw = af_same * xf_sw - af_cross * xf + sf_ref[i]
        nb = ab_same * xb + ab_cross * xb_sw + xb_ref[cb]
        nb_sw = ab_same * xb_sw - ab_cross * xb + sb_ref[cb]
        return nf, nf_sw, nb, nb_sw

    lax.fori_loop(0, n_chunks, body, (zero, zero, zero, zero))


def _s5_scan(xc_f, xc_b, a_step, n_ctx):
    n_chunks, b, lanes = xc_f.shape
    tl = S5_SCAN_LANES
    blk = pl.BlockSpec((n_chunks, b, tl), lambda i: (0, 0, i))
    out = jax.ShapeDtypeStruct(xc_f.shape, F32)
    return pl.pallas_call(
        functools.partial(_s5_scan_kernel, n_ctx=n_ctx),
        out_shape=(out, out),
        grid=(lanes // tl,),
        in_specs=[blk, blk, pl.BlockSpec((2, 2, tl), lambda i: (0, 0, i))],
        out_specs=(blk, blk),
        scratch_shapes=[pltpu.VMEM((n_chunks, b, tl), F32), pltpu.VMEM((n_chunks, b, tl), F32)],
        compiler_params=_params("arbitrary"),
        name="s5_scan",
    )(xc_f, xc_b, a_step)


def _s5_output_kernel(u_ref, t_ref, xf_ref, xb_ref, v_ref, y_ref):
    y = jnp.dot(u_ref[0], t_ref[0], preferred_element_type=F32)
    y = y + jnp.dot(xf_ref[...].astype(BF16), v_ref[0, 0], preferred_element_type=F32)
    y = y + jnp.dot(xb_ref[...].astype(BF16), v_ref[1, 0], preferred_element_type=F32)
    y_ref[0] = y


def _s5_output(u_rows, toeplitz, xin_f, xin_b, v_out):
    g, rows, _ = u_rows.shape
    col = pl.BlockSpec((rows, S5_PAIR), lambda i: (0, i))
    return pl.pallas_call(
        _s5_output_kernel,
        out_shape=jax.ShapeDtypeStruct((g, rows, S5_ROW), F32),
        grid=(g,),
        in_specs=[pl.BlockSpec((1, rows, S5_ROW), lambda i: (i, 0, 0)),
                  pl.BlockSpec((1, S5_ROW, S5_ROW), lambda i: (i, 0, 0)),
                  col, col,
                  pl.BlockSpec((2, 1, S5_PAIR, S5_ROW), lambda i: (0, i, 0, 0))],
        out_specs=pl.BlockSpec((1, rows, S5_ROW), lambda i: (i, 0, 0)),
        compiler_params=_params("arbitrary"),
        name="s5_output",
    )(u_rows, toeplitz, xin_f, xin_b, v_out)


def _s5_mixer(u, ops, ctx_len):
    toeplitz, w_in, v_out, a_step = ops
    b, s, _ = u.shape
    n_chunks = s // S5_CHUNK
    u_rows = u.reshape(b, n_chunks, S5_CHUNK, S5_GROUPS, S5_GROUP_DIM)
    u_rows = jnp.transpose(u_rows, (3, 1, 0, 2, 4)).reshape(S5_GROUPS, n_chunks * b, S5_ROW)
    xc_f, xc_b = _s5_chunk_states(u_rows, w_in)
    shape3 = (n_chunks, b, S5_GROUPS * S5_PAIR)
    xin_f, xin_b = _s5_scan(xc_f.reshape(shape3), xc_b.reshape(shape3), a_step, ctx_len // S5_CHUNK)
    y_rows = _s5_output(u_rows, toeplitz, xin_f.reshape(xc_f.shape), xin_b.reshape(xc_b.shape), v_out)
    y = y_rows.reshape(S5_GROUPS, n_chunks, b, S5_CHUNK, S5_GROUP_DIM)
    return jnp.transpose(y, (2, 1, 3, 0, 4)).reshape(b, s, S5_WIDTH)


def _merge_kernel(x_ref, mod_ref, gain_ref, hf_ref, hb_ref, ys_ref, wo_ref, gmh_ref, gsgu_ref,
                  wsgu_ref, bsgu_ref, wglu_ref, bglu_ref, wsc_ref, wum_ref, wug_ref, wus_ref,
                  wuc_ref, wout_ref, o_ref, *, ctx_len, tile0):
    t = pl.program_id(1) + tile0
    x = x_ref[0]
    mod = mod_ref[0, 0]
    h = _modulated_norm(x, gain_ref[...], mod[0:1], mod[1:2]).astype(BF16)
    tm = x.shape[0]

    def proj(idx, width=MLSTM_WIDTH):
        start = idx * MLSTM_WIDTH
        return jnp.dot(h, wo_ref[:, start:start + width], preferred_element_type=F32)

    def gate(j):
        start = 6 * MLSTM_WIDTH + j * D_MODEL
        return jax.nn.sigmoid(jnp.dot(h, wo_ref[:, start:start + D_MODEL], preferred_element_type=F32))

    hm = hf_ref[0, 0] + hb_ref[0, 0]
    parts = []
    for hd in range(HEADS):
        hh = hm[:, hd * HEAD_DIM:(hd + 1) * HEAD_DIM]
        parts.append(hh * lax.rsqrt(jnp.mean(hh * hh, axis=-1, keepdims=True) + RMS_EPS))
    y_a = (jnp.concatenate(parts, axis=-1) * gmh_ref[...]) * jax.nn.sigmoid(proj(0))
    acc = gate(0) * jnp.dot(y_a.astype(BF16), wum_ref[...], preferred_element_type=F32)

    su = jax.nn.gelu(proj(1))
    sv = jax.nn.gelu(proj(2))
    mu = jnp.mean(sv, axis=-1, keepdims=True)
    cen = sv - mu
    var = jnp.mean(cen * cen, axis=-1, keepdims=True)
    vn = (cen * lax.rsqrt(var + RMS_EPS) * gsgu_ref[...]).astype(BF16)
    bias = bsgu_ref[...]
    rows = []
    for n in range(tm // SGU_CHUNK):
        cols = []
        for gi in range(SGU_GROUPS):
            blk = vn[n * SGU_CHUNK:(n + 1) * SGU_CHUNK, gi * SGU_GROUP_DIM:(gi + 1) * SGU_GROUP_DIM]
            cols.append(jnp.dot(wsgu_ref[gi], blk, preferred_element_type=F32) + bias[:, gi:gi + 1])
        rows.append(jnp.concatenate(cols, axis=-1))
    y_b = su * jnp.concatenate(rows, axis=0)
    acc = acc + gate(1) * jnp.dot(y_b.astype(BF16), wug_ref[...], preferred_element_type=F32)

    ys = jax.nn.gelu(ys_ref[0])
    y_c = ys * jax.nn.sigmoid(jnp.dot(ys.astype(BF16), wglu_ref[...], preferred_element_type=F32)
                              + bglu_ref[...])
    acc = acc + gate(2) * jnp.dot(y_c.astype(BF16), wus_ref[...], preferred_element_type=F32)

    first, last = _row_edges(t, tm, ctx_len)
    y_d = proj(3) * _conv3(proj(4) * proj(5), wsc_ref[...], first, last)
    acc = acc + gate(3) * jnp.dot(y_d.astype(BF16), wuc_ref[...], preferred_element_type=F32)

    o_ref[0] = x + mod[2:3] * jnp.dot(acc.astype(BF16), wout_ref[...], preferred_element_type=F32)


def _merge(xs, mods, gain, h_dirs, y_s5, lw, ctx_len, tile0):
    b, s, d = xs.shape
    tm = TOKEN_TILE
    tok = lambda width: pl.BlockSpec((1, tm, width), lambda i, t: (i, t + tile0, 0))
    hdir = lambda direction: pl.BlockSpec((1, 1, tm, MLSTM_WIDTH),
                                          lambda i, t: (direction, i, t + tile0, 0))
    weights = (lw['w_out_side'], lw['g_mh'], lw['g_sgu'], lw['w_sgu'], lw['b_sgu_t'], lw['w_glu'],
               lw['b_glu'], lw['w_sconv'], lw['w_up_mlstm'], lw['w_up_sgu'], lw['w_up_s5'],
               lw['w_up_sconv'], lw['w_out'])
    return pl.pallas_call(
        functools.partial(_merge_kernel, ctx_len=ctx_len, tile0=tile0),
        out_shape=jax.ShapeDtypeStruct((b, s, d), F32),
        grid=(b, s // tm - tile0),
        in_specs=[tok(d),
                  pl.BlockSpec((1, 1, 6, d), lambda i, t: (i, jnp.minimum(t + tile0, 1), 0, 0)),
                  _resident((1, d)), hdir(0), hdir(1), tok(S5_WIDTH)]
                 + [_resident(w.shape) for w in weights],
        out_specs=tok(d),
        compiler_params=_params("arbitrary", "arbitrary"),
        name="merge",
    )(xs, mods, gain, h_dirs, h_dirs, y_s5, *weights)


def _ffn_kernel(x_ref, mod_ref, gain_ref, wg_ref, wu_ref, wd_ref, gf_ref, o_ref, *, final):
    x = x_ref[0]
    mod = mod_ref[0, 0]
    h = _modulated_norm(x, gain_ref[...], mod[3:4], mod[4:5]).astype(BF16)
    a = jnp.dot(h, wg_ref[...], preferred_element_type=F32)
    b = jnp.dot(h, wu_ref[...], preferred_element_type=F32)
    hid = ((a * jax.nn.sigmoid(a)) * b).astype(BF16)
    y = x + mod[5:6] * jnp.dot(hid, wd_ref[...], preferred_element_type=F32)
    if final:
        y = y * lax.rsqrt(jnp.mean(y * y, axis=-1, keepdims=True) + RMS_EPS) * gf_ref[...]
    o_ref[0] = y


def _ffn(xs, mods, gain, w_gate, w_up, w_down, g_final, tile0, final):
    b, s, d = xs.shape
    tm = TOKEN_TILE
    n_tiles = s // tm - tile0
    out_tile0 = 0 if final else tile0
    return pl.pallas_call(
        functools.partial(_ffn_kernel, final=final),
        out_shape=jax.ShapeDtypeStruct((b, (n_tiles + out_tile0) * tm, d), F32),
        grid=(b, n_tiles),
        in_specs=[pl.BlockSpec((1, tm, d), lambda i, t: (i, t + tile0, 0)),
                  pl.BlockSpec((1, 1, 6, d), lambda i, t: (i, jnp.minimum(t + tile0, 1), 0, 0)),
                  _resident((1, d)), _resident(w_gate.shape), _resident(w_up.shape),
                  _resident(w_down.shape), _resident((1, d))],
        out_specs=pl.BlockSpec((1, tm, d), lambda i, t: (i, t + out_tile0, 0)),
        compiler_params=_params("arbitrary", "arbitrary"),
        name="ffn_final" if final else "ffn",
    )(xs, mods, gain, w_gate, w_up, w_down, g_final)


def _layer_weights(l, w_in, b_gates, w_conv_qk, g_mh, g_sgu, w_sgu, b_sgu, w_glu, b_glu, w_sconv,
                   w_up_mlstm, w_up_sgu, w_up_s5, w_up_sconv, w_out, w_ffn_gate, w_ffn_up, w_ffn_down):
    w = w_in[l]
    gate_pad = ((0, 0), (0, LANES - 2 * HEADS))
    w_state = jnp.concatenate([
        w[:, :GATE_COL],
        jnp.pad(w[:, GATE_COL:GATE_COL + 2 * HEADS], gate_pad),
        jnp.pad(w[:, GATE_COL + 2 * HEADS:GATE_COL + 4 * HEADS], gate_pad),
        w[:, GATE_COL + 4 * HEADS:STATE_DIM]], axis=1).astype(BF16)
    bg = b_gates[l]
    b_pad = (0, LANES - 2 * HEADS)
    bg = jnp.concatenate([jnp.pad(bg[:2 * HEADS], b_pad), jnp.pad(bg[2 * HEADS:], b_pad)])
    return dict(
        w_state=w_state,
        b_gates=bg.reshape(1, 2 * LANES),
        w_conv=w_conv_qk[l].reshape(2 * CONV_K, MLSTM_WIDTH),
        w_out_side=w[:, STATE_DIM:].astype(BF16),
        g_mh=g_mh[l].reshape(1, -1),
        g_sgu=g_sgu[l].reshape(1, -1),
        w_sgu=w_sgu[l].astype(BF16),
        b_sgu_t=b_sgu[l].T,
        w_glu=w_glu[l].astype(BF16),
        b_glu=b_glu[l].reshape(1, -1),
        w_sconv=w_sconv[l],
        w_up_mlstm=w_up_mlstm[l].astype(BF16),
        w_up_sgu=w_up_sgu[l].astype(BF16),
        w_up_s5=w_up_s5[l].astype(BF16),
        w_up_sconv=w_up_sconv[l].astype(BF16),
        w_out=w_out[l].astype(BF16),
        w_ffn_gate=w_ffn_gate[l].astype(BF16),
        w_ffn_up=w_ffn_up[l].astype(BF16),
        w_ffn_down=w_ffn_down[l].astype(BF16),
    )


def kernel(x, c, ctx, c_ctx, w_mod, b_mod, g_norm_mix, g_norm_ffn, w_in, b_gates, w_conv_qk, g_mh, g_sgu, w_sgu, b_sgu, s5_a_re, s5_a_im, s5_log_dt, s5_b_re, s5_b_im, s5_c_re, s5_c_im, s5_d, w_glu, b_glu, w_sconv, w_up_mlstm, w_up_sgu, w_up_s5, w_up_sconv, w_out, w_ffn_gate, w_ffn_up, w_ffn_down, g_final):
    batch, seq, d = x.shape
    ctx_len = ctx.shape[1]
    depth = w_mod.shape[0]
    assert d == D_MODEL and ctx_len == TOKEN_TILE and seq % TOKEN_TILE == 0 and batch <= 7
    ctx_tiles = ctx_len // TOKEN_TILE

    c_rows = jnp.zeros((8, d), F32).at[:batch].set(c).at[batch].set(c_ctx)
    mod_all = _modulation(c_rows, w_mod, b_mod).reshape(depth, 8, 6, d)
    xs = jnp.concatenate([ctx, x], axis=1)

    out = None
    for l in range(depth):
        last = l == depth - 1
        mod_x = mod_all[l, :batch]
        mod_c = jnp.broadcast_to(mod_all[l, batch][None], (batch, 6, d))
        mods = jnp.stack([mod_c, mod_x], axis=1)
        lw = _layer_weights(l, w_in, b_gates, w_conv_qk, g_mh, g_sgu, w_sgu, b_sgu, w_glu, b_glu,
                            w_sconv, w_up_mlstm, w_up_sgu, w_up_s5, w_up_sconv, w_out,
                            w_ffn_gate, w_ffn_up, w_ffn_down)
        s5_ops = _s5_operators(s5_a_re[l], s5_a_im[l], s5_log_dt[l], s5_b_re[l], s5_b_im[l],
                               s5_c_re[l], s5_c_im[l], s5_d[l])
        gain_mix = g_norm_mix[l].reshape(1, d)
        gain_ffn = g_norm_ffn[l].reshape(1, d)

        q, k, v, gates, u = _state_proj(xs, mods, gain_mix, lw['w_state'], lw['b_gates'],
                                        lw['w_conv'], ctx_len)
        h_dirs = _mlstm(q, k, v, gates, ctx_len)
        y_s5 = _s5_mixer(u, s5_ops, ctx_len)
        tile0 = ctx_tiles if last else 0
        x_mid = _merge(xs, mods, gain_mix, h_dirs, y_s5, lw, ctx_len, tile0)
        xs = _ffn(x_mid, mods, gain_ffn, lw['w_ffn_gate'], lw['w_ffn_up'], lw['w_ffn_down'],
                  g_final.reshape(1, d), tile0, last)
        out = xs
    return out
```

```python
import functools
import math

import jax
import jax.numpy as jnp
from jax import lax
from jax.experimental import pallas as pl
from jax.experimental.pallas import tpu as pltpu

F32 = jnp.float32
BF16 = jnp.bfloat16

D_MODEL = 1024
GRID_W = 64
N_BRANCH = 4
RMS_EPS = 1e-6
CONV_K = 3
HEADS = 4
HEAD_DIM = 128
MLSTM_WIDTH = HEADS * HEAD_DIM
MLSTM_CHUNK = 128
SGU_GROUPS = 4
SGU_GROUP_DIM = 128
SGU_WIDTH = SGU_GROUPS * SGU_GROUP_DIM
SGU_CHUNK = 128
S5_GROUP_DIM = 16
S5_GROUPS = 24
S5_WIDTH = S5_GROUPS * S5_GROUP_DIM
S5_STATE = 64
SCONV_WIDTH = 512
STATE_DIM = 3 * MLSTM_WIDTH + 4 * HEADS + S5_WIDTH
GATE_COL = 3 * MLSTM_WIDTH

LANES = 128
TOKEN_TILE = 256
S5_CHUNK = 16
S5_ROW = S5_CHUNK * S5_GROUP_DIM
S5_PAIR = 2 * S5_STATE
S5_SCAN_PAIRS = 2
S5_BATCH = 4
STATE_COLS = 3 * MLSTM_WIDTH + 2 * LANES + S5_WIDTH
U_COL = 3 * MLSTM_WIDTH + 2 * LANES
VMEM_LIMIT = 52 * 1024 * 1024


def _resident(shape):
    zeros = (0,) * len(shape)
    return pl.BlockSpec(shape, lambda *_: zeros, pipeline_mode=pl.Buffered(1))


def _params(*semantics):
    return pltpu.CompilerParams(dimension_semantics=semantics, vmem_limit_bytes=VMEM_LIMIT)


def _modulated_norm(x, gain, shift, scale):
    y = x * lax.rsqrt(jnp.mean(x * x, axis=-1, keepdims=True) + RMS_EPS) * gain
    return y * (1.0 + scale) + shift


def _split3(a):
    hi = a.astype(BF16)
    r1 = a - hi.astype(F32)
    mid = r1.astype(BF16)
    lo = (r1 - mid.astype(F32)).astype(BF16)
    return hi, mid, lo


def _conv3(a, w, first, last):
    n = a.shape[0]
    prev = jnp.where(first, 0.0, pltpu.roll(a, 1, 0))
    nxt = jnp.where(last, 0.0, pltpu.roll(a, n - 1, 0))
    return prev * w[0:1] + a * w[1:2] + nxt * w[2:3]


def _row_edges(tile_idx, n_rows, ctx_len):
    period = jnp.where(tile_idx == 0, ctx_len, GRID_W)
    pos = lax.broadcasted_iota(jnp.int32, (n_rows, 1), 0) & (period - 1)
    return pos == 0, pos == period - 1


def _mod_kernel(c_ref, w_ref, b_ref, o_ref):
    a = c_ref[...]
    a = a * jax.nn.sigmoid(a)
    a_hi = a.astype(BF16)
    a_lo = (a - a_hi.astype(F32)).astype(BF16)
    w = w_ref[0]
    w_hi = w.astype(BF16)
    w_lo = (w - w_hi.astype(F32)).astype(BF16)
    acc = jnp.dot(a_hi, w_hi, preferred_element_type=F32)
    acc = acc + jnp.dot(a_lo, w_hi, preferred_element_type=F32)
    acc = acc + jnp.dot(a_hi, w_lo, preferred_element_type=F32)
    o_ref[0] = acc + b_ref[0]


def _modulation(c_rows, w_mod, b_mod):
    depth, d, n = w_mod.shape
    tn = 1536
    return pl.pallas_call(
        _mod_kernel,
        out_shape=jax.ShapeDtypeStruct((depth, 8, n), F32),
        grid=(depth, n // tn),
        in_specs=[
            pl.BlockSpec((8, d), lambda l, j: (0, 0)),
            pl.BlockSpec((1, d, tn), lambda l, j: (l, 0, j)),
            pl.BlockSpec((1, 1, tn), lambda l, j: (l, 0, j)),
        ],
        out_specs=pl.BlockSpec((1, 8, tn), lambda l, j: (l, 0, j)),
        compiler_params=_params("arbitrary", "arbitrary"),
        name="modulation",
    )(c_rows, w_mod, b_mod.reshape(depth, 1, n))


def _to_group_rows(z, u_ref, buf_ref):
    n_chunks = z.shape[0] // S5_CHUNK
    per_block = LANES // S5_GROUP_DIM
    for gb in range(S5_WIDTH // LANES):
        buf_ref[gb] = z[:, gb * LANES:(gb + 1) * LANES]
    for s in range(S5_CHUNK):
        for gb in range(S5_WIDTH // LANES):
            zs = buf_ref[gb, pl.ds(s, n_chunks, stride=S5_CHUNK), :]
            for gl in range(per_block):
                u_ref[gb * per_block + gl, :, s * S5_GROUP_DIM:(s + 1) * S5_GROUP_DIM] = (
                    zs[:, gl * S5_GROUP_DIM:(gl + 1) * S5_GROUP_DIM])


def _from_group_rows(y_ref, buf_ref):
    n_chunks = y_ref.shape[1]
    per_block = LANES // S5_GROUP_DIM
    for s in range(S5_CHUNK):
        for gb in range(S5_WIDTH // LANES):
            pieces = [y_ref[gb * per_block + gl, :, s * S5_GROUP_DIM:(s + 1) * S5_GROUP_DIM]
                      for gl in range(per_block)]
            buf_ref[gb, pl.ds(s, n_chunks, stride=S5_CHUNK), :] = jnp.concatenate(pieces, axis=-1)
    return jnp.concatenate([buf_ref[gb] for gb in range(S5_WIDTH // LANES)], axis=-1)


def _state_proj_kernel(x_ref, mod_ref, gain_ref, w_ref, bg_ref, wc_ref,
                       q_ref, k_ref, v_ref, g_ref, u_ref, ubuf_ref, *, ctx_len):
    t = pl.program_id(1)
    mod = mod_ref[0, 0]
    h = _modulated_norm(x_ref[0], gain_ref[...], mod[0:1], mod[1:2])
    z = jnp.dot(h.astype(BF16), w_ref[...], preferred_element_type=F32)
    first, last = _row_edges(t, z.shape[0], ctx_len)
    wc = wc_ref[...]
    w = MLSTM_WIDTH
    q = _conv3(z[:, 0:w], wc[0:3], first, last)
    k = _conv3(z[:, w:2 * w], wc[3:6], first, last)
    q_ref[0] = (q * jax.nn.sigmoid(q)).astype(BF16)
    k_ref[0] = ((k * jax.nn.sigmoid(k)) * (HEAD_DIM ** -0.5)).astype(BF16)
    v_ref[0] = z[:, 2 * w:3 * w].astype(BF16)
    g_ref[0] = z[:, GATE_COL:GATE_COL + 2 * LANES] + bg_ref[...]
    _to_group_rows(z[:, U_COL:U_COL + S5_WIDTH], u_ref, ubuf_ref)


def _state_proj(xs, mods, gain, w_state, b_gates, w_conv, ctx_len):
    b, s, d = xs.shape
    tm = TOKEN_TILE
    tiles = s // tm
    tok = lambda width: pl.BlockSpec((1, tm, width), lambda i, t: (i, t, 0))
    return pl.pallas_call(
        functools.partial(_state_proj_kernel, ctx_len=ctx_len),
        out_shape=(
            jax.ShapeDtypeStruct((b, s, MLSTM_WIDTH), BF16),
            jax.ShapeDtypeStruct((b, s, MLSTM_WIDTH), BF16),
            jax.ShapeDtypeStruct((b, s, MLSTM_WIDTH), BF16),
            jax.ShapeDtypeStruct((b, s, 2 * LANES), F32),
            jax.ShapeDtypeStruct((S5_GROUPS, b * s // S5_CHUNK, S5_ROW), F32),
        ),
        grid=(b, s // tm),
        in_specs=[
            tok(d),
            pl.BlockSpec((1, 1, 6, d), lambda i, t: (i, jnp.minimum(t, 1), 0, 0)),
            _resident((1, d)),
            _resident((d, STATE_COLS)),
            _resident((1, 2 * LANES)),
            _resident((2 * CONV_K, MLSTM_WIDTH)),
        ],
        out_specs=(tok(MLSTM_WIDTH), tok(MLSTM_WIDTH), tok(MLSTM_WIDTH), tok(2 * LANES),
                   pl.BlockSpec((S5_GROUPS, tm // S5_CHUNK, S5_ROW), lambda i, t: (0, i * tiles + t, 0))),
        scratch_shapes=[pltpu.VMEM((S5_WIDTH // LANES, tm, LANES), F32)],
        compiler_params=_params("arbitrary", "arbitrary"),
        name="state_proj",
    )(xs, mods, gain, w_state, b_gates, w_conv)


def _mlstm_kernel(q_ref, k_ref, v_ref, g_ref, h_ref, c_ref, n_ref, m_ref):
    direction = pl.program_id(1)
    step = pl.program_id(2)
    lc = MLSTM_CHUNK

    @pl.when(step == 0)
    def _():
        c_ref[...] = jnp.zeros_like(c_ref)
        n_ref[...] = jnp.zeros_like(n_ref)
        m_ref[...] = jnp.zeros_like(m_ref)

    row = lax.broadcasted_iota(jnp.int32, (lc, lc), 0)
    col = lax.broadcasted_iota(jnp.int32, (lc, lc), 1)
    tri = (row - col) * (1 - 2 * direction) >= 0
    tri_b = jnp.where(tri, 1.0, 0.0).astype(BF16)

    g = g_ref[0]
    lf = jax.nn.log_sigmoid(g)
    hi, mid, lo = _split3(lf)
    bcum = (jnp.dot(tri_b, hi, preferred_element_type=F32)
            + jnp.dot(tri_b, mid, preferred_element_type=F32)
            + jnp.dot(tri_b, lo, preferred_element_type=F32))
    total = jnp.sum(lf, axis=0, keepdims=True)
    g_t = g.T
    bcum_t = bcum.T

    q = q_ref[0]
    k = k_ref[0]
    v = v_ref[0]
    for hd in range(HEADS):
        lanes = slice(hd * HEAD_DIM, (hd + 1) * HEAD_DIM)
        qh, kh, vh = q[:, lanes], k[:, lanes], v[:, lanes]
        i_col = g[:, hd:hd + 1]
        b_col = bcum[:, HEADS + hd:HEADS + hd + 1]
        i_row = g_t[hd:hd + 1, :]
        b_row = bcum_t[HEADS + hd:HEADS + hd + 1, :]
        b_last = total[:, HEADS + hd:HEADS + hd + 1]
        m = m_ref[hd][:, 0:1]
        c_mat = c_ref[hd]
        n_vec = n_ref[hd]

        log_s = b_last - b_col + i_col
        m_new = jnp.maximum(b_last + m, jnp.max(log_s, axis=0, keepdims=True))
        w_s = jnp.exp(log_s - m_new)
        decay = jnp.exp(b_last + m - m_new)

        log_w = jnp.where(tri, b_col - b_row + i_row, -jnp.inf)
        log_inter = b_col + m
        m_t = jnp.maximum(log_inter, jnp.max(log_w, axis=1, keepdims=True))
        inter = jnp.exp(log_inter - m_t)
        qk = lax.dot_general(qh, kh, (((1,), (1,)), ((), ())), preferred_element_type=F32)
        s_mat = qk * jnp.exp(log_w - m_t)
        num = inter * jnp.dot(qh, c_mat.astype(BF16), preferred_element_type=F32)
        num = num + jnp.dot(s_mat.astype(BF16), vh, preferred_element_type=F32)
        qn = jnp.sum(qh.astype(F32) * n_vec, axis=1, keepdims=True)
        den = inter * qn + jnp.sum(s_mat, axis=1, keepdims=True)
        h_ref[0, 0, :, lanes] = num / jnp.maximum(jnp.abs(den), jnp.exp(-m_t))

        wk = w_s * kh.astype(F32)
        c_ref[hd] = decay * c_mat + jnp.dot(wk.T.astype(BF16), vh, preferred_element_type=F32)
        n_ref[hd] = decay * n_vec + jnp.sum(wk, axis=0, keepdims=True)
        m_ref[hd] = jnp.broadcast_to(m_new, (1, LANES))


def _mlstm(q, k, v, gates, ctx_len):
    b, s, w = q.shape
    lc = MLSTM_CHUNK
    n_chunks = s // lc
    n_ctx = ctx_len // lc

    def chunk(d, j):
        back = jnp.where(j < n_ctx, n_ctx - 1 - j, n_chunks + n_ctx - 1 - j)
        return jnp.where(d == 0, j, back)

    tok = pl.BlockSpec((1, lc, w), lambda i, d, j: (i, chunk(d, j), 0))
    return pl.pallas_call(
        _mlstm_kernel,
        out_shape=jax.ShapeDtypeStruct((2, b, s, w), F32),
        grid=(b, 2, n_chunks),
        in_specs=[tok, tok, tok,
                  pl.BlockSpec((1, lc, LANES), lambda i, d, j: (i, chunk(d, j), d))],
        out_specs=pl.BlockSpec((1, 1, lc, w), lambda i, d, j: (d, i, chunk(d, j), 0)),
        scratch_shapes=[
            pltpu.VMEM((HEADS, HEAD_DIM, HEAD_DIM), F32),
            pltpu.VMEM((HEADS, 1, HEAD_DIM), F32),
            pltpu.VMEM((HEADS, 1, LANES), F32),
        ],
        compiler_params=_params("arbitrary", "arbitrary", "arbitrary"),
        name="mlstm",
    )(q, k, v, gates)


def _s5_operators(a_re, a_im, log_dt, b_re, b_im, c_re, c_im, d_skip):
    hp = lax.Precision.HIGHEST
    lch = S5_CHUNK
    a_re = a_re.astype(F32)
    a_im = a_im.astype(F32)
    dt = jnp.exp(log_dt.astype(F32))[..., None]
    la_re, la_im = dt * a_re, dt * a_im
    mag = jnp.exp(la_re)
    ab_re, ab_im = mag * jnp.cos(la_im), mag * jnp.sin(la_im)
    nr, ni = ab_re - 1.0, ab_im
    den = a_re * a_re + a_im * a_im
    f_re = ((nr * a_re + ni * a_im) / den)[..., None]
    f_im = ((ni * a_re - nr * a_im) / den)[..., None]
    b_re = b_re.astype(F32)
    b_im = b_im.astype(F32)
    bb_re = f_re * b_re - f_im * b_im
    bb_im = f_re * b_im + f_im * b_re
    c_re = c_re.astype(F32)
    c_im = c_im.astype(F32)

    def a_pow(n):
        n = n.astype(F32)[:, None, None, None]
        mg = jnp.exp(n * la_re)
        return mg * jnp.cos(n * la_im), mg * jnp.sin(n * la_im)

    steps = jnp.arange(lch + 1)
    p_re, p_im = a_pow(steps)
    ab_b_re = p_re[..., None] * bb_re - p_im[..., None] * bb_im
    ab_b_im = p_re[..., None] * bb_im + p_im[..., None] * bb_re
    resp = (jnp.einsum('gjp,ndgpk->ndgjk', c_re, ab_b_re, precision=hp)
            - jnp.einsum('gjp,ndgpk->ndgjk', c_im, ab_b_im, precision=hp))
    src = jnp.arange(lch)[:, None]
    tgt = jnp.arange(lch)[None, :]
    lag_f = jnp.clip(tgt - src, 0, lch)
    lag_b = jnp.clip(src - tgt, 0, lch)
    k_f = jnp.where((tgt >= src)[..., None, None, None], resp[lag_f, 0], 0.0)
    k_b = jnp.where((src >= tgt)[..., None, None, None], resp[lag_b, 1], 0.0)
    skip = (jnp.eye(lch, dtype=F32)[:, :, None, None, None]
            * (d_skip.astype(F32).reshape(S5_GROUPS, S5_GROUP_DIM)[:, :, None]
               * jnp.eye(S5_GROUP_DIM, dtype=F32)[None]))
    toeplitz = jnp.transpose(k_f + k_b + skip, (2, 0, 4, 1, 3))
    toeplitz = toeplitz.reshape(S5_GROUPS, S5_ROW, S5_ROW)

    def to_state(re, im):
        re = jnp.transpose(re, (1, 0, 3, 2)).reshape(S5_GROUPS, S5_ROW, S5_STATE)
        im = jnp.transpose(im, (1, 0, 3, 2)).reshape(S5_GROUPS, S5_ROW, S5_STATE)
        return jnp.concatenate([re, im], axis=-1)
    w_f = to_state(ab_b_re[lch - 1::-1, 0][:lch], ab_b_im[lch - 1::-1, 0][:lch])
    w_b = to_state(ab_b_re[:lch, 1], ab_b_im[:lch, 1])
    w_in = jnp.concatenate([w_f, w_b], axis=-1)

    def from_state(pr, pi):
        ca_re = c_re[None] * pr[:, :, None, :] - c_im[None] * pi[:, :, None, :]
        ca_im = c_re[None] * pi[:, :, None, :] + c_im[None] * pr[:, :, None, :]
        re = jnp.transpose(ca_re, (1, 3, 0, 2)).reshape(S5_GROUPS, S5_STATE, S5_ROW)
        im = jnp.transpose(-ca_im, (1, 3, 0, 2)).reshape(S5_GROUPS, S5_STATE, S5_ROW)
        return jnp.concatenate([re, im], axis=1)
    v_f = from_state(p_re[1:, 0], p_im[1:, 0])
    v_b = from_state(p_re[lch:0:-1, 1], p_im[lch:0:-1, 1])
    v_out = jnp.stack([v_f, v_b])

    n_pairs = S5_GROUPS // 2
    eye2 = jnp.eye(2, dtype=F32)
    t4 = toeplitz.reshape(n_pairs, 2, S5_ROW, S5_ROW)
    t_pair = (t4[:, :, :, None, :] * eye2[None, :, None, :, None]).reshape(n_pairs, 2 * S5_ROW, 2 * S5_ROW)
    w6 = w_in.reshape(n_pairs, 2, S5_ROW, 2, 2, S5_STATE)
    w_pair = (w6[:, :, :, :, :, None, :] * eye2[None, :, None, None, None, :, None])
    w_pair = w_pair.reshape(n_pairs, 2 * S5_ROW, 4 * S5_PAIR)
    v6 = v_out.reshape(2, n_pairs, 2, 2, S5_STATE, S5_ROW)
    v6 = jnp.transpose(v6, (0, 1, 3, 2, 4, 5))
    v_pair = (v6[:, :, :, :, :, None, :] * eye2[None, None, None, :, None, :, None])
    v_pair = v_pair.reshape(2, n_pairs, 2 * S5_PAIR, 2 * S5_ROW)

    al_re, al_im = p_re[lch], p_im[lch]
    def rows(a):
        a = a.reshape(2, 1, S5_GROUPS * S5_STATE)
        return jnp.broadcast_to(a, (2, S5_BATCH, S5_GROUPS * S5_STATE)).reshape(2 * S5_BATCH, -1)
    a_step = jnp.stack([rows(al_re), rows(al_im)])
    return t_pair.astype(BF16), w_pair.astype(BF16), v_pair.astype(BF16), a_step


def _s5_chunk_state_kernel(u_ref, w_ref, x_ref, *, n_chunks):
    u = jnp.concatenate([u_ref[0], u_ref[1]], axis=-1).astype(BF16)
    x = jnp.dot(u, w_ref[0], preferred_element_type=F32)
    for b in range(S5_BATCH):
        xb = x[b * n_chunks:(b + 1) * n_chunks]
        for direction in range(2):
            for part in range(2):
                lo = (2 * direction + part) * S5_PAIR
                x_ref[part, pl.ds(direction * S5_BATCH + b, n_chunks, stride=2 * S5_BATCH), :] = (
                    xb[:, lo:lo + S5_PAIR])


def _s5_chunk_states(u_rows, w_pair, n_chunks):
    g, rows, _ = u_rows.shape
    n_pairs = g // 2
    return pl.pallas_call(
        functools.partial(_s5_chunk_state_kernel, n_chunks=n_chunks),
        out_shape=jax.ShapeDtypeStruct((2 * n_pairs, n_chunks * 2 * S5_BATCH, S5_PAIR), F32),
        grid=(n_pairs,),
        in_specs=[pl.BlockSpec((2, rows, S5_ROW), lambda i: (i, 0, 0)),
                  pl.BlockSpec((1, 2 * S5_ROW, 4 * S5_PAIR), lambda i: (i, 0, 0))],
        out_specs=pl.BlockSpec((2, n_chunks * 2 * S5_BATCH, S5_PAIR), lambda i: (i, 0, 0)),
        compiler_params=_params("arbitrary"),
        name="s5_chunk_states",
    )(u_rows, w_pair)


def _s5_scan_kernel(x_ref, a_ref, of_ref, ob_ref, *, n_chunks, n_ctx):
    tile = 2 * S5_BATCH
    n_slabs = x_ref.shape[0]
    a_re = a_ref[0]
    a_im = a_ref[1]
    fwd_rows = lax.broadcasted_iota(jnp.int32, (tile, S5_PAIR), 0) < S5_BATCH

    def body(i, state):
        cb = jnp.where(i < n_ctx, n_ctx - 1 - i, n_chunks + n_ctx - 1 - i)
        rf = pl.multiple_of(i * tile, tile)
        rb = pl.multiple_of(cb * tile, tile)
        new_state = []
        for p in range(n_slabs // 2):
            re, im = state[2 * p], state[2 * p + 1]
            for j, val in ((2 * p, re), (2 * p + 1, im)):
                of_ref[j, pl.ds(rf, tile), :] = val
                ob_ref[j, pl.ds(rb, tile), :] = val
            in_re = jnp.where(fwd_rows, x_ref[2 * p, pl.ds(rf, tile), :], x_ref[2 * p, pl.ds(rb, tile), :])
            in_im = jnp.where(fwd_rows, x_ref[2 * p + 1, pl.ds(rf, tile), :],
                              x_ref[2 * p + 1, pl.ds(rb, tile), :])
            ar, ai = a_re[:, p * S5_PAIR:(p + 1) * S5_PAIR], a_im[:, p * S5_PAIR:(p + 1) * S5_PAIR]
            new_state.append(ar * re - ai * im + in_re)
            new_state.append(ar * im + ai * re + in_im)
        return tuple(new_state)

    zero = jnp.zeros((tile, S5_PAIR), F32)
    lax.fori_loop(0, n_chunks, body, (zero,) * n_slabs, unroll=4)


def _s5_scan(xc, a_step, n_chunks, n_ctx):
    slabs, rows, lanes = xc.shape
    per_step = 2 * S5_SCAN_PAIRS
    blk = pl.BlockSpec((per_step, rows, lanes), lambda i: (i, 0, 0))
    out = jax.ShapeDtypeStruct(xc.shape, F32)
    return pl.pallas_call(
        functools.partial(_s5_scan_kernel, n_chunks=n_chunks, n_ctx=n_ctx),
        out_shape=(out, out),
        grid=(slabs // per_step,),
        in_specs=[blk, pl.BlockSpec((2, 2 * S5_BATCH, S5_SCAN_PAIRS * S5_PAIR), lambda i: (0, 0, i))],
        out_specs=(blk, blk),
        compiler_params=_params("arbitrary"),
        name="s5_scan",
    )(xc, a_step)


def _s5_output_kernel(u_ref, t_ref, xf_ref, xb_ref, v_ref, y_ref, *, n_chunks):
    u = jnp.concatenate([u_ref[0], u_ref[1]], axis=-1).astype(BF16)
    stride = 2 * S5_BATCH

    def batch_rows(ref, first_row):
        return jnp.concatenate(
            [jnp.concatenate([ref[part, pl.ds(first_row + b, n_chunks, stride=stride), :]
                              for part in range(2)], axis=-1) for b in range(S5_BATCH)], axis=0)

    xf = batch_rows(xf_ref, 0)
    xb = batch_rows(xb_ref, S5_BATCH)
    y = jnp.dot(u, t_ref[0], preferred_element_type=F32)
    y = y + jnp.dot(xf.astype(BF16), v_ref[0, 0], preferred_element_type=F32)
    y = y + jnp.dot(xb.astype(BF16), v_ref[1, 0], preferred_element_type=F32)
    y_ref[0] = y[:, :S5_ROW]
    y_ref[1] = y[:, S5_ROW:]


def _s5_output(u_rows, t_pair, xin_f, xin_b, v_pair, n_chunks):
    g, rows, _ = u_rows.shape
    n_pairs = g // 2
    col = pl.BlockSpec((2,) + xin_f.shape[1:], lambda i: (i, 0, 0))
    return pl.pallas_call(
        functools.partial(_s5_output_kernel, n_chunks=n_chunks),
        out_shape=jax.ShapeDtypeStruct((g, rows, S5_ROW), F32),
        grid=(n_pairs,),
        in_specs=[pl.BlockSpec((2, rows, S5_ROW), lambda i: (i, 0, 0)),
                  pl.BlockSpec((1, 2 * S5_ROW, 2 * S5_ROW), lambda i: (i, 0, 0)),
                  col, col,
                  pl.BlockSpec((2, 1, 2 * S5_PAIR, 2 * S5_ROW), lambda i: (0, i, 0, 0))],
        out_specs=pl.BlockSpec((2, rows, S5_ROW), lambda i: (i, 0, 0)),
        compiler_params=_params("arbitrary"),
        name="s5_output",
    )(u_rows, t_pair, xin_f, xin_b, v_pair)


def _s5_mixer(u_rows, ops, n_chunks, n_ctx):
    t_pair, w_pair, v_pair, a_step = ops
    xc = _s5_chunk_states(u_rows, w_pair, n_chunks)
    xin_f, xin_b = _s5_scan(xc, a_step, n_chunks, n_ctx)
    return _s5_output(u_rows, t_pair, xin_f, xin_b, v_pair, n_chunks)


def _merge_kernel(x_ref, mod_ref, gain_ref, hf_ref, hb_ref, ys_ref, wo_ref, gmh_ref, gsgu_ref,
                  wsgu_ref, bsgu_ref, wglu_ref, bglu_ref, wsc_ref, wum_ref, wug_ref, wus_ref,
                  wuc_ref, wout_ref, o_ref, ybuf_ref, *, ctx_len, tile0):
    t = pl.program_id(1) + tile0
    x = x_ref[0]
    mod = mod_ref[0, 0]
    h = _modulated_norm(x, gain_ref[...], mod[0:1], mod[1:2]).astype(BF16)
    tm = x.shape[0]

    def proj(idx, width=MLSTM_WIDTH):
        start = idx * MLSTM_WIDTH
        return jnp.dot(h, wo_ref[:, start:start + width], preferred_element_type=F32)

    def gate(j):
        start = 6 * MLSTM_WIDTH + j * D_MODEL
        return jax.nn.sigmoid(jnp.dot(h, wo_ref[:, start:start + D_MODEL], preferred_element_type=F32))

    hm = hf_ref[0, 0] + hb_ref[0, 0]
    parts = []
    for hd in range(HEADS):
        hh = hm[:, hd * HEAD_DIM:(hd + 1) * HEAD_DIM]
        parts.append(hh * lax.rsqrt(jnp.mean(hh * hh, axis=-1, keepdims=True) + RMS_EPS))
    y_a = (jnp.concatenate(parts, axis=-1) * gmh_ref[...]) * jax.nn.sigmoid(proj(0))
    acc = gate(0) * jnp.dot(y_a.astype(BF16), wum_ref[...], preferred_element_type=F32)

    su = jax.nn.gelu(proj(1))
    sv = jax.nn.gelu(proj(2))
    mu = jnp.mean(sv, axis=-1, keepdims=True)
    cen = sv - mu
    var = jnp.mean(cen * cen, axis=-1, keepdims=True)
    vn = (cen * lax.rsqrt(var + RMS_EPS) * gsgu_ref[...]).astype(BF16)
    bias = bsgu_ref[...]
    rows = []
    for n in range(tm // SGU_CHUNK):
        cols = []
        for gi in range(SGU_GROUPS):
            blk = vn[n * SGU_CHUNK:(n + 1) * SGU_CHUNK, gi * SGU_GROUP_DIM:(gi + 1) * SGU_GROUP_DIM]
            cols.append(jnp.dot(wsgu_ref[gi], blk, preferred_element_type=F32) + bias[:, gi:gi + 1])
        rows.append(jnp.concatenate(cols, axis=-1))
    y_b = su * jnp.concatenate(rows, axis=0)
    acc = acc + gate(1) * jnp.dot(y_b.astype(BF16), wug_ref[...], preferred_element_type=F32)

    ys = jax.nn.gelu(_from_group_rows(ys_ref, ybuf_ref))
    y_c = ys * jax.nn.sigmoid(jnp.dot(ys.astype(BF16), wglu_ref[...], preferred_element_type=F32)
                              + bglu_ref[...])
    acc = acc + gate(2) * jnp.dot(y_c.astype(BF16), wus_ref[...], preferred_element_type=F32)

    first, last = _row_edges(t, tm, ctx_len)
    y_d = proj(3) * _conv3(proj(4) * proj(5), wsc_ref[...], first, last)
    acc = acc + gate(3) * jnp.dot(y_d.astype(BF16), wuc_ref[...], preferred_element_type=F32)

    o_ref[0] = x + mod[2:3] * jnp.dot(acc.astype(BF16), wout_ref[...], preferred_element_type=F32)


def _merge(xs, mods, gain, h_dirs, y_s5, lw, ctx_len, tile0):
    b, s, d = xs.shape
    tm = TOKEN_TILE
    tiles = s // tm
    tok = lambda width: pl.BlockSpec((1, tm, width), lambda i, t: (i, t + tile0, 0))
    hdir = lambda direction: pl.BlockSpec((1, 1, tm, MLSTM_WIDTH),
                                          lambda i, t: (direction, i, t + tile0, 0))
    group_rows = pl.BlockSpec((S5_GROUPS, tm // S5_CHUNK, S5_ROW),
                              lambda i, t: (0, i * tiles + t + tile0, 0))
    weights = (lw['w_out_side'], lw['g_mh'], lw['g_sgu'], lw['w_sgu'], lw['b_sgu_t'], lw['w_glu'],
               lw['b_glu'], lw['w_sconv'], lw['w_up_mlstm'], lw['w_up_sgu'], lw['w_up_s5'],
               lw['w_up_sconv'], lw['w_out'])
    return pl.pallas_call(
        functools.partial(_merge_kernel, ctx_len=ctx_len, tile0=tile0),
        out_shape=jax.ShapeDtypeStruct((b, s, d), F32),
        grid=(b, s // tm - tile0),
        in_specs=[tok(d),
                  pl.BlockSpec((1, 1, 6, d), lambda i, t: (i, jnp.minimum(t + tile0, 1), 0, 0)),
                  _resident((1, d)), hdir(0), hdir(1), group_rows]
                 + [_resident(w.shape) for w in weights],
        out_specs=tok(d),
        scratch_shapes=[pltpu.VMEM((S5_WIDTH // LANES, tm, LANES), F32)],
        compiler_params=_params("arbitrary", "arbitrary"),
        name="merge",
    )(xs, mods, gain, h_dirs, h_dirs, y_s5, *weights)


def _ffn_kernel(x_ref, mod_ref, gain_ref, wg_ref, wu_ref, wd_ref, gf_ref, o_ref, *, final):
    x = x_ref[0]
    mod = mod_ref[0, 0]
    h = _modulated_norm(x, gain_ref[...], mod[3:4], mod[4:5]).astype(BF16)
    a = jnp.dot(h, wg_ref[...], preferred_element_type=F32)
    b = jnp.dot(h, wu_ref[...], preferred_element_type=F32)
    hid = ((a * jax.nn.sigmoid(a)) * b).astype(BF16)
    y = x + mod[5:6] * jnp.dot(hid, wd_ref[...], preferred_element_type=F32)
    if final:
        y = y * lax.rsqrt(jnp.mean(y * y, axis=-1, keepdims=True) + RMS_EPS) * gf_ref[...]
    o_ref[0] = y


def _ffn(xs, mods, gain, w_gate, w_up, w_down, g_final, tile0, final):
    b, s, d = xs.shape
    tm = TOKEN_TILE
    n_tiles = s // tm - tile0
    out_tile0 = 0 if final else tile0
    return pl.pallas_call(
        functools.partial(_ffn_kernel, final=final),
        out_shape=jax.ShapeDtypeStruct((b, (n_tiles + out_tile0) * tm, d), F32),
        grid=(b, n_tiles),
        in_specs=[pl.BlockSpec((1, tm, d), lambda i, t: (i, t + tile0, 0)),
                  pl.BlockSpec((1, 1, 6, d), lambda i, t: (i, jnp.minimum(t + tile0, 1), 0, 0)),
                  _resident((1, d)), _resident(w_gate.shape), _resident(w_up.shape),
                  _resident(w_down.shape), _resident((1, d))],
        out_specs=pl.BlockSpec((1, tm, d), lambda i, t: (i, t + out_tile0, 0)),
        compiler_params=_params("arbitrary", "arbitrary"),
        name="ffn_final" if final else "ffn",
    )(xs, mods, gain, w_gate, w_up, w_down, g_final)


def _layer_weights(l, w_in, b_gates, w_conv_qk, g_mh, g_sgu, w_sgu, b_sgu, w_glu, b_glu, w_sconv,
                   w_up_mlstm, w_up_sgu, w_up_s5, w_up_sconv, w_out, w_ffn_gate, w_ffn_up, w_ffn_down):
    w = w_in[l]
    gate_pad = ((0, 0), (0, LANES - 2 * HEADS))
    w_state = jnp.concatenate([
        w[:, :GATE_COL],
        jnp.pad(w[:, GATE_COL:GATE_COL + 2 * HEADS], gate_pad),
        jnp.pad(w[:, GATE_COL + 2 * HEADS:GATE_COL + 4 * HEADS], gate_pad),
        w[:, GATE_COL + 4 * HEADS:STATE_DIM]], axis=1).astype(BF16)
    bg = b_gates[l]
    b_pad = (0, LANES - 2 * HEADS)
    bg = jnp.concatenate([jnp.pad(bg[:2 * HEADS], b_pad), jnp.pad(bg[2 * HEADS:], b_pad)])
    return dict(
        w_state=w_state,
        b_gates=bg.reshape(1, 2 * LANES),
        w_conv=w_conv_qk[l].reshape(2 * CONV_K, MLSTM_WIDTH),
        w_out_side=w[:, STATE_DIM:].astype(BF16),
        g_mh=g_mh[l].reshape(1, -1),
        g_sgu=g_sgu[l].reshape(1, -1),
        w_sgu=w_sgu[l].astype(BF16),
        b_sgu_t=b_sgu[l].T,
        w_glu=w_glu[l].astype(BF16),
        b_glu=b_glu[l].reshape(1, -1),
        w_sconv=w_sconv[l],
        w_up_mlstm=w_up_mlstm[l].astype(BF16),
        w_up_sgu=w_up_sgu[l].astype(BF16),
        w_up_s5=w_up_s5[l].astype(BF16),
        w_up_sconv=w_up_sconv[l].astype(BF16),
        w_out=w_out[l].astype(BF16),
        w_ffn_gate=w_ffn_gate[l].astype(BF16),
        w_ffn_up=w_ffn_up[l].astype(BF16),
        w_ffn_down=w_ffn_down[l].astype(BF16),
    )


def kernel(x, c, ctx, c_ctx, w_mod, b_mod, g_norm_mix, g_norm_ffn, w_in, b_gates, w_conv_qk, g_mh, g_sgu, w_sgu, b_sgu, s5_a_re, s5_a_im, s5_log_dt, s5_b_re, s5_b_im, s5_c_re, s5_c_im, s5_d, w_glu, b_glu, w_sconv, w_up_mlstm, w_up_sgu, w_up_s5, w_up_sconv, w_out, w_ffn_gate, w_ffn_up, w_ffn_down, g_final):
    batch, seq, d = x.shape
    ctx_len = ctx.shape[1]
    depth = w_mod.shape[0]
    assert d == D_MODEL and ctx_len == TOKEN_TILE and seq % TOKEN_TILE == 0 and batch == S5_BATCH
    ctx_tiles = ctx_len // TOKEN_TILE

    c_rows = jnp.zeros((8, d), F32).at[:batch].set(c).at[batch].set(c_ctx)
    mod_all = _modulation(c_rows, w_mod, b_mod).reshape(depth, 8, 6, d)
    xs = jnp.concatenate([ctx, x], axis=1)

    out = None
    for l in range(depth):
        last = l == depth - 1
        mod_x = mod_all[l, :batch]
        mod_c = jnp.broadcast_to(mod_all[l, batch][None], (batch, 6, d))
        mods = jnp.stack([mod_c, mod_x], axis=1)
        lw = _layer_weights(l, w_in, b_gates, w_conv_qk, g_mh, g_sgu, w_sgu, b_sgu, w_glu, b_glu,
                            w_sconv, w_up_mlstm, w_up_sgu, w_up_s5, w_up_sconv, w_out,
                            w_ffn_gate, w_ffn_up, w_ffn_down)
        s5_ops = _s5_operators(s5_a_re[l], s5_a_im[l], s5_log_dt[l], s5_b_re[l], s5_b_im[l],
                               s5_c_re[l], s5_c_im[l], s5_d[l])
        gain_mix = g_norm_mix[l].reshape(1, d)
        gain_ffn = g_norm_ffn[l].reshape(1, d)

        q, k, v, gates, u_rows = _state_proj(xs, mods, gain_mix, lw['w_state'], lw['b_gates'],
                                             lw['w_conv'], ctx_len)
        h_dirs = _mlstm(q, k, v, gates, ctx_len)
        y_s5 = _s5_mixer(u_rows, s5_ops, (ctx_len + seq) // S5_CHUNK, ctx_len // S5_CHUNK)
        tile0 = ctx_tiles if last else 0
        x_mid = _merge(xs, mods, gain_mix, h_dirs, y_s5, lw, ctx_len, tile0)
        xs = _ffn(x_mid, mods, gain_ffn, lw['w_ffn_gate'], lw['w_ffn_up'], lw['w_ffn_down'],
                  g_final.reshape(1, d), tile0, last)
        out = xs
    return out
```

```python
import functools
import math

import jax
import jax.numpy as jnp
from jax import lax
from jax.experimental import pallas as pl
from jax.experimental.pallas import tpu as pltpu

F32 = jnp.float32
BF16 = jnp.bfloat16

D_MODEL = 1024
GRID_W = 64
N_BRANCH = 4
RMS_EPS = 1e-6
CONV_K = 3
HEADS = 4
HEAD_DIM = 128
MLSTM_WIDTH = HEADS * HEAD_DIM
MLSTM_CHUNK = 128
SGU_GROUPS = 4
SGU_GROUP_DIM = 128
SGU_WIDTH = SGU_GROUPS * SGU_GROUP_DIM
SGU_CHUNK = 128
S5_GROUP_DIM = 16
S5_GROUPS = 24
S5_WIDTH = S5_GROUPS * S5_GROUP_DIM
S5_STATE = 64
SCONV_WIDTH = 512
STATE_DIM = 3 * MLSTM_WIDTH + 4 * HEADS + S5_WIDTH
GATE_COL = 3 * MLSTM_WIDTH

LANES = 128
TOKEN_TILE = 256
S5_CHUNK = 16
S5_ROW = S5_CHUNK * S5_GROUP_DIM
S5_PAIR = 2 * S5_STATE
S5_SCAN_PAIRS = 2
S5_BATCH = 4
STATE_COLS = 3 * MLSTM_WIDTH + 2 * LANES + S5_WIDTH
U_COL = 3 * MLSTM_WIDTH + 2 * LANES
VMEM_LIMIT = 52 * 1024 * 1024


def _resident(shape):
    zeros = (0,) * len(shape)
    return pl.BlockSpec(shape, lambda *_: zeros, pipeline_mode=pl.Buffered(1))


def _params(*semantics):
    return pltpu.CompilerParams(dimension_semantics=semantics, vmem_limit_bytes=VMEM_LIMIT)


def _modulated_norm(x, gain, shift, scale):
    y = x * lax.rsqrt(jnp.mean(x * x, axis=-1, keepdims=True) + RMS_EPS) * gain
    return y * (1.0 + scale) + shift


def _split3(a):
    hi = a.astype(BF16)
    r1 = a - hi.astype(F32)
    mid = r1.astype(BF16)
    lo = (r1 - mid.astype(F32)).astype(BF16)
    return hi, mid, lo


def _conv3(a, w, first, last):
    n = a.shape[0]
    prev = jnp.where(first, 0.0, pltpu.roll(a, 1, 0))
    nxt = jnp.where(last, 0.0, pltpu.roll(a, n - 1, 0))
    return prev * w[0:1] + a * w[1:2] + nxt * w[2:3]


def _row_edges(tile_idx, n_rows, ctx_len):
    period = jnp.where(tile_idx == 0, ctx_len, GRID_W)
    pos = lax.broadcasted_iota(jnp.int32, (n_rows, 1), 0) & (period - 1)
    return pos == 0, pos == period - 1


def _mod_kernel(c_ref, w_ref, b_ref, o_ref):
    a = c_ref[...]
    a = a * jax.nn.sigmoid(a)
    a_hi = a.astype(BF16)
    a_lo = (a - a_hi.astype(F32)).astype(BF16)
    w = w_ref[0]
    w_hi = w.astype(BF16)
    w_lo = (w - w_hi.astype(F32)).astype(BF16)
    acc = jnp.dot(a_hi, w_hi, preferred_element_type=F32)
    acc = acc + jnp.dot(a_lo, w_hi, preferred_element_type=F32)
    acc = acc + jnp.dot(a_hi, w_lo, preferred_element_type=F32)
    o_ref[0] = acc + b_ref[0]


def _modulation(c_rows, w_mod, b_mod):
    depth, d, n = w_mod.shape
    tn = 1536
    return pl.pallas_call(
        _mod_kernel,
        out_shape=jax.ShapeDtypeStruct((depth, 8, n), F32),
        grid=(depth, n // tn),
        in_specs=[
            pl.BlockSpec((8, d), lambda l, j: (0, 0)),
            pl.BlockSpec((1, d, tn), lambda l, j: (l, 0, j)),
            pl.BlockSpec((1, 1, tn), lambda l, j: (l, 0, j)),
        ],
        out_specs=pl.BlockSpec((1, 8, tn), lambda l, j: (l, 0, j)),
        compiler_params=_params("arbitrary", "arbitrary"),
        name="modulation",
    )(c_rows, w_mod, b_mod.reshape(depth, 1, n))


def _to_group_rows(z, u_ref, buf_ref):
    n_chunks = z.shape[0] // S5_CHUNK
    per_block = LANES // S5_GROUP_DIM
    for gb in range(S5_WIDTH // LANES):
        buf_ref[gb] = z[:, gb * LANES:(gb + 1) * LANES]
    for s in range(S5_CHUNK):
        for gb in range(S5_WIDTH // LANES):
            zs = buf_ref[gb, pl.ds(s, n_chunks, stride=S5_CHUNK), :]
            for gl in range(per_block):
                u_ref[gb * per_block + gl, :, s * S5_GROUP_DIM:(s + 1) * S5_GROUP_DIM] = (
                    zs[:, gl * S5_GROUP_DIM:(gl + 1) * S5_GROUP_DIM])


def _from_group_rows(y_ref, buf_ref):
    n_chunks = y_ref.shape[1]
    per_block = LANES // S5_GROUP_DIM
    for s in range(S5_CHUNK):
        for gb in range(S5_WIDTH // LANES):
            pieces = [y_ref[gb * per_block + gl, :, s * S5_GROUP_DIM:(s + 1) * S5_GROUP_DIM]
                      for gl in range(per_block)]
            buf_ref[gb, pl.ds(s, n_chunks, stride=S5_CHUNK), :] = jnp.concatenate(pieces, axis=-1)
    return jnp.concatenate([buf_ref[gb] for gb in range(S5_WIDTH // LANES)], axis=-1)


def _state_proj_kernel(x_ref, mod_ref, gain_ref, w_ref, bg_ref, wc_ref,
                       q_ref, k_ref, v_ref, g_ref, u_ref, ubuf_ref, *, ctx_len):
    t = pl.program_id(1)
    mod = mod_ref[0, 0]
    h = _modulated_norm(x_ref[0], gain_ref[...], mod[0:1], mod[1:2])
    z = jnp.dot(h.astype(BF16), w_ref[...], preferred_element_type=F32)
    first, last = _row_edges(t, z.shape[0], ctx_len)
    wc = wc_ref[...]
    w = MLSTM_WIDTH
    q = _conv3(z[:, 0:w], wc[0:3], first, last)
    k = _conv3(z[:, w:2 * w], wc[3:6], first, last)
    q_ref[0] = (q * jax.nn.sigmoid(q)).astype(BF16)
    k_ref[0] = ((k * jax.nn.sigmoid(k)) * (HEAD_DIM ** -0.5)).astype(BF16)
    v_ref[0] = z[:, 2 * w:3 * w].astype(BF16)
    g_ref[0] = z[:, GATE_COL:GATE_COL + 2 * LANES] + bg_ref[...]
    _to_group_rows(z[:, U_COL:U_COL + S5_WIDTH], u_ref, ubuf_ref)


def _state_proj(xs, mods, gain, w_state, b_gates, w_conv, ctx_len):
    b, s, d = xs.shape
    tm = TOKEN_TILE
    tiles = s // tm
    tok = lambda width: pl.BlockSpec((1, tm, width), lambda i, t: (i, t, 0))
    return pl.pallas_call(
        functools.partial(_state_proj_kernel, ctx_len=ctx_len),
        out_shape=(
            jax.ShapeDtypeStruct((b, s, MLSTM_WIDTH), BF16),
            jax.ShapeDtypeStruct((b, s, MLSTM_WIDTH), BF16),
            jax.ShapeDtypeStruct((b, s, MLSTM_WIDTH), BF16),
            jax.ShapeDtypeStruct((b, s, 2 * LANES), F32),
            jax.ShapeDtypeStruct((S5_GROUPS, b * s // S5_CHUNK, S5_ROW), F32),
        ),
        grid=(b, s // tm),
        in_specs=[
            tok(d),
            pl.BlockSpec((1, 1, 6, d), lambda i, t: (i, jnp.minimum(t, 1), 0, 0)),
            _resident((1, d)),
            _resident((d, STATE_COLS)),
            _resident((1, 2 * LANES)),
            _resident((2 * CONV_K, MLSTM_WIDTH)),
        ],
        out_specs=(tok(MLSTM_WIDTH), tok(MLSTM_WIDTH), tok(MLSTM_WIDTH), tok(2 * LANES),
                   pl.BlockSpec((S5_GROUPS, tm // S5_CHUNK, S5_ROW), lambda i, t: (0, i * tiles + t, 0))),
        scratch_shapes=[pltpu.VMEM((S5_WIDTH // LANES, tm, LANES), F32)],
        compiler_params=_params("arbitrary", "arbitrary"),
        name="state_proj",
    )(xs, mods, gain, w_state, b_gates, w_conv)


def _mlstm_kernel(q_ref, k_ref, v_ref, g_ref, h_ref, c_ref, n_ref, m_ref):
    direction = pl.program_id(1)
    step = pl.program_id(2)
    lc = MLSTM_CHUNK

    @pl.when(step == 0)
    def _():
        c_ref[...] = jnp.zeros_like(c_ref)
        n_ref[...] = jnp.zeros_like(n_ref)
        m_ref[...] = jnp.zeros_like(m_ref)

    row = lax.broadcasted_iota(jnp.int32, (lc, lc), 0)
    col = lax.broadcasted_iota(jnp.int32, (lc, lc), 1)
    tri = (row - col) * (1 - 2 * direction) >= 0
    tri_b = jnp.where(tri, 1.0, 0.0).astype(BF16)

    g = g_ref[0]
    lf = jax.nn.log_sigmoid(g)
    hi, mid, lo = _split3(lf)
    bcum = (jnp.dot(tri_b, hi, preferred_element_type=F32)
            + jnp.dot(tri_b, mid, preferred_element_type=F32)
            + jnp.dot(tri_b, lo, preferred_element_type=F32))
    total = jnp.sum(lf, axis=0, keepdims=True)
    g_t = g.T
    bcum_t = bcum.T

    q = q_ref[0]
    k = k_ref[0]
    v = v_ref[0]
    for hd in range(HEADS):
        lanes = slice(hd * HEAD_DIM, (hd + 1) * HEAD_DIM)
        qh, kh, vh = q[:, lanes], k[:, lanes], v[:, lanes]
        i_col = g[:, hd:hd + 1]
        b_col = bcum[:, HEADS + hd:HEADS + hd + 1]
        i_row = g_t[hd:hd + 1, :]
        b_row = bcum_t[HEADS + hd:HEADS + hd + 1, :]
        b_last = total[:, HEADS + hd:HEADS + hd + 1]
        m = m_ref[hd][:, 0:1]
        c_mat = c_ref[hd]
        n_vec = n_ref[hd]

        log_s = b_last - b_col + i_col
        m_new = jnp.maximum(b_last + m, jnp.max(log_s, axis=0, keepdims=True))
        w_s = jnp.exp(log_s - m_new)
        decay = jnp.exp(b_last + m - m_new)

        log_w = jnp.where(tri, b_col - b_row + i_row, -jnp.inf)
        log_inter = b_col + m
        m_t = jnp.maximum(log_inter, jnp.max(log_w, axis=1, keepdims=True))
        inter = jnp.exp(log_inter - m_t)
        qk = lax.dot_general(qh, kh, (((1,), (1,)), ((), ())), preferred_element_type=F32)
        s_mat = qk * jnp.exp(log_w - m_t)
        num = inter * jnp.dot(qh, c_mat.astype(BF16), preferred_element_type=F32)
        num = num + jnp.dot(s_mat.astype(BF16), vh, preferred_element_type=F32)
        qn = jnp.sum(qh.astype(F32) * n_vec, axis=1, keepdims=True)
        den = inter * qn + jnp.sum(s_mat, axis=1, keepdims=True)
        h_ref[0, 0, :, lanes] = num / jnp.maximum(jnp.abs(den), jnp.exp(-m_t))

        wk = w_s * kh.astype(F32)
        c_ref[hd] = decay * c_mat + jnp.dot(wk.T.astype(BF16), vh, preferred_element_type=F32)
        n_ref[hd] = decay * n_vec + jnp.sum(wk, axis=0, keepdims=True)
        m_ref[hd] = jnp.broadcast_to(m_new, (1, LANES))


def _mlstm(q, k, v, gates, ctx_len):
    b, s, w = q.shape
    lc = MLSTM_CHUNK
    n_chunks = s // lc
    n_ctx = ctx_len // lc

    def chunk(d, j):
        back = jnp.where(j < n_ctx, n_ctx - 1 - j, n_chunks + n_ctx - 1 - j)
        return jnp.where(d == 0, j, back)

    tok = pl.BlockSpec((1, lc, w), lambda i, d, j: (i, chunk(d, j), 0))
    return pl.pallas_call(
        _mlstm_kernel,
        out_shape=jax.ShapeDtypeStruct((2, b, s, w), F32),
        grid=(b, 2, n_chunks),
        in_specs=[tok, tok, tok,
                  pl.BlockSpec((1, lc, LANES), lambda i, d, j: (i, chunk(d, j), d))],
        out_specs=pl.BlockSpec((1, 1, lc, w), lambda i, d, j: (d, i, chunk(d, j), 0)),
        scratch_shapes=[
            pltpu.VMEM((HEADS, HEAD_DIM, HEAD_DIM), F32),
            pltpu.VMEM((HEADS, 1, HEAD_DIM), F32),
            pltpu.VMEM((HEADS, 1, LANES), F32),
        ],
        compiler_params=_params("arbitrary", "arbitrary", "arbitrary"),
        name="mlstm",
    )(q, k, v, gates)


def _s5_operators(a_re, a_im, log_dt, b_re, b_im, c_re, c_im, d_skip):
    hp = lax.Precision.HIGHEST
    lch = S5_CHUNK
    depth = a_re.shape[0]
    a_re = a_re.astype(F32)
    a_im = a_im.astype(F32)
    dt = jnp.exp(log_dt.astype(F32))[..., None]
    la_re, la_im = dt * a_re, dt * a_im
    mag = jnp.exp(la_re)
    ab_re, ab_im = mag * jnp.cos(la_im), mag * jnp.sin(la_im)
    nr, ni = ab_re - 1.0, ab_im
    den = a_re * a_re + a_im * a_im
    f_re = ((nr * a_re + ni * a_im) / den)[:, :, :, None, :]
    f_im = ((ni * a_re - nr * a_im) / den)[:, :, :, None, :]
    bt_re = jnp.swapaxes(b_re.astype(F32), -1, -2)[:, None]
    bt_im = jnp.swapaxes(b_im.astype(F32), -1, -2)[:, None]
    bb_re = f_re * bt_re - f_im * bt_im
    bb_im = f_re * bt_im + f_im * bt_re
    c_re = c_re.astype(F32)
    c_im = c_im.astype(F32)

    n = jnp.arange(-(lch - 1), lch + 1, dtype=F32)[:, None]
    mg = jnp.exp(n * la_re[:, :, :, None, :])
    ang = n * la_im[:, :, :, None, :]
    pw_re, pw_im = mg * jnp.cos(ang), mg * jnp.sin(ang)
    zero = lch - 1

    def powers(direction, first, step):
        start = zero + first
        stop = start + step * lch
        sl = slice(start, stop if stop >= 0 else None, step)
        return pw_re[:, direction, :, sl, None, :], pw_im[:, direction, :, sl, None, :]

    def times(x_re, x_im, p):
        return x_re * p[0] - x_im * p[1], x_re * p[1] + x_im * p[0]

    def rows256(x):
        return x.reshape(depth, S5_GROUPS, S5_ROW, S5_STATE)

    bbf = (bb_re[:, 0, :, None], bb_im[:, 0, :, None])
    bbb = (bb_re[:, 1, :, None], bb_im[:, 1, :, None])
    cc = (c_re[:, :, None], c_im[:, :, None])

    def response(bb, direction, sign):
        l_re, l_im = times(*bb, powers(direction, 0, -sign))
        r_re, r_im = times(*cc, powers(direction, 0, sign))
        left = jnp.concatenate([rows256(l_re), rows256(l_im)], axis=-1)
        right = jnp.concatenate([rows256(r_re), -rows256(r_im)], axis=-1)
        return jnp.einsum('lgap,lgbp->lgab', left, right, precision=hp)

    src = (jnp.arange(S5_ROW) // S5_GROUP_DIM)[:, None]
    tgt = (jnp.arange(S5_ROW) // S5_GROUP_DIM)[None, :]
    d_rows = jnp.tile(d_skip.astype(F32).reshape(depth, S5_GROUPS, 1, S5_GROUP_DIM), (1, 1, 1, lch))
    toeplitz = (jnp.where(tgt >= src, response(bbf, 0, 1), 0.0)
                + jnp.where(src >= tgt, response(bbb, 1, -1), 0.0)
                + jnp.eye(S5_ROW, dtype=F32) * d_rows)

    def pair_block(x_re, x_im):
        x_re = x_re.reshape(depth, S5_GROUPS // 2, 2, S5_ROW, S5_STATE)
        x_im = x_im.reshape(depth, S5_GROUPS // 2, 2, S5_ROW, S5_STATE)
        z = jnp.zeros_like(x_re[:, :, 0])
        top = jnp.concatenate([x_re[:, :, 0], z, x_im[:, :, 0], z], axis=-1)
        bottom = jnp.concatenate([z, x_re[:, :, 1], z, x_im[:, :, 1]], axis=-1)
        return jnp.concatenate([top, bottom], axis=-2)

    def state_map(x, p):
        re, im = times(*x, p)
        return rows256(re), rows256(im)

    w_f = pair_block(*state_map(bbf, powers(0, lch - 1, -1)))
    w_b = pair_block(*state_map(bbb, powers(1, 0, 1)))
    w_pair = jnp.concatenate([w_f, w_b], axis=-1)
    vf_re, vf_im = state_map(cc, powers(0, 1, 1))
    vb_re, vb_im = state_map(cc, powers(1, lch, -1))
    vt_pair = jnp.stack([pair_block(vf_re, -vf_im), pair_block(vb_re, -vb_im)], axis=1)

    al_re, al_im = pw_re[:, :, :, zero + lch], pw_im[:, :, :, zero + lch]
    def rows(a):
        a = a.reshape(depth, 2, 1, S5_GROUPS * S5_STATE)
        return jnp.broadcast_to(a, (depth, 2, S5_BATCH, S5_GROUPS * S5_STATE)).reshape(depth, 2 * S5_BATCH, -1)
    a_step = jnp.stack([rows(al_re), rows(al_im)], axis=1)
    return toeplitz.astype(BF16), w_pair.astype(BF16), vt_pair.astype(BF16), a_step


def _s5_chunk_state_kernel(u_ref, w_ref, x_ref, *, n_chunks):
    u = jnp.concatenate([u_ref[0], u_ref[1]], axis=-1).astype(BF16)
    x = jnp.dot(u, w_ref[0], preferred_element_type=F32)
    for b in range(S5_BATCH):
        xb = x[b * n_chunks:(b + 1) * n_chunks]
        for direction in range(2):
            for part in range(2):
                lo = (2 * direction + part) * S5_PAIR
                x_ref[part, pl.ds(direction * S5_BATCH + b, n_chunks, stride=2 * S5_BATCH), :] = (
                    xb[:, lo:lo + S5_PAIR])


def _s5_chunk_states(u_rows, w_pair, n_chunks):
    g, rows, _ = u_rows.shape
    n_pairs = g // 2
    return pl.pallas_call(
        functools.partial(_s5_chunk_state_kernel, n_chunks=n_chunks),
        out_shape=jax.ShapeDtypeStruct((2 * n_pairs, n_chunks * 2 * S5_BATCH, S5_PAIR), F32),
        grid=(n_pairs,),
        in_specs=[pl.BlockSpec((2, rows, S5_ROW), lambda i: (i, 0, 0)),
                  pl.BlockSpec((1, 2 * S5_ROW, 4 * S5_PAIR), lambda i: (i, 0, 0))],
        out_specs=pl.BlockSpec((2, n_chunks * 2 * S5_BATCH, S5_PAIR), lambda i: (i, 0, 0)),
        compiler_params=_params("arbitrary"),
        name="s5_chunk_states",
    )(u_rows, w_pair)


def _s5_scan_kernel(x_ref, a_ref, of_ref, ob_ref, *, n_chunks, n_ctx):
    tile = 2 * S5_BATCH
    n_slabs = x_ref.shape[0]
    a_re = a_ref[0]
    a_im = a_ref[1]
    fwd_rows = lax.broadcasted_iota(jnp.int32, (tile, S5_PAIR), 0) < S5_BATCH

    def body(i, state):
        cb = jnp.where(i < n_ctx, n_ctx - 1 - i, n_chunks + n_ctx - 1 - i)
        rf = pl.multiple_of(i * tile, tile)
        rb = pl.multiple_of(cb * tile, tile)
        new_state = []
        for p in range(n_slabs // 2):
            re, im = state[2 * p], state[2 * p + 1]
            for j, val in ((2 * p, re), (2 * p + 1, im)):
                of_ref[j, pl.ds(rf, tile), :] = val
                ob_ref[j, pl.ds(rb, tile), :] = val
            in_re = jnp.where(fwd_rows, x_ref[2 * p, pl.ds(rf, tile), :], x_ref[2 * p, pl.ds(rb, tile), :])
            in_im = jnp.where(fwd_rows, x_ref[2 * p + 1, pl.ds(rf, tile), :],
                              x_ref[2 * p + 1, pl.ds(rb, tile), :])
            ar, ai = a_re[:, p * S5_PAIR:(p + 1) * S5_PAIR], a_im[:, p * S5_PAIR:(p + 1) * S5_PAIR]
            new_state.append(ar * re - ai * im + in_re)
            new_state.append(ar * im + ai * re + in_im)
        return tuple(new_state)

    zero = jnp.zeros((tile, S5_PAIR), F32)
    lax.fori_loop(0, n_chunks, body, (zero,) * n_slabs, unroll=4)


def _s5_scan(xc, a_step, n_chunks, n_ctx):
    slabs, rows, lanes = xc.shape
    per_step = 2 * S5_SCAN_PAIRS
    blk = pl.BlockSpec((per_step, rows, lanes), lambda i: (i, 0, 0))
    out = jax.ShapeDtypeStruct(xc.shape, F32)
    return pl.pallas_call(
        functools.partial(_s5_scan_kernel, n_chunks=n_chunks, n_ctx=n_ctx),
        out_shape=(out, out),
        grid=(slabs // per_step,),
        in_specs=[blk, pl.BlockSpec((2, 2 * S5_BATCH, S5_SCAN_PAIRS * S5_PAIR), lambda i: (0, 0, i))],
        out_specs=(blk, blk),
        compiler_params=_params("arbitrary"),
        name="s5_scan",
    )(xc, a_step)


def _s5_output_kernel(u_ref, t_ref, xf_ref, xb_ref, v_ref, y_ref, *, n_chunks):
    stride = 2 * S5_BATCH

    def batch_rows(ref, first_row):
        return jnp.concatenate(
            [jnp.concatenate([ref[part, pl.ds(first_row + b, n_chunks, stride=stride), :]
                              for part in range(2)], axis=-1) for b in range(S5_BATCH)], axis=0)

    xf = batch_rows(xf_ref, 0)
    xb = batch_rows(xb_ref, S5_BATCH)
    last = (((1,), (1,)), ((), ()))
    y = lax.dot_general(xf.astype(BF16), v_ref[0, 0], last, preferred_element_type=F32)
    y = y + lax.dot_general(xb.astype(BF16), v_ref[1, 0], last, preferred_element_type=F32)
    for a in range(2):
        within = jnp.dot(u_ref[a].astype(BF16), t_ref[a], preferred_element_type=F32)
        y_ref[a] = within + y[:, a * S5_ROW:(a + 1) * S5_ROW]


def _s5_output(u_rows, t_pair, xin_f, xin_b, v_pair, n_chunks):
    g, rows, _ = u_rows.shape
    n_pairs = g // 2
    col = pl.BlockSpec((2,) + xin_f.shape[1:], lambda i: (i, 0, 0))
    return pl.pallas_call(
        functools.partial(_s5_output_kernel, n_chunks=n_chunks),
        out_shape=jax.ShapeDtypeStruct((g, rows, S5_ROW), F32),
        grid=(n_pairs,),
        in_specs=[pl.BlockSpec((2, rows, S5_ROW), lambda i: (i, 0, 0)),
                  pl.BlockSpec((2, S5_ROW, S5_ROW), lambda i: (i, 0, 0)),
                  col, col,
                  pl.BlockSpec((2, 1, 2 * S5_ROW, 2 * S5_PAIR), lambda i: (0, i, 0, 0))],
        out_specs=pl.BlockSpec((2, rows, S5_ROW), lambda i: (i, 0, 0)),
        compiler_params=_params("arbitrary"),
        name="s5_output",
    )(u_rows, t_pair, xin_f, xin_b, v_pair)


def _s5_mixer(u_rows, ops, n_chunks, n_ctx):
    toeplitz, w_pair, vt_pair, a_step = ops
    xc = _s5_chunk_states(u_rows, w_pair, n_chunks)
    xin_f, xin_b = _s5_scan(xc, a_step, n_chunks, n_ctx)
    return _s5_output(u_rows, toeplitz, xin_f, xin_b, vt_pair, n_chunks)


def _merge_kernel(x_ref, mod_ref, gain_ref, hf_ref, hb_ref, ys_ref, wo_ref, gmh_ref, gsgu_ref,
                  wsgu_ref, bsgu_ref, wglu_ref, bglu_ref, wsc_ref, wum_ref, wug_ref, wus_ref,
                  wuc_ref, wout_ref, o_ref, ybuf_ref, *, ctx_len, tile0):
    t = pl.program_id(1) + tile0
    x = x_ref[0]
    mod = mod_ref[0, 0]
    h = _modulated_norm(x, gain_ref[...], mod[0:1], mod[1:2]).astype(BF16)
    tm = x.shape[0]

    def proj(idx, width=MLSTM_WIDTH):
        start = idx * MLSTM_WIDTH
        return jnp.dot(h, wo_ref[:, start:start + width], preferred_element_type=F32)

    def gate(j):
        start = 6 * MLSTM_WIDTH + j * D_MODEL
        return jax.nn.sigmoid(jnp.dot(h, wo_ref[:, start:start + D_MODEL], preferred_element_type=F32))

    hm = hf_ref[0, 0] + hb_ref[0, 0]
    parts = []
    for hd in range(HEADS):
        hh = hm[:, hd * HEAD_DIM:(hd + 1) * HEAD_DIM]
        parts.append(hh * lax.rsqrt(jnp.mean(hh * hh, axis=-1, keepdims=True) + RMS_EPS))
    y_a = (jnp.concatenate(parts, axis=-1) * gmh_ref[...]) * jax.nn.sigmoid(proj(0))
    acc = gate(0) * jnp.dot(y_a.astype(BF16), wum_ref[...], preferred_element_type=F32)

    su = jax.nn.gelu(proj(1))
    sv = jax.nn.gelu(proj(2))
    mu = jnp.mean(sv, axis=-1, keepdims=True)
    cen = sv - mu
    var = jnp.mean(cen * cen, axis=-1, keepdims=True)
    vn = (cen * lax.rsqrt(var + RMS_EPS) * gsgu_ref[...]).astype(BF16)
    bias = bsgu_ref[...]
    rows = []
    for n in range(tm // SGU_CHUNK):
        cols = []
        for gi in range(SGU_GROUPS):
            blk = vn[n * SGU_CHUNK:(n + 1) * SGU_CHUNK, gi * SGU_GROUP_DIM:(gi + 1) * SGU_GROUP_DIM]
            cols.append(jnp.dot(wsgu_ref[gi], blk, preferred_element_type=F32) + bias[:, gi:gi + 1])
        rows.append(jnp.concatenate(cols, axis=-1))
    y_b = su * jnp.concatenate(rows, axis=0)
    acc = acc + gate(1) * jnp.dot(y_b.astype(BF16), wug_ref[...], preferred_element_type=F32)

    ys = jax.nn.gelu(_from_group_rows(ys_ref, ybuf_ref))
    y_c = ys * jax.nn.sigmoid(jnp.dot(ys.astype(BF16), wglu_ref[...], preferred_element_type=F32)
                              + bglu_ref[...])
    acc = acc + gate(2) * jnp.dot(y_c.astype(BF16), wus_ref[...], preferred_element_type=F32)

    first, last = _row_edges(t, tm, ctx_len)
    y_d = proj(3) * _conv3(proj(4) * proj(5), wsc_ref[...], first, last)
    acc = acc + gate(3) * jnp.dot(y_d.astype(BF16), wuc_ref[...], preferred_element_type=F32)

    o_ref[0] = x + mod[2:3] * jnp.dot(acc.astype(BF16), wout_ref[...], preferred_element_type=F32)


def _merge(xs, mods, gain, h_dirs, y_s5, lw, ctx_len, tile0):
    b, s, d = xs.shape
    tm = TOKEN_TILE
    tiles = s // tm
    tok = lambda width: pl.BlockSpec((1, tm, width), lambda i, t: (i, t + tile0, 0))
    hdir = lambda direction: pl.BlockSpec((1, 1, tm, MLSTM_WIDTH),
                                          lambda i, t: (direction, i, t + tile0, 0))
    group_rows = pl.BlockSpec((S5_GROUPS, tm // S5_CHUNK, S5_ROW),
                              lambda i, t: (0, i * tiles + t + tile0, 0))
    weights = (lw['w_out_side'], lw['g_mh'], lw['g_sgu'], lw['w_sgu'], lw['b_sgu_t'], lw['w_glu'],
               lw['b_glu'], lw['w_sconv'], lw['w_up_mlstm'], lw['w_up_sgu'], lw['w_up_s5'],
               lw['w_up_sconv'], lw['w_out'])
    return pl.pallas_call(
        functools.partial(_merge_kernel, ctx_len=ctx_len, tile0=tile0),
        out_shape=jax.ShapeDtypeStruct((b, s - tile0 * tm, d), F32),
        grid=(b, tiles - tile0),
        in_specs=[tok(d),
                  pl.BlockSpec((1, 1, 6, d), lambda i, t: (i, jnp.minimum(t + tile0, 1), 0, 0)),
                  _resident((1, d)), hdir(0), hdir(1), group_rows]
                 + [_resident(w.shape) for w in weights],
        out_specs=pl.BlockSpec((1, tm, d), lambda i, t: (i, t, 0)),
        scratch_shapes=[pltpu.VMEM((S5_WIDTH // LANES, tm, LANES), F32)],
        compiler_params=_params("arbitrary", "arbitrary"),
        name="merge",
    )(xs, mods, gain, h_dirs, h_dirs, y_s5, *weights)


def _ffn_kernel(x_ref, mod_ref, gain_ref, wg_ref, wu_ref, wd_ref, gf_ref, o_ref, *, final):
    x = x_ref[0]
    mod = mod_ref[0, 0]
    h = _modulated_norm(x, gain_ref[...], mod[3:4], mod[4:5]).astype(BF16)
    a = jnp.dot(h, wg_ref[...], preferred_element_type=F32)
    b = jnp.dot(h, wu_ref[...], preferred_element_type=F32)
    hid = ((a * jax.nn.sigmoid(a)) * b).astype(BF16)
    y = x + mod[5:6] * jnp.dot(hid, wd_ref[...], preferred_element_type=F32)
    if final:
        y = y * lax.rsqrt(jnp.mean(y * y, axis=-1, keepdims=True) + RMS_EPS) * gf_ref[...]
    o_ref[0] = y


def _ffn(xs, mods, gain, w_gate, w_up, w_down, g_final, tile0, final):
    b, s, d = xs.shape
    tm = TOKEN_TILE
    return pl.pallas_call(
        functools.partial(_ffn_kernel, final=final),
        out_shape=jax.ShapeDtypeStruct((b, s, d), F32),
        grid=(b, s // tm),
        in_specs=[pl.BlockSpec((1, tm, d), lambda i, t: (i, t, 0)),
                  pl.BlockSpec((1, 1, 6, d), lambda i, t: (i, jnp.minimum(t + tile0, 1), 0, 0)),
                  _resident((1, d)), _resident(w_gate.shape), _resident(w_up.shape),
                  _resident(w_down.shape), _resident((1, d))],
        out_specs=pl.BlockSpec((1, tm, d), lambda i, t: (i, t, 0)),
        compiler_params=_params("arbitrary", "arbitrary"),
        name="ffn_final" if final else "ffn",
    )(xs, mods, gain, w_gate, w_up, w_down, g_final)


def _layer_weights(l, w_in, b_gates, w_conv_qk, g_mh, g_sgu, w_sgu, b_sgu, w_glu, b_glu, w_sconv,
                   w_up_mlstm, w_up_sgu, w_up_s5, w_up_sconv, w_out, w_ffn_gate, w_ffn_up, w_ffn_down):
    w = w_in[l]
    gate_pad = ((0, 0), (0, LANES - 2 * HEADS))
    w_state = jnp.concatenate([
        w[:, :GATE_COL],
        jnp.pad(w[:, GATE_COL:GATE_COL + 2 * HEADS], gate_pad),
        jnp.pad(w[:, GATE_COL + 2 * HEADS:GATE_COL + 4 * HEADS], gate_pad),
        w[:, GATE_COL + 4 * HEADS:STATE_DIM]], axis=1).astype(BF16)
    bg = b_gates[l]
    b_pad = (0, LANES - 2 * HEADS)
    bg = jnp.concatenate([jnp.pad(bg[:2 * HEADS], b_pad), jnp.pad(bg[2 * HEADS:], b_pad)])
    return dict(
        w_state=w_state,
        b_gates=bg.reshape(1, 2 * LANES),
        w_conv=w_conv_qk[l].reshape(2 * CONV_K, MLSTM_WIDTH),
        w_out_side=w[:, STATE_DIM:].astype(BF16),
        g_mh=g_mh[l].reshape(1, -1),
        g_sgu=g_sgu[l].reshape(1, -1),
        w_sgu=w_sgu[l].astype(BF16),
        b_sgu_t=b_sgu[l].T,
        w_glu=w_glu[l].astype(BF16),
        b_glu=b_glu[l].reshape(1, -1),
        w_sconv=w_sconv[l],
        w_up_mlstm=w_up_mlstm[l].astype(BF16),
        w_up_sgu=w_up_sgu[l].astype(BF16),
        w_up_s5=w_up_s5[l].astype(BF16),
        w_up_sconv=w_up_sconv[l].astype(BF16),
        w_out=w_out[l].astype(BF16),
        w_ffn_gate=w_ffn_gate[l].astype(BF16),
        w_ffn_up=w_ffn_up[l].astype(BF16),
        w_ffn_down=w_ffn_down[l].astype(BF16),
    )


def kernel(x, c, ctx, c_ctx, w_mod, b_mod, g_norm_mix, g_norm_ffn, w_in, b_gates, w_conv_qk, g_mh, g_sgu, w_sgu, b_sgu, s5_a_re, s5_a_im, s5_log_dt, s5_b_re, s5_b_im, s5_c_re, s5_c_im, s5_d, w_glu, b_glu, w_sconv, w_up_mlstm, w_up_sgu, w_up_s5, w_up_sconv, w_out, w_ffn_gate, w_ffn_up, w_ffn_down, g_final):
    batch, seq, d = x.shape
    ctx_len = ctx.shape[1]
    depth = w_mod.shape[0]
    assert d == D_MODEL and ctx_len == TOKEN_TILE and seq % TOKEN_TILE == 0 and batch == S5_BATCH
    ctx_tiles = ctx_len // TOKEN_TILE

    c_rows = jnp.zeros((8, d), F32).at[:batch].set(c).at[batch].set(c_ctx)
    mod_all = _modulation(c_rows, w_mod, b_mod).reshape(depth, 8, 6, d)
    xs = jnp.concatenate([ctx, x], axis=1)
    s5_ops_all = _s5_operators(s5_a_re, s5_a_im, s5_log_dt, s5_b_re, s5_b_im, s5_c_re, s5_c_im, s5_d)

    out = None
    for l in range(depth):
        last = l == depth - 1
        mod_x = mod_all[l, :batch]
        mod_c = jnp.broadcast_to(mod_all[l, batch][None], (batch, 6, d))
        mods = jnp.stack([mod_c, mod_x], axis=1)
        lw = _layer_weights(l, w_in, b_gates, w_conv_qk, g_mh, g_sgu, w_sgu, b_sgu, w_glu, b_glu,
                            w_sconv, w_up_mlstm, w_up_sgu, w_up_s5, w_up_sconv, w_out,
                            w_ffn_gate, w_ffn_up, w_ffn_down)
        s5_ops = tuple(op[l] for op in s5_ops_all)
        gain_mix = g_norm_mix[l].reshape(1, d)
        gain_ffn = g_norm_ffn[l].reshape(1, d)

        q, k, v, gates, u_rows = _state_proj(xs, mods, gain_mix, lw['w_state'], lw['b_gates'],
                                             lw['w_conv'], ctx_len)
        h_dirs = _mlstm(q, k, v, gates, ctx_len)
        y_s5 = _s5_mixer(u_rows, s5_ops, (ctx_len + seq) // S5_CHUNK, ctx_len // S5_CHUNK)
        tile0 = ctx_tiles if last else 0
        x_mid = _merge(xs, mods, gain_mix, h_dirs, y_s5, lw, ctx_len, tile0)
        xs = _ffn(x_mid, mods, gain_ffn, lw['w_ffn_gate'], lw['w_ffn_up'], lw['w_ffn_down'],
                  g_final.reshape(1, d), tile0, last)
        out = xs
    return out
```

```python
import functools
import math

import jax
import jax.numpy as jnp
from jax import lax
from jax.experimental import pallas as pl
from jax.experimental.pallas import tpu as pltpu

F32 = jnp.float32
BF16 = jnp.bfloat16

D_MODEL = 1024
GRID_W = 64
N_BRANCH = 4
RMS_EPS = 1e-6
CONV_K = 3
HEADS = 4
HEAD_DIM = 128
MLSTM_WIDTH = HEADS * HEAD_DIM
MLSTM_CHUNK = 128
SGU_GROUPS = 4
SGU_GROUP_DIM = 128
SGU_WIDTH = SGU_GROUPS * SGU_GROUP_DIM
SGU_CHUNK = 128
S5_GROUP_DIM = 16
S5_GROUPS = 24
S5_WIDTH = S5_GROUPS * S5_GROUP_DIM
S5_STATE = 64
SCONV_WIDTH = 512
STATE_DIM = 3 * MLSTM_WIDTH + 4 * HEADS + S5_WIDTH
GATE_COL = 3 * MLSTM_WIDTH

LANES = 128
TOKEN_TILE = 256
S5_CHUNK = 16
S5_ROW = S5_CHUNK * S5_GROUP_DIM
S5_PAIR = 2 * S5_STATE
S5_SCAN_PAIRS = 2
S5_BATCH = 4
Z_GATE_COL = 2 * MLSTM_WIDTH
U_COL = Z_GATE_COL + 2 * LANES
STATE_COLS = U_COL + S5_WIDTH
VMEM_LIMIT = 52 * 1024 * 1024


def _resident(shape):
    zeros = (0,) * len(shape)
    return pl.BlockSpec(shape, lambda *_: zeros, pipeline_mode=pl.Buffered(1))


def _params(*semantics):
    return pltpu.CompilerParams(dimension_semantics=semantics, vmem_limit_bytes=VMEM_LIMIT)


def _modulated_norm(x, gain, shift, scale):
    y = x * lax.rsqrt(jnp.mean(x * x, axis=-1, keepdims=True) + RMS_EPS) * gain
    return y * (1.0 + scale) + shift


def _split3(a):
    hi = a.astype(BF16)
    r1 = a - hi.astype(F32)
    mid = r1.astype(BF16)
    lo = (r1 - mid.astype(F32)).astype(BF16)
    return hi, mid, lo


def _conv3(a, w, first, last):
    n = a.shape[0]
    prev = jnp.where(first, 0.0, pltpu.roll(a, 1, 0))
    nxt = jnp.where(last, 0.0, pltpu.roll(a, n - 1, 0))
    return prev * w[0:1] + a * w[1:2] + nxt * w[2:3]


def _row_edges(tile_idx, n_rows, ctx_len):
    period = jnp.where(tile_idx == 0, ctx_len, GRID_W)
    pos = lax.broadcasted_iota(jnp.int32, (n_rows, 1), 0) & (period - 1)
    return pos == 0, pos == period - 1


def _mod_kernel(c_ref, w_ref, b_ref, o_ref):
    a = c_ref[...]
    a = a * jax.nn.sigmoid(a)
    a_hi = a.astype(BF16)
    a_lo = (a - a_hi.astype(F32)).astype(BF16)
    w = w_ref[0]
    w_hi = w.astype(BF16)
    w_lo = (w - w_hi.astype(F32)).astype(BF16)
    acc = jnp.dot(a_hi, w_hi, preferred_element_type=F32)
    acc = acc + jnp.dot(a_lo, w_hi, preferred_element_type=F32)
    acc = acc + jnp.dot(a_hi, w_lo, preferred_element_type=F32)
    o_ref[0] = acc + b_ref[0]


def _modulation(c_rows, w_mod, b_mod):
    depth, d, n = w_mod.shape
    tn = 1536
    return pl.pallas_call(
        _mod_kernel,
        out_shape=jax.ShapeDtypeStruct((depth, 8, n), F32),
        grid=(depth, n // tn),
        in_specs=[
            pl.BlockSpec((8, d), lambda l, j: (0, 0)),
            pl.BlockSpec((1, d, tn), lambda l, j: (l, 0, j)),
            pl.BlockSpec((1, 1, tn), lambda l, j: (l, 0, j)),
        ],
        out_specs=pl.BlockSpec((1, 8, tn), lambda l, j: (l, 0, j)),
        compiler_params=_params("arbitrary", "arbitrary"),
        name="modulation",
    )(c_rows, w_mod, b_mod.reshape(depth, 1, n))


def _to_group_rows(z, u_ref, buf_ref):
    n_chunks = z.shape[0] // S5_CHUNK
    per_block = LANES // S5_GROUP_DIM
    for gb in range(S5_WIDTH // LANES):
        buf_ref[gb] = z[:, gb * LANES:(gb + 1) * LANES]
    for s in range(S5_CHUNK):
        for gb in range(S5_WIDTH // LANES):
            zs = buf_ref[gb, pl.ds(s, n_chunks, stride=S5_CHUNK), :]
            for gl in range(per_block):
                u_ref[gb * per_block + gl, :, s * S5_GROUP_DIM:(s + 1) * S5_GROUP_DIM] = (
                    zs[:, gl * S5_GROUP_DIM:(gl + 1) * S5_GROUP_DIM])


def _from_group_rows(y_ref, buf_ref):
    n_chunks = y_ref.shape[1]
    per_block = LANES // S5_GROUP_DIM
    for s in range(S5_CHUNK):
        for gb in range(S5_WIDTH // LANES):
            pieces = [y_ref[gb * per_block + gl, :, s * S5_GROUP_DIM:(s + 1) * S5_GROUP_DIM]
                      for gl in range(per_block)]
            buf_ref[gb, pl.ds(s, n_chunks, stride=S5_CHUNK), :] = jnp.concatenate(pieces, axis=-1)
    return jnp.concatenate([buf_ref[gb] for gb in range(S5_WIDTH // LANES)], axis=-1)


def _state_proj_kernel(x_ref, mod_ref, gain_ref, w_ref, wvt_ref, wgt_ref, bg_ref, bgt_ref, wc_ref, eye_ref,
                       qt_ref, k_ref, vt_ref, g_ref, gt_ref, u_ref, ubuf_ref, *, ctx_len):
    t = pl.program_id(1)
    mod = mod_ref[0, 0]
    h = _modulated_norm(x_ref[0], gain_ref[...], mod[0:1], mod[1:2]).astype(BF16)
    z = jnp.dot(h, w_ref[...], preferred_element_type=F32)
    first, last = _row_edges(t, z.shape[0], ctx_len)
    wc = wc_ref[...]
    w = MLSTM_WIDTH
    q = _conv3(z[:, 0:w], wc[0:3], first, last)
    k = _conv3(z[:, w:2 * w], wc[3:6], first, last)
    q = (q * jax.nn.sigmoid(q)).astype(BF16)
    k_ref[0] = ((k * jax.nn.sigmoid(k)) * (HEAD_DIM ** -0.5)).astype(BF16)
    g_ref[0] = z[:, Z_GATE_COL:Z_GATE_COL + 2 * LANES] + bg_ref[...]
    contract_last = (((1,), (1,)), ((), ()))
    qt_ref[0] = lax.dot_general(eye_ref[...], q, contract_last, preferred_element_type=F32).astype(BF16)
    vt_ref[0] = lax.dot_general(wvt_ref[...], h, contract_last, preferred_element_type=F32).astype(BF16)
    gt_ref[0] = lax.dot_general(wgt_ref[...], h, contract_last, preferred_element_type=F32) + bgt_ref[...]
    _to_group_rows(z[:, U_COL:U_COL + S5_WIDTH], u_ref, ubuf_ref)


def _state_proj(xs, mods, gain, lw, ctx_len):
    b, s, d = xs.shape
    tm = TOKEN_TILE
    tiles = s // tm
    tok = lambda width: pl.BlockSpec((1, tm, width), lambda i, t: (i, t, 0))
    chan = lambda height: pl.BlockSpec((1, height, tm), lambda i, t: (i, 0, t))
    weights = (lw['w_state'], lw['w_v_t'], lw['w_gates_t'], lw['b_gates'], lw['b_gates_t'], lw['w_conv'],
               jnp.eye(MLSTM_WIDTH, dtype=BF16))
    return pl.pallas_call(
        functools.partial(_state_proj_kernel, ctx_len=ctx_len),
        out_shape=(
            jax.ShapeDtypeStruct((b, MLSTM_WIDTH, s), BF16),
            jax.ShapeDtypeStruct((b, s, MLSTM_WIDTH), BF16),
            jax.ShapeDtypeStruct((b, MLSTM_WIDTH, s), BF16),
            jax.ShapeDtypeStruct((b, s, 2 * LANES), F32),
            jax.ShapeDtypeStruct((b, 2 * LANES, s), F32),
            jax.ShapeDtypeStruct((S5_GROUPS, b * s // S5_CHUNK, S5_ROW), F32),
        ),
        grid=(b, s // tm),
        in_specs=[
            tok(d),
            pl.BlockSpec((1, 1, 6, d), lambda i, t: (i, jnp.minimum(t, 1), 0, 0)),
            _resident((1, d)),
        ] + [_resident(w.shape) for w in weights],
        out_specs=(chan(MLSTM_WIDTH), tok(MLSTM_WIDTH), chan(MLSTM_WIDTH), tok(2 * LANES), chan(2 * LANES),
                   pl.BlockSpec((S5_GROUPS, tm // S5_CHUNK, S5_ROW), lambda i, t: (0, i * tiles + t, 0))),
        scratch_shapes=[pltpu.VMEM((S5_WIDTH // LANES, tm, LANES), F32)],
        compiler_params=_params("arbitrary", "arbitrary"),
        name="state_proj",
    )(xs, mods, gain, *weights)


def _mlstm_chunk(direction, qt_ref, k_ref, vt_ref, g_ref, gt_ref, h_ref, ct_ref, n_ref, m_ref):
    lc = MLSTM_CHUNK
    sign = 1 - 2 * direction
    row = lax.broadcasted_iota(jnp.int32, (lc, lc), 0)
    col = lax.broadcasted_iota(jnp.int32, (lc, lc), 1)
    tri = (col - row) * sign >= 0
    tri_b = jnp.where(tri, 1.0, 0.0).astype(BF16)
    tri_t = jnp.where((row - col) * sign >= 0, 1.0, 0.0).astype(BF16)

    g = g_ref[0]
    gt = gt_ref[0]
    lf = jax.nn.log_sigmoid(g)
    lf_t = jax.nn.log_sigmoid(gt)
    bcum = sum(jnp.dot(tri_t, part, preferred_element_type=F32) for part in _split3(lf))
    bcum_t = sum(jnp.dot(part, tri_b, preferred_element_type=F32) for part in _split3(lf_t))
    total = jnp.sum(lf, axis=0, keepdims=True)

    qt = qt_ref[0]
    k = k_ref[0]
    vt = vt_ref[0]
    for hd in range(HEADS):
        sl = slice(hd * HEAD_DIM, (hd + 1) * HEAD_DIM)
        qt_h, k_h, vt_h = qt[sl, :], k[:, sl], vt[sl, :]
        a_bc = jnp.broadcast_to(g[:, hd:hd + 1] - bcum[:, HEADS + hd:HEADS + hd + 1], (lc, lc))
        b_row = bcum_t[HEADS + hd:HEADS + hd + 1, :]
        b_last = total[:, HEADS + hd:HEADS + hd + 1]
        m = m_ref[direction, hd][:, 0:1]
        ct = ct_ref[direction, hd]
        n_vec = n_ref[direction, hd]

        m_new = b_last + jnp.maximum(m, jnp.max(a_bc, axis=0, keepdims=True)[:, 0:1])
        w_bc = jnp.exp(a_bc + (b_last - m_new))
        decay = jnp.exp(b_last + m - m_new)

        log_w = jnp.where(tri, a_bc + b_row, -jnp.inf)
        log_inter = b_row + m
        m_t = jnp.maximum(log_inter, jnp.max(log_w, axis=0, keepdims=True))
        inter = jnp.exp(log_inter - m_t)
        s_t = jnp.dot(k_h, qt_h, preferred_element_type=F32) * jnp.exp(log_w - m_t)
        n_rows = jnp.broadcast_to(n_vec, (8, HEAD_DIM)).astype(BF16)
        qn = jnp.dot(n_rows, qt_h, preferred_element_type=F32)[0:1]
        den = inter * qn + jnp.sum(s_t, axis=0, keepdims=True)
        r_den = 1.0 / jnp.maximum(jnp.abs(den), jnp.exp(-m_t))
        lhs = jnp.concatenate([vt_h, ct.astype(BF16)], axis=1)
        rhs = jnp.concatenate([s_t.astype(BF16), (qt_h.astype(F32) * inter).astype(BF16)], axis=0)
        h_ref[0, :, sl] = (jnp.dot(lhs, rhs, preferred_element_type=F32) * r_den).T

        wk = w_bc * k_h.astype(F32)
        ct_ref[direction, hd] = decay * ct + jnp.dot(vt_h, wk.astype(BF16), preferred_element_type=F32)
        n_ref[direction, hd] = decay * n_vec + jnp.sum(wk, axis=0, keepdims=True)
        m_ref[direction, hd] = jnp.broadcast_to(m_new, (1, LANES))


def _mlstm_kernel(qtf_ref, kf_ref, vtf_ref, gf_ref, gtf_ref, qtb_ref, kb_ref, vtb_ref, gb_ref, gtb_ref,
                  hf_ref, hb_ref, ct_ref, n_ref, m_ref):
    @pl.when(pl.program_id(1) == 0)
    def _():
        ct_ref[...] = jnp.zeros_like(ct_ref)
        n_ref[...] = jnp.zeros_like(n_ref)
        m_ref[...] = jnp.zeros_like(m_ref)

    _mlstm_chunk(0, qtf_ref, kf_ref, vtf_ref, gf_ref, gtf_ref, hf_ref, ct_ref, n_ref, m_ref)
    _mlstm_chunk(1, qtb_ref, kb_ref, vtb_ref, gb_ref, gtb_ref, hb_ref, ct_ref, n_ref, m_ref)


def _mlstm(qt, k, vt, gates, gates_t, ctx_len):
    b, s, w = k.shape
    lc = MLSTM_CHUNK
    n_chunks = s // lc
    n_ctx = ctx_len // lc

    def back(j):
        return jnp.where(j < n_ctx, n_ctx - 1 - j, n_chunks + n_ctx - 1 - j)

    def specs(direction, chunk):
        return [pl.BlockSpec((1, w, lc), lambda i, j: (i, 0, chunk(j))),
                pl.BlockSpec((1, lc, w), lambda i, j: (i, chunk(j), 0)),
                pl.BlockSpec((1, w, lc), lambda i, j: (i, 0, chunk(j))),
                pl.BlockSpec((1, lc, LANES), lambda i, j: (i, chunk(j), direction)),
                pl.BlockSpec((1, LANES, lc), lambda i, j: (i, direction, chunk(j)))]

    fwd = lambda j: j
    out = jax.ShapeDtypeStruct((b, s, w), F32)
    return pl.pallas_call(
        _mlstm_kernel,
        out_shape=(out, out),
        grid=(b, n_chunks),
        in_specs=specs(0, fwd) + specs(1, back),
        out_specs=(pl.BlockSpec((1, lc, w), lambda i, j: (i, j, 0)),
                   pl.BlockSpec((1, lc, w), lambda i, j: (i, back(j), 0))),
        scratch_shapes=[
            pltpu.VMEM((2, HEADS, HEAD_DIM, HEAD_DIM), F32),
            pltpu.VMEM((2, HEADS, 1, HEAD_DIM), F32),
            pltpu.VMEM((2, HEADS, 1, LANES), F32),
        ],
        compiler_params=_params("arbitrary", "arbitrary"),
        name="mlstm",
    )(qt, k, vt, gates, gates_t, qt, k, vt, gates, gates_t)


def _s5_operators(a_re, a_im, log_dt, b_re, b_im, c_re, c_im, d_skip):
    hp = lax.Precision.HIGHEST
    lch = S5_CHUNK
    depth = a_re.shape[0]
    a_re = a_re.astype(F32)
    a_im = a_im.astype(F32)
    dt = jnp.exp(log_dt.astype(F32))[..., None]
    la_re, la_im = dt * a_re, dt * a_im
    mag = jnp.exp(la_re)
    ab_re, ab_im = mag * jnp.cos(la_im), mag * jnp.sin(la_im)
    nr, ni = ab_re - 1.0, ab_im
    den = a_re * a_re + a_im * a_im
    f_re = ((nr * a_re + ni * a_im) / den)[:, :, :, None, :]
    f_im = ((ni * a_re - nr * a_im) / den)[:, :, :, None, :]
    bt_re = jnp.swapaxes(b_re.astype(F32), -1, -2)[:, None]
    bt_im = jnp.swapaxes(b_im.astype(F32), -1, -2)[:, None]
    bb_re = f_re * bt_re - f_im * bt_im
    bb_im = f_re * bt_im + f_im * bt_re
    c_re = c_re.astype(F32)
    c_im = c_im.astype(F32)

    n = jnp.arange(-(lch - 1), lch + 1, dtype=F32)[:, None]
    mg = jnp.exp(n * la_re[:, :, :, None, :])
    ang = n * la_im[:, :, :, None, :]
    pw_re, pw_im = mg * jnp.cos(ang), mg * jnp.sin(ang)
    zero = lch - 1

    def powers(direction, first, step):
        start = zero + first
        stop = start + step * lch
        sl = slice(start, stop if stop >= 0 else None, step)
        return pw_re[:, direction, :, sl, None, :], pw_im[:, direction, :, sl, None, :]

    def times(x_re, x_im, p):
        return x_re * p[0] - x_im * p[1], x_re * p[1] + x_im * p[0]

    def rows256(x):
        return x.reshape(depth, S5_GROUPS, S5_ROW, S5_STATE)

    bbf = (bb_re[:, 0, :, None], bb_im[:, 0, :, None])
    bbb = (bb_re[:, 1, :, None], bb_im[:, 1, :, None])
    cc = (c_re[:, :, None], c_im[:, :, None])

    def response(bb, direction, sign):
        l_re, l_im = times(*bb, powers(direction, 0, -sign))
        r_re, r_im = times(*cc, powers(direction, 0, sign))
        left = jnp.concatenate([rows256(l_re), rows256(l_im)], axis=-1)
        right = jnp.concatenate([rows256(r_re), -rows256(r_im)], axis=-1)
        return jnp.einsum('lgap,lgbp->lgab', left, right, precision=hp)

    src = (jnp.arange(S5_ROW) // S5_GROUP_DIM)[:, None]
    tgt = (jnp.arange(S5_ROW) // S5_GROUP_DIM)[None, :]
    d_rows = jnp.tile(d_skip.astype(F32).reshape(depth, S5_GROUPS, 1, S5_GROUP_DIM), (1, 1, 1, lch))
    toeplitz = (jnp.where(tgt >= src, response(bbf, 0, 1), 0.0)
                + jnp.where(src >= tgt, response(bbb, 1, -1), 0.0)
                + jnp.eye(S5_ROW, dtype=F32) * d_rows)

    def pair_block(x_re, x_im):
        x_re = x_re.reshape(depth, S5_GROUPS // 2, 2, S5_ROW, S5_STATE)
        x_im = x_im.reshape(depth, S5_GROUPS // 2, 2, S5_ROW, S5_STATE)
        z = jnp.zeros_like(x_re[:, :, 0])
        top = jnp.concatenate([x_re[:, :, 0], z, x_im[:, :, 0], z], axis=-1)
        bottom = jnp.concatenate([z, x_re[:, :, 1], z, x_im[:, :, 1]], axis=-1)
        return jnp.concatenate([top, bottom], axis=-2)

    def state_map(x, p):
        re, im = times(*x, p)
        return rows256(re), rows256(im)

    w_f = pair_block(*state_map(bbf, powers(0, lch - 1, -1)))
    w_b = pair_block(*state_map(bbb, powers(1, 0, 1)))
    w_pair = jnp.concatenate([w_f, w_b], axis=-1)
    vf_re, vf_im = state_map(cc, powers(0, 1, 1))
    vb_re, vb_im = state_map(cc, powers(1, lch, -1))
    vt_pair = jnp.stack([pair_block(vf_re, -vf_im), pair_block(vb_re, -vb_im)], axis=1)

    al_re, al_im = pw_re[:, :, :, zero + lch], pw_im[:, :, :, zero + lch]
    def rows(a):
        a = a.reshape(depth, 2, 1, S5_GROUPS * S5_STATE)
        return jnp.broadcast_to(a, (depth, 2, S5_BATCH, S5_GROUPS * S5_STATE)).reshape(depth, 2 * S5_BATCH, -1)
    a_step = jnp.stack([rows(al_re), rows(al_im)], axis=1)
    return toeplitz.astype(BF16), w_pair.astype(BF16), vt_pair.astype(BF16), a_step


def _s5_chunk_state_kernel(u_ref, w_ref, x_ref, *, n_chunks):
    u = jnp.concatenate([u_ref[0], u_ref[1]], axis=-1).astype(BF16)
    x = jnp.dot(u, w_ref[0], preferred_element_type=F32)
    for b in range(S5_BATCH):
        xb = x[b * n_chunks:(b + 1) * n_chunks]
        for direction in range(2):
            for part in range(2):
                lo = (2 * direction + part) * S5_PAIR
                x_ref[part, pl.ds(direction * S5_BATCH + b, n_chunks, stride=2 * S5_BATCH), :] = (
                    xb[:, lo:lo + S5_PAIR])


def _s5_chunk_states(u_rows, w_pair, n_chunks):
    g, rows, _ = u_rows.shape
    n_pairs = g // 2
    return pl.pallas_call(
        functools.partial(_s5_chunk_state_kernel, n_chunks=n_chunks),
        out_shape=jax.ShapeDtypeStruct((2 * n_pairs, n_chunks * 2 * S5_BATCH, S5_PAIR), F32),
        grid=(n_pairs,),
        in_specs=[pl.BlockSpec((2, rows, S5_ROW), lambda i: (i, 0, 0)),
                  pl.BlockSpec((1, 2 * S5_ROW, 4 * S5_PAIR), lambda i: (i, 0, 0))],
        out_specs=pl.BlockSpec((2, n_chunks * 2 * S5_BATCH, S5_PAIR), lambda i: (i, 0, 0)),
        compiler_params=_params("arbitrary"),
        name="s5_chunk_states",
    )(u_rows, w_pair)


def _s5_scan_kernel(x_ref, a_ref, of_ref, ob_ref, *, n_chunks, n_ctx):
    tile = 2 * S5_BATCH
    n_slabs = x_ref.shape[0]
    a_re = a_ref[0]
    a_im = a_ref[1]
    fwd_rows = lax.broadcasted_iota(jnp.int32, (tile, S5_PAIR), 0) < S5_BATCH

    def body(i, state):
        cb = jnp.where(i < n_ctx, n_ctx - 1 - i, n_chunks + n_ctx - 1 - i)
        rf = pl.multiple_of(i * tile, tile)
        rb = pl.multiple_of(cb * tile, tile)
        new_state = []
        for p in range(n_slabs // 2):
            re, im = state[2 * p], state[2 * p + 1]
            for j, val in ((2 * p, re), (2 * p + 1, im)):
                of_ref[j, pl.ds(rf, tile), :] = val
                ob_ref[j, pl.ds(rb, tile), :] = val
            in_re = jnp.where(fwd_rows, x_ref[2 * p, pl.ds(rf, tile), :], x_ref[2 * p, pl.ds(rb, tile), :])
            in_im = jnp.where(fwd_rows, x_ref[2 * p + 1, pl.ds(rf, tile), :],
                              x_ref[2 * p + 1, pl.ds(rb, tile), :])
            ar, ai = a_re[:, p * S5_PAIR:(p + 1) * S5_PAIR], a_im[:, p * S5_PAIR:(p + 1) * S5_PAIR]
            new_state.append(ar * re - ai * im + in_re)
            new_state.append(ar * im + ai * re + in_im)
        return tuple(new_state)

    zero = jnp.zeros((tile, S5_PAIR), F32)
    lax.fori_loop(0, n_chunks, body, (zero,) * n_slabs, unroll=4)


def _s5_scan(xc, a_step, n_chunks, n_ctx):
    slabs, rows, lanes = xc.shape
    per_step = 2 * S5_SCAN_PAIRS
    blk = pl.BlockSpec((per_step, rows, lanes), lambda i: (i, 0, 0))
    out = jax.ShapeDtypeStruct(xc.shape, F32)
    return pl.pallas_call(
        functools.partial(_s5_scan_kernel, n_chunks=n_chunks, n_ctx=n_ctx),
        out_shape=(out, out),
        grid=(slabs // per_step,),
        in_specs=[blk, pl.BlockSpec((2, 2 * S5_BATCH, S5_SCAN_PAIRS * S5_PAIR), lambda i: (0, 0, i))],
        out_specs=(blk, blk),
        compiler_params=_params("arbitrary"),
        name="s5_scan",
    )(xc, a_step)


def _s5_output_kernel(u_ref, t_ref, xf_ref, xb_ref, v_ref, y_ref, *, n_chunks):
    stride = 2 * S5_BATCH

    def batch_rows(ref, first_row):
        return jnp.concatenate(
            [jnp.concatenate([ref[part, pl.ds(first_row + b, n_chunks, stride=stride), :]
                              for part in range(2)], axis=-1) for b in range(S5_BATCH)], axis=0)

    xf = batch_rows(xf_ref, 0)
    xb = batch_rows(xb_ref, S5_BATCH)
    last = (((1,), (1,)), ((), ()))
    y = lax.dot_general(xf.astype(BF16), v_ref[0, 0], last, preferred_element_type=F32)
    y = y + lax.dot_general(xb.astype(BF16), v_ref[1, 0], last, preferred_element_type=F32)
    for a in range(2):
        within = jnp.dot(u_ref[a].astype(BF16), t_ref[a], preferred_element_type=F32)
        y_ref[a] = within + y[:, a * S5_ROW:(a + 1) * S5_ROW]


def _s5_output(u_rows, t_pair, xin_f, xin_b, v_pair, n_chunks):
    g, rows, _ = u_rows.shape
    n_pairs = g // 2
    col = pl.BlockSpec((2,) + xin_f.shape[1:], lambda i: (i, 0, 0))
    return pl.pallas_call(
        functools.partial(_s5_output_kernel, n_chunks=n_chunks),
        out_shape=jax.ShapeDtypeStruct((g, rows, S5_ROW), F32),
        grid=(n_pairs,),
        in_specs=[pl.BlockSpec((2, rows, S5_ROW), lambda i: (i, 0, 0)),
                  pl.BlockSpec((2, S5_ROW, S5_ROW), lambda i: (i, 0, 0)),
                  col, col,
                  pl.BlockSpec((2, 1, 2 * S5_ROW, 2 * S5_PAIR), lambda i: (0, i, 0, 0))],
        out_specs=pl.BlockSpec((2, rows, S5_ROW), lambda i: (i, 0, 0)),
        compiler_params=_params("arbitrary"),
        name="s5_output",
    )(u_rows, t_pair, xin_f, xin_b, v_pair)


def _s5_mixer(u_rows, ops, n_chunks, n_ctx):
    toeplitz, w_pair, vt_pair, a_step = ops
    xc = _s5_chunk_states(u_rows, w_pair, n_chunks)
    xin_f, xin_b = _s5_scan(xc, a_step, n_chunks, n_ctx)
    return _s5_output(u_rows, toeplitz, xin_f, xin_b, vt_pair, n_chunks)


def _merge_kernel(x_ref, mod_ref, gain_ref, hf_ref, hb_ref, ys_ref, wo_ref, gmh_ref, gsgu_ref,
                  wsgu_ref, bsgu_ref, wglu_ref, bglu_ref, wsc_ref, wum_ref, wug_ref, wus_ref,
                  wuc_ref, wout_ref, o_ref, ybuf_ref, *, ctx_len, tile0):
    t = pl.program_id(1) + tile0
    x = x_ref[0]
    mod = mod_ref[0, 0]
    h = _modulated_norm(x, gain_ref[...], mod[0:1], mod[1:2]).astype(BF16)
    tm = x.shape[0]

    def proj(idx, width=MLSTM_WIDTH):
        start = idx * MLSTM_WIDTH
        return jnp.dot(h, wo_ref[:, start:start + width], preferred_element_type=F32)

    def gate(j):
        start = 6 * MLSTM_WIDTH + j * D_MODEL
        return jax.nn.sigmoid(jnp.dot(h, wo_ref[:, start:start + D_MODEL], preferred_element_type=F32))

    hm = hf_ref[0] + hb_ref[0]
    parts = []
    for hd in range(HEADS):
        hh = hm[:, hd * HEAD_DIM:(hd + 1) * HEAD_DIM]
        parts.append(hh * lax.rsqrt(jnp.mean(hh * hh, axis=-1, keepdims=True) + RMS_EPS))
    y_a = (jnp.concatenate(parts, axis=-1) * gmh_ref[...]) * jax.nn.sigmoid(proj(0))
    acc = gate(0) * jnp.dot(y_a.astype(BF16), wum_ref[...], preferred_element_type=F32)

    su = jax.nn.gelu(proj(1))
    sv = jax.nn.gelu(proj(2))
    mu = jnp.mean(sv, axis=-1, keepdims=True)
    cen = sv - mu
    var = jnp.mean(cen * cen, axis=-1, keepdims=True)
    vn = (cen * lax.rsqrt(var + RMS_EPS) * gsgu_ref[...]).astype(BF16)
    bias = bsgu_ref[...]
    rows = []
    for n in range(tm // SGU_CHUNK):
        cols = []
        for gi in range(SGU_GROUPS):
            blk = vn[n * SGU_CHUNK:(n + 1) * SGU_CHUNK, gi * SGU_GROUP_DIM:(gi + 1) * SGU_GROUP_DIM]
            cols.append(jnp.dot(wsgu_ref[gi], blk, preferred_element_type=F32) + bias[:, gi:gi + 1])
        rows.append(jnp.concatenate(cols, axis=-1))
    y_b = su * jnp.concatenate(rows, axis=0)
    acc = acc + gate(1) * jnp.dot(y_b.astype(BF16), wug_ref[...], preferred_element_type=F32)

    ys = jax.nn.gelu(_from_group_rows(ys_ref, ybuf_ref))
    y_c = ys * jax.nn.sigmoid(jnp.dot(ys.astype(BF16), wglu_ref[...], preferred_element_type=F32)
                              + bglu_ref[...])
    acc = acc + gate(2) * jnp.dot(y_c.astype(BF16), wus_ref[...], preferred_element_type=F32)

    first, last = _row_edges(t, tm, ctx_len)
    y_d = proj(3) * _conv3(proj(4) * proj(5), wsc_ref[...], first, last)
    acc = acc + gate(3) * jnp.dot(y_d.astype(BF16), wuc_ref[...], preferred_element_type=F32)

    o_ref[0] = x + mod[2:3] * jnp.dot(acc.astype(BF16), wout_ref[...], preferred_element_type=F32)


def _merge(xs, mods, gain, h_fwd, h_bwd, y_s5, lw, ctx_len, tile0):
    b, s, d = xs.shape
    tm = TOKEN_TILE
    tiles = s // tm
    tok = lambda width: pl.BlockSpec((1, tm, width), lambda i, t: (i, t + tile0, 0))
    group_rows = pl.BlockSpec((S5_GROUPS, tm // S5_CHUNK, S5_ROW),
                              lambda i, t: (0, i * tiles + t + tile0, 0))
    weights = (lw['w_out_side'], lw['g_mh'], lw['g_sgu'], lw['w_sgu'], lw['b_sgu_t'], lw['w_glu'],
               lw['b_glu'], lw['w_sconv'], lw['w_up_mlstm'], lw['w_up_sgu'], lw['w_up_s5'],
               lw['w_up_sconv'], lw['w_out'])
    return pl.pallas_call(
        functools.partial(_merge_kernel, ctx_len=ctx_len, tile0=tile0),
        out_shape=jax.ShapeDtypeStruct((b, s - tile0 * tm, d), F32),
        grid=(b, tiles - tile0),
        in_specs=[tok(d),
                  pl.BlockSpec((1, 1, 6, d), lambda i, t: (i, jnp.minimum(t + tile0, 1), 0, 0)),
                  _resident((1, d)), tok(MLSTM_WIDTH), tok(MLSTM_WIDTH), group_rows]
                 + [_resident(w.shape) for w in weights],
        out_specs=pl.BlockSpec((1, tm, d), lambda i, t: (i, t, 0)),
        scratch_shapes=[pltpu.VMEM((S5_WIDTH // LANES, tm, LANES), F32)],
        compiler_params=_params("arbitrary", "arbitrary"),
        name="merge",
    )(xs, mods, gain, h_fwd, h_bwd, y_s5, *weights)


def _ffn_kernel(x_ref, mod_ref, gain_ref, wg_ref, wu_ref, wd_ref, gf_ref, o_ref, *, final):
    x = x_ref[0]
    mod = mod_ref[0, 0]
    h = _modulated_norm(x, gain_ref[...], mod[3:4], mod[4:5]).astype(BF16)
    a = jnp.dot(h, wg_ref[...], preferred_element_type=F32)
    b = jnp.dot(h, wu_ref[...], preferred_element_type=F32)
    hid = ((a * jax.nn.sigmoid(a)) * b).astype(BF16)
    y = x + mod[5:6] * jnp.dot(hid, wd_ref[...], preferred_element_type=F32)
    if final:
        y = y * lax.rsqrt(jnp.mean(y * y, axis=-1, keepdims=True) + RMS_EPS) * gf_ref[...]
    o_ref[0] = y


def _ffn(xs, mods, gain, w_gate, w_up, w_down, g_final, tile0, final):
    b, s, d = xs.shape
    tm = TOKEN_TILE
    return pl.pallas_call(
        functools.partial(_ffn_kernel, final=final),
        out_shape=jax.ShapeDtypeStruct((b, s, d), F32),
        grid=(b, s // tm),
        in_specs=[pl.BlockSpec((1, tm, d), lambda i, t: (i, t, 0)),
                  pl.BlockSpec((1, 1, 6, d), lambda i, t: (i, jnp.minimum(t + tile0, 1), 0, 0)),
                  _resident((1, d)), _resident(w_gate.shape), _resident(w_up.shape),
                  _resident(w_down.shape), _resident((1, d))],
        out_specs=pl.BlockSpec((1, tm, d), lambda i, t: (i, t, 0)),
        compiler_params=_params("arbitrary", "arbitrary"),
        name="ffn_final" if final else "ffn",
    )(xs, mods, gain, w_gate, w_up, w_down, g_final)


def _layer_weights(l, w_in, b_gates, w_conv_qk, g_mh, g_sgu, w_sgu, b_sgu, w_glu, b_glu, w_sconv,
                   w_up_mlstm, w_up_sgu, w_up_s5, w_up_sconv, w_out, w_ffn_gate, w_ffn_up, w_ffn_down):
    w = w_in[l]
    gate_pad = ((0, 0), (0, LANES - 2 * HEADS))
    w_gates = jnp.concatenate([
        jnp.pad(w[:, GATE_COL:GATE_COL + 2 * HEADS], gate_pad),
        jnp.pad(w[:, GATE_COL + 2 * HEADS:GATE_COL + 4 * HEADS], gate_pad)], axis=1).astype(BF16)
    w_state = jnp.concatenate([
        w[:, :2 * MLSTM_WIDTH].astype(BF16), w_gates,
        w[:, GATE_COL + 4 * HEADS:STATE_DIM].astype(BF16)], axis=1)
    bg = b_gates[l]
    b_pad = (0, LANES - 2 * HEADS)
    bg = jnp.concatenate([jnp.pad(bg[:2 * HEADS], b_pad), jnp.pad(bg[2 * HEADS:], b_pad)])
    return dict(
        w_state=w_state,
        w_v_t=w[:, 2 * MLSTM_WIDTH:GATE_COL].T.astype(BF16),
        w_gates_t=w_gates.T,
        b_gates=bg.reshape(1, 2 * LANES),
        b_gates_t=jnp.broadcast_to(bg[:, None], (2 * LANES, TOKEN_TILE)),
        w_conv=w_conv_qk[l].reshape(2 * CONV_K, MLSTM_WIDTH),
        w_out_side=w[:, STATE_DIM:].astype(BF16),
        g_mh=g_mh[l].reshape(1, -1),
        g_sgu=g_sgu[l].reshape(1, -1),
        w_sgu=w_sgu[l].astype(BF16),
        b_sgu_t=b_sgu[l].T,
        w_glu=w_glu[l].astype(BF16),
        b_glu=b_glu[l].reshape(1, -1),
        w_sconv=w_sconv[l],
        w_up_mlstm=w_up_mlstm[l].astype(BF16),
        w_up_sgu=w_up_sgu[l].astype(BF16),
        w_up_s5=w_up_s5[l].astype(BF16),
        w_up_sconv=w_up_sconv[l].astype(BF16),
        w_out=w_out[l].astype(BF16),
        w_ffn_gate=w_ffn_gate[l].astype(BF16),
        w_ffn_up=w_ffn_up[l].astype(BF16),
        w_ffn_down=w_ffn_down[l].astype(BF16),
    )


def kernel(x, c, ctx, c_ctx, w_mod, b_mod, g_norm_mix, g_norm_ffn, w_in, b_gates, w_conv_qk, g_mh, g_sgu, w_sgu, b_sgu, s5_a_re, s5_a_im, s5_log_dt, s5_b_re, s5_b_im, s5_c_re, s5_c_im, s5_d, w_glu, b_glu, w_sconv, w_up_mlstm, w_up_sgu, w_up_s5, w_up_sconv, w_out, w_ffn_gate, w_ffn_up, w_ffn_down, g_final):
    batch, seq, d = x.shape
    ctx_len = ctx.shape[1]
    depth = w_mod.shape[0]
    assert d == D_MODEL and ctx_len == TOKEN_TILE and seq % TOKEN_TILE == 0 and batch == S5_BATCH
    ctx_tiles = ctx_len // TOKEN_TILE

    c_rows = jnp.zeros((8, d), F32).at[:batch].set(c).at[batch].set(c_ctx)
    mod_all = _modulation(c_rows, w_mod, b_mod).reshape(depth, 8, 6, d)
    xs = jnp.concatenate([ctx, x], axis=1)
    s5_ops_all = _s5_operators(s5_a_re, s5_a_im, s5_log_dt, s5_b_re, s5_b_im, s5_c_re, s5_c_im, s5_d)

    out = None
    for l in range(depth):
        last = l == depth - 1
        mod_x = mod_all[l, :batch]
        mod_c = jnp.broadcast_to(mod_all[l, batch][None], (batch, 6, d))
        mods = jnp.stack([mod_c, mod_x], axis=1)
        lw = _layer_weights(l, w_in, b_gates, w_conv_qk, g_mh, g_sgu, w_sgu, b_sgu, w_glu, b_glu,
                            w_sconv, w_up_mlstm, w_up_sgu, w_up_s5, w_up_sconv, w_out,
                            w_ffn_gate, w_ffn_up, w_ffn_down)
        s5_ops = tuple(op[l] for op in s5_ops_all)
        gain_mix = g_norm_mix[l].reshape(1, d)
        gain_ffn = g_norm_ffn[l].reshape(1, d)

        qt, k, vt, gates, gates_t, u_rows = _state_proj(xs, mods, gain_mix, lw, ctx_len)
        h_fwd, h_bwd = _mlstm(qt, k, vt, gates, gates_t, ctx_len)
        y_s5 = _s5_mixer(u_rows, s5_ops, (ctx_len + seq) // S5_CHUNK, ctx_len // S5_CHUNK)
        tile0 = ctx_tiles if last else 0
        x_mid = _merge(xs, mods, gain_mix, h_fwd, h_bwd, y_s5, lw, ctx_len, tile0)
        xs = _ffn(x_mid, mods, gain_ffn, lw['w_ffn_gate'], lw['w_ffn_up'], lw['w_ffn_down'],
                  g_final.reshape(1, d), tile0, last)
        out = xs
    return out
```

```python
import functools
import math

import jax
import jax.numpy as jnp
from jax import lax
from jax.experimental import pallas as pl
from jax.experimental.pallas import tpu as pltpu

F32 = jnp.float32
BF16 = jnp.bfloat16

D_MODEL = 1024
GRID_W = 64
N_BRANCH = 4
RMS_EPS = 1e-6
CONV_K = 3
HEADS = 4
HEAD_DIM = 128
MLSTM_WIDTH = HEADS * HEAD_DIM
MLSTM_CHUNK = 128
SGU_GROUPS = 4
SGU_GROUP_DIM = 128
SGU_WIDTH = SGU_GROUPS * SGU_GROUP_DIM
SGU_CHUNK = 128
S5_GROUP_DIM = 16
S5_GROUPS = 24
S5_WIDTH = S5_GROUPS * S5_GROUP_DIM
S5_STATE = 64
SCONV_WIDTH = 512
STATE_DIM = 3 * MLSTM_WIDTH + 4 * HEADS + S5_WIDTH
GATE_COL = 3 * MLSTM_WIDTH

LANES = 128
TOKEN_TILE = 256
S5_CHUNK = 16
S5_ROW = S5_CHUNK * S5_GROUP_DIM
S5_PAIR = 2 * S5_STATE
S5_SCAN_PAIRS = 2
S5_BATCH = 4
Z_GATE_COL = 2 * MLSTM_WIDTH
U_COL = Z_GATE_COL + 2 * LANES
STATE_COLS = U_COL + S5_WIDTH
VMEM_LIMIT = 52 * 1024 * 1024


def _resident(shape):
    zeros = (0,) * len(shape)
    return pl.BlockSpec(shape, lambda *_: zeros, pipeline_mode=pl.Buffered(1))


def _params(*semantics, flags=None):
    return pltpu.CompilerParams(dimension_semantics=semantics, vmem_limit_bytes=VMEM_LIMIT, flags=flags)


def _modulated_norm(x, gain, shift, scale):
    y = x * lax.rsqrt(jnp.mean(x * x, axis=-1, keepdims=True) + RMS_EPS) * gain
    return y * (1.0 + scale) + shift


def _split3(a):
    hi = a.astype(BF16)
    r1 = a - hi.astype(F32)
    mid = r1.astype(BF16)
    lo = (r1 - mid.astype(F32)).astype(BF16)
    return hi, mid, lo


def _conv3(a, w, first, last):
    n = a.shape[0]
    prev = jnp.where(first, 0.0, pltpu.roll(a, 1, 0))
    nxt = jnp.where(last, 0.0, pltpu.roll(a, n - 1, 0))
    return prev * w[0:1] + a * w[1:2] + nxt * w[2:3]


def _row_edges(tile_idx, n_rows, ctx_len):
    period = jnp.where(tile_idx == 0, ctx_len, GRID_W)
    pos = lax.broadcasted_iota(jnp.int32, (n_rows, 1), 0) & (period - 1)
    return pos == 0, pos == period - 1


def _mod_kernel(c_ref, w_ref, b_ref, o_ref):
    a = c_ref[...]
    a = a * jax.nn.sigmoid(a)
    a_hi = a.astype(BF16)
    a_lo = (a - a_hi.astype(F32)).astype(BF16)
    w = w_ref[0]
    w_hi = w.astype(BF16)
    w_lo = (w - w_hi.astype(F32)).astype(BF16)
    acc = jnp.dot(a_hi, w_hi, preferred_element_type=F32)
    acc = acc + jnp.dot(a_lo, w_hi, preferred_element_type=F32)
    acc = acc + jnp.dot(a_hi, w_lo, preferred_element_type=F32)
    o_ref[0] = acc + b_ref[0]


def _modulation(c_rows, w_mod, b_mod):
    depth, d, n = w_mod.shape
    tn = 1536
    return pl.pallas_call(
        _mod_kernel,
        out_shape=jax.ShapeDtypeStruct((depth, 8, n), F32),
        grid=(depth, n // tn),
        in_specs=[
            pl.BlockSpec((8, d), lambda l, j: (0, 0)),
            pl.BlockSpec((1, d, tn), lambda l, j: (l, 0, j)),
            pl.BlockSpec((1, 1, tn), lambda l, j: (l, 0, j)),
        ],
        out_specs=pl.BlockSpec((1, 8, tn), lambda l, j: (l, 0, j)),
        compiler_params=_params("arbitrary", "arbitrary"),
        name="modulation",
    )(c_rows, w_mod, b_mod.reshape(depth, 1, n))


def _to_group_rows(z, u_ref, buf_ref):
    n_chunks = z.shape[0] // S5_CHUNK
    per_block = LANES // S5_GROUP_DIM
    for gb in range(S5_WIDTH // LANES):
        buf_ref[gb] = z[:, gb * LANES:(gb + 1) * LANES]
    for s in range(S5_CHUNK):
        for gb in range(S5_WIDTH // LANES):
            zs = buf_ref[gb, pl.ds(s, n_chunks, stride=S5_CHUNK), :]
            for gl in range(per_block):
                u_ref[gb * per_block + gl, :, s * S5_GROUP_DIM:(s + 1) * S5_GROUP_DIM] = (
                    zs[:, gl * S5_GROUP_DIM:(gl + 1) * S5_GROUP_DIM])


def _from_group_rows(y_ref, buf_ref):
    n_chunks = y_ref.shape[1]
    per_block = LANES // S5_GROUP_DIM
    for s in range(S5_CHUNK):
        for gb in range(S5_WIDTH // LANES):
            pieces = [y_ref[gb * per_block + gl, :, s * S5_GROUP_DIM:(s + 1) * S5_GROUP_DIM]
                      for gl in range(per_block)]
            buf_ref[gb, pl.ds(s, n_chunks, stride=S5_CHUNK), :] = jnp.concatenate(pieces, axis=-1)
    return jnp.concatenate([buf_ref[gb] for gb in range(S5_WIDTH // LANES)], axis=-1)


def _state_proj_kernel(x_ref, mod_ref, gain_ref, w_ref, wvt_ref, wgt_ref, bg_ref, bgt_ref, wc_ref, eye_ref,
                       qt_ref, k_ref, vt_ref, g_ref, gt_ref, u_ref, ubuf_ref, *, ctx_len):
    t = pl.program_id(1)
    mod = mod_ref[0, 0]
    h = _modulated_norm(x_ref[0], gain_ref[...], mod[0:1], mod[1:2]).astype(BF16)
    z = jnp.dot(h, w_ref[...], preferred_element_type=F32)
    contract_last = (((1,), (1,)), ((), ()))
    vt_ref[0] = lax.dot_general(wvt_ref[...], h, contract_last, preferred_element_type=F32).astype(BF16)
    gt_ref[0] = lax.dot_general(wgt_ref[...], h, contract_last, preferred_element_type=F32) + bgt_ref[...]
    first, last = _row_edges(t, z.shape[0], ctx_len)
    wc = wc_ref[...]
    w = MLSTM_WIDTH
    q = _conv3(z[:, 0:w], wc[0:3], first, last)
    k = _conv3(z[:, w:2 * w], wc[3:6], first, last)
    q = (q * jax.nn.sigmoid(q)).astype(BF16)
    qt_ref[0] = lax.dot_general(eye_ref[...], q, contract_last, preferred_element_type=F32).astype(BF16)
    k_ref[0] = ((k * jax.nn.sigmoid(k)) * (HEAD_DIM ** -0.5)).astype(BF16)
    g_ref[0] = z[:, Z_GATE_COL:Z_GATE_COL + 2 * LANES] + bg_ref[...]
    _to_group_rows(z[:, U_COL:U_COL + S5_WIDTH], u_ref, ubuf_ref)


def _state_proj(xs, mods, gain, lw, ctx_len):
    b, s, d = xs.shape
    tm = TOKEN_TILE
    tiles = s // tm
    tok = lambda width: pl.BlockSpec((1, tm, width), lambda i, t: (i, t, 0))
    chan = lambda height: pl.BlockSpec((1, height, tm), lambda i, t: (i, 0, t))
    weights = (lw['w_state'], lw['w_v_t'], lw['w_gates_t'], lw['b_gates'], lw['b_gates_t'], lw['w_conv'],
               jnp.eye(MLSTM_WIDTH, dtype=BF16))
    return pl.pallas_call(
        functools.partial(_state_proj_kernel, ctx_len=ctx_len),
        out_shape=(
            jax.ShapeDtypeStruct((b, MLSTM_WIDTH, s), BF16),
            jax.ShapeDtypeStruct((b, s, MLSTM_WIDTH), BF16),
            jax.ShapeDtypeStruct((b, MLSTM_WIDTH, s), BF16),
            jax.ShapeDtypeStruct((b, s, 2 * LANES), F32),
            jax.ShapeDtypeStruct((b, 2 * LANES, s), F32),
            jax.ShapeDtypeStruct((S5_GROUPS, b * s // S5_CHUNK, S5_ROW), F32),
        ),
        grid=(b, s // tm),
        in_specs=[
            tok(d),
            pl.BlockSpec((1, 1, 6, d), lambda i, t: (i, jnp.minimum(t, 1), 0, 0)),
            _resident((1, d)),
        ] + [_resident(w.shape) for w in weights],
        out_specs=(chan(MLSTM_WIDTH), tok(MLSTM_WIDTH), chan(MLSTM_WIDTH), tok(2 * LANES), chan(2 * LANES),
                   pl.BlockSpec((S5_GROUPS, tm // S5_CHUNK, S5_ROW), lambda i, t: (0, i * tiles + t, 0))),
        scratch_shapes=[pltpu.VMEM((S5_WIDTH // LANES, tm, LANES), F32)],
        compiler_params=_params("arbitrary", "arbitrary"),
        name="state_proj",
    )(xs, mods, gain, *weights)


def _mlstm_chunk(direction, qt_ref, k_ref, vt_ref, g_ref, gt_ref, h_ref, ct_ref, n_ref, m_ref):
    lc = MLSTM_CHUNK
    sign = 1 - 2 * direction
    row = lax.broadcasted_iota(jnp.int32, (lc, lc), 0)
    col = lax.broadcasted_iota(jnp.int32, (lc, lc), 1)
    tri = (col - row) * sign >= 0
    tri_b = jnp.where(tri, 1.0, 0.0).astype(BF16)
    tri_t = jnp.where((row - col) * sign >= 0, 1.0, 0.0).astype(BF16)

    g = g_ref[0]
    gt = gt_ref[0]
    lf = jax.nn.log_sigmoid(g)
    lf_t = jax.nn.log_sigmoid(gt)
    bcum = sum(jnp.dot(tri_t, part, preferred_element_type=F32) for part in _split3(lf))
    bcum_t = sum(jnp.dot(part, tri_b, preferred_element_type=F32) for part in _split3(lf_t))
    total = jnp.sum(lf, axis=0, keepdims=True)

    qt = qt_ref[0]
    k = k_ref[0]
    vt = vt_ref[0]
    heads = []
    for hd in range(HEADS):
        sl = slice(hd * HEAD_DIM, (hd + 1) * HEAD_DIM)
        heads.append(dict(
            sl=sl, qt=qt[sl, :], k=k[:, sl], vt=vt[sl, :], tri=tri,
            a_col=g[:, hd:hd + 1] - bcum[:, HEADS + hd:HEADS + hd + 1],
            b_row=bcum_t[HEADS + hd:HEADS + hd + 1, :],
            b_last=total[:, HEADS + hd:HEADS + hd + 1],
            m=m_ref[hd][:, 0:1], ct=ct_ref[hd][...], n=n_ref[hd][...],
            h_ref=h_ref, ct_ref=ct_ref[hd], n_ref=n_ref[hd], m_ref=m_ref[hd]))
    return heads


def _mlstm_scores(hd):
    hd['kq'] = jnp.dot(hd['k'], hd['qt'], preferred_element_type=F32)
    n_rows = jnp.broadcast_to(hd['n'], (8, HEAD_DIM)).astype(BF16)
    hd['qn'] = jnp.dot(n_rows, hd['qt'], preferred_element_type=F32)[0:1]


def _mlstm_weights(hd):
    lc = MLSTM_CHUNK
    a_bc = jnp.broadcast_to(hd['a_col'], (lc, lc))
    b_row, b_last, m = hd['b_row'], hd['b_last'], hd['m']
    m_new = b_last + jnp.maximum(m, jnp.max(a_bc, axis=0, keepdims=True)[:, 0:1])
    w_bc = jnp.exp(a_bc + (b_last - m_new))
    hd['decay'] = jnp.exp(b_last + m - m_new)
    hd['m_new'] = m_new

    log_w = jnp.where(hd['tri'], a_bc + b_row, -jnp.inf)
    log_inter = b_row + m
    m_t = jnp.maximum(log_inter, jnp.max(log_w, axis=0, keepdims=True))
    inter = jnp.exp(log_inter - m_t)
    s_t = hd['kq'] * jnp.exp(log_w - m_t)
    den = inter * hd['qn'] + jnp.sum(s_t, axis=0, keepdims=True)
    hd['r_den'] = 1.0 / jnp.maximum(jnp.abs(den), jnp.exp(-m_t))
    hd['rhs'] = jnp.concatenate([s_t.astype(BF16), (hd['qt'].astype(F32) * inter).astype(BF16)], axis=0)
    wk = w_bc * hd['k'].astype(F32)
    hd['wk'] = wk.astype(BF16)
    hd['wk_sum'] = jnp.sum(wk, axis=0, keepdims=True)


def _mlstm_outputs(hd):
    lhs = jnp.concatenate([hd['vt'], hd['ct'].astype(BF16)], axis=1)
    hd['h_ref'][0, :, hd['sl']] = (jnp.dot(lhs, hd['rhs'], preferred_element_type=F32) * hd['r_den']).T
    hd['ct_ref'][...] = hd['decay'] * hd['ct'] + jnp.dot(hd['vt'], hd['wk'], preferred_element_type=F32)
    hd['n_ref'][...] = hd['decay'] * hd['n'] + hd['wk_sum']
    hd['m_ref'][...] = jnp.broadcast_to(hd['m_new'], (1, LANES))


def _mlstm_kernel(qtf_ref, kf_ref, vtf_ref, gf_ref, gtf_ref, qtb_ref, kb_ref, vtb_ref, gb_ref, gtb_ref,
                  hf_ref, hb_ref, *state_refs):
    per = 2 * HEADS
    ct_refs, n_refs, m_refs = state_refs[:per], state_refs[per:2 * per], state_refs[2 * per:]

    @pl.when(pl.program_id(1) == 0)
    def _():
        for ref in state_refs:
            ref[...] = jnp.zeros_like(ref)

    heads = (_mlstm_chunk(0, qtf_ref, kf_ref, vtf_ref, gf_ref, gtf_ref, hf_ref,
                          ct_refs[:HEADS], n_refs[:HEADS], m_refs[:HEADS])
             + _mlstm_chunk(1, qtb_ref, kb_ref, vtb_ref, gb_ref, gtb_ref, hb_ref,
                            ct_refs[HEADS:], n_refs[HEADS:], m_refs[HEADS:]))
    for phase in (_mlstm_scores, _mlstm_weights, _mlstm_outputs):
        for hd in heads:
            phase(hd)


def _mlstm(qt, k, vt, gates, gates_t, ctx_len):
    b, s, w = k.shape
    lc = MLSTM_CHUNK
    n_chunks = s // lc
    n_ctx = ctx_len // lc

    def back(j):
        return jnp.where(j < n_ctx, n_ctx - 1 - j, n_chunks + n_ctx - 1 - j)

    def specs(direction, chunk):
        return [pl.BlockSpec((1, w, lc), lambda i, j: (i, 0, chunk(j))),
                pl.BlockSpec((1, lc, w), lambda i, j: (i, chunk(j), 0)),
                pl.BlockSpec((1, w, lc), lambda i, j: (i, 0, chunk(j))),
                pl.BlockSpec((1, lc, LANES), lambda i, j: (i, chunk(j), direction)),
                pl.BlockSpec((1, LANES, lc), lambda i, j: (i, direction, chunk(j)))]

    fwd = lambda j: j
    out = jax.ShapeDtypeStruct((b, s, w), F32)
    return pl.pallas_call(
        _mlstm_kernel,
        out_shape=(out, out),
        grid=(b, n_chunks),
        in_specs=specs(0, fwd) + specs(1, back),
        out_specs=(pl.BlockSpec((1, lc, w), lambda i, j: (i, j, 0)),
                   pl.BlockSpec((1, lc, w), lambda i, j: (i, back(j), 0))),
        scratch_shapes=([pltpu.VMEM((HEAD_DIM, HEAD_DIM), F32)] * (2 * HEADS)
                        + [pltpu.VMEM((1, HEAD_DIM), F32)] * (2 * HEADS)
                        + [pltpu.VMEM((1, LANES), F32)] * (2 * HEADS)),
        compiler_params=_params("arbitrary", "arbitrary"),
        name="mlstm",
    )(qt, k, vt, gates, gates_t, qt, k, vt, gates, gates_t)


def _s5_operators(a_re, a_im, log_dt, b_re, b_im, c_re, c_im, d_skip):
    hp = lax.Precision.HIGHEST
    lch = S5_CHUNK
    depth = a_re.shape[0]
    a_re = a_re.astype(F32)
    a_im = a_im.astype(F32)
    dt = jnp.exp(log_dt.astype(F32))[..., None]
    la_re, la_im = dt * a_re, dt * a_im
    mag = jnp.exp(la_re)
    ab_re, ab_im = mag * jnp.cos(la_im), mag * jnp.sin(la_im)
    nr, ni = ab_re - 1.0, ab_im
    den = a_re * a_re + a_im * a_im
    f_re = ((nr * a_re + ni * a_im) / den)[:, :, :, None, :]
    f_im = ((ni * a_re - nr * a_im) / den)[:, :, :, None, :]
    bt_re = jnp.swapaxes(b_re.astype(F32), -1, -2)[:, None]
    bt_im = jnp.swapaxes(b_im.astype(F32), -1, -2)[:, None]
    bb_re = f_re * bt_re - f_im * bt_im
    bb_im = f_re * bt_im + f_im * bt_re
    c_re = c_re.astype(F32)
    c_im = c_im.astype(F32)

    n = jnp.arange(-(lch - 1), lch + 1, dtype=F32)[:, None]
    mg = jnp.exp(n * la_re[:, :, :, None, :])
    ang = n * la_im[:, :, :, None, :]
    pw_re, pw_im = mg * jnp.cos(ang), mg * jnp.sin(ang)
    zero = lch - 1

    def powers(direction, first, step):
        start = zero + first
        stop = start + step * lch
        sl = slice(start, stop if stop >= 0 else None, step)
        return pw_re[:, direction, :, sl, None, :], pw_im[:, direction, :, sl, None, :]

    def times(x_re, x_im, p):
        return x_re * p[0] - x_im * p[1], x_re * p[1] + x_im * p[0]

    def rows256(x):
        return x.reshape(depth, S5_GROUPS, S5_ROW, S5_STATE)

    bbf = (bb_re[:, 0, :, None], bb_im[:, 0, :, None])
    bbb = (bb_re[:, 1, :, None], bb_im[:, 1, :, None])
    cc = (c_re[:, :, None], c_im[:, :, None])

    def response(bb, direction, sign):
        l_re, l_im = times(*bb, powers(direction, 0, -sign))
        r_re, r_im = times(*cc, powers(direction, 0, sign))
        left = jnp.concatenate([rows256(l_re), rows256(l_im)], axis=-1)
        right = jnp.concatenate([rows256(r_re), -rows256(r_im)], axis=-1)
        return jnp.einsum('lgap,lgbp->lgab', left, right, precision=hp)

    src = (jnp.arange(S5_ROW) // S5_GROUP_DIM)[:, None]
    tgt = (jnp.arange(S5_ROW) // S5_GROUP_DIM)[None, :]
    d_rows = jnp.tile(d_skip.astype(F32).reshape(depth, S5_GROUPS, 1, S5_GROUP_DIM), (1, 1, 1, lch))
    toeplitz = (jnp.where(tgt >= src, response(bbf, 0, 1), 0.0)
                + jnp.where(src >= tgt, response(bbb, 1, -1), 0.0)
                + jnp.eye(S5_ROW, dtype=F32) * d_rows)

    def pair_block(x_re, x_im):
        x_re = x_re.reshape(depth, S5_GROUPS // 2, 2, S5_ROW, S5_STATE)
        x_im = x_im.reshape(depth, S5_GROUPS // 2, 2, S5_ROW, S5_STATE)
        z = jnp.zeros_like(x_re[:, :, 0])
        top = jnp.concatenate([x_re[:, :, 0], z, x_im[:, :, 0], z], axis=-1)
        bottom = jnp.concatenate([z, x_re[:, :, 1], z, x_im[:, :, 1]], axis=-1)
        return jnp.concatenate([top, bottom], axis=-2)

    def state_map(x, p):
        re, im = times(*x, p)
        return rows256(re), rows256(im)

    w_f = pair_block(*state_map(bbf, powers(0, lch - 1, -1)))
    w_b = pair_block(*state_map(bbb, powers(1, 0, 1)))
    w_pair = jnp.concatenate([w_f, w_b], axis=-1)
    vf_re, vf_im = state_map(cc, powers(0, 1, 1))
    vb_re, vb_im = state_map(cc, powers(1, lch, -1))
    vt_pair = jnp.stack([pair_block(vf_re, -vf_im), pair_block(vb_re, -vb_im)], axis=1)

    al_re, al_im = pw_re[:, :, :, zero + lch], pw_im[:, :, :, zero + lch]
    def rows(a):
        a = a.reshape(depth, 2, 1, S5_GROUPS * S5_STATE)
        return jnp.broadcast_to(a, (depth, 2, S5_BATCH, S5_GROUPS * S5_STATE)).reshape(depth, 2 * S5_BATCH, -1)
    a_step = jnp.stack([rows(al_re), rows(al_im)], axis=1)
    return toeplitz.astype(BF16), w_pair.astype(BF16), vt_pair.astype(BF16), a_step


def _s5_chunk_state_kernel(u_ref, w_ref, x_ref, *, n_chunks):
    u = jnp.concatenate([u_ref[0], u_ref[1]], axis=-1).astype(BF16)
    x = jnp.dot(u, w_ref[0], preferred_element_type=F32)
    for b in range(S5_BATCH):
        xb = x[b * n_chunks:(b + 1) * n_chunks]
        for direction in range(2):
            for part in range(2):
                lo = (2 * direction + part) * S5_PAIR
                x_ref[part, pl.ds(direction * S5_BATCH + b, n_chunks, stride=2 * S5_BATCH), :] = (
                    xb[:, lo:lo + S5_PAIR])


def _s5_chunk_states(u_rows, w_pair, n_chunks):
    g, rows, _ = u_rows.shape
    n_pairs = g // 2
    return pl.pallas_call(
        functools.partial(_s5_chunk_state_kernel, n_chunks=n_chunks),
        out_shape=jax.ShapeDtypeStruct((2 * n_pairs, n_chunks * 2 * S5_BATCH, S5_PAIR), F32),
        grid=(n_pairs,),
        in_specs=[pl.BlockSpec((2, rows, S5_ROW), lambda i: (i, 0, 0)),
                  pl.BlockSpec((1, 2 * S5_ROW, 4 * S5_PAIR), lambda i: (i, 0, 0))],
        out_specs=pl.BlockSpec((2, n_chunks * 2 * S5_BATCH, S5_PAIR), lambda i: (i, 0, 0)),
        compiler_params=_params("arbitrary"),
        name="s5_chunk_states",
    )(u_rows, w_pair)


def _s5_scan_kernel(x_ref, a_ref, of_ref, ob_ref, *, n_chunks, n_ctx):
    tile = 2 * S5_BATCH
    n_slabs = x_ref.shape[0]
    a_re = a_ref[0]
    a_im = a_ref[1]
    fwd_rows = lax.broadcasted_iota(jnp.int32, (tile, S5_PAIR), 0) < S5_BATCH

    def body(i, state):
        cb = jnp.where(i < n_ctx, n_ctx - 1 - i, n_chunks + n_ctx - 1 - i)
        rf = pl.multiple_of(i * tile, tile)
        rb = pl.multiple_of(cb * tile, tile)
        new_state = []
        for p in range(n_slabs // 2):
            re, im = state[2 * p], state[2 * p + 1]
            for j, val in ((2 * p, re), (2 * p + 1, im)):
                of_ref[j, pl.ds(rf, tile), :] = val
                ob_ref[j, pl.ds(rb, tile), :] = val
            in_re = jnp.where(fwd_rows, x_ref[2 * p, pl.ds(rf, tile), :], x_ref[2 * p, pl.ds(rb, tile), :])
            in_im = jnp.where(fwd_rows, x_ref[2 * p + 1, pl.ds(rf, tile), :],
                              x_ref[2 * p + 1, pl.ds(rb, tile), :])
            ar, ai = a_re[:, p * S5_PAIR:(p + 1) * S5_PAIR], a_im[:, p * S5_PAIR:(p + 1) * S5_PAIR]
            new_state.append(ar * re - ai * im + in_re)
            new_state.append(ar * im + ai * re + in_im)
        return tuple(new_state)

    zero = jnp.zeros((tile, S5_PAIR), F32)
    lax.fori_loop(0, n_chunks, body, (zero,) * n_slabs, unroll=4)


def _s5_scan(xc, a_step, n_chunks, n_ctx):
    slabs, rows, lanes = xc.shape
    per_step = 2 * S5_SCAN_PAIRS
    blk = pl.BlockSpec((per_step, rows, lanes), lambda i: (i, 0, 0))
    out = jax.ShapeDtypeStruct(xc.shape, F32)
    return pl.pallas_call(
        functools.partial(_s5_scan_kernel, n_chunks=n_chunks, n_ctx=n_ctx),
        out_shape=(out, out),
        grid=(slabs // per_step,),
        in_specs=[blk, pl.BlockSpec((2, 2 * S5_BATCH, S5_SCAN_PAIRS * S5_PAIR), lambda i: (0, 0, i))],
        out_specs=(blk, blk),
        compiler_params=_params("arbitrary"),
        name="s5_scan",
    )(xc, a_step)


def _s5_output_kernel(u_ref, t_ref, xf_ref, xb_ref, v_ref, y_ref, *, n_chunks):
    stride = 2 * S5_BATCH

    def batch_rows(ref, first_row):
        return jnp.concatenate(
            [jnp.concatenate([ref[part, pl.ds(first_row + b, n_chunks, stride=stride), :]
                              for part in range(2)], axis=-1) for b in range(S5_BATCH)], axis=0)

    xf = batch_rows(xf_ref, 0)
    xb = batch_rows(xb_ref, S5_BATCH)
    last = (((1,), (1,)), ((), ()))
    y = lax.dot_general(xf.astype(BF16), v_ref[0, 0], last, preferred_element_type=F32)
    y = y + lax.dot_general(xb.astype(BF16), v_ref[1, 0], last, preferred_element_type=F32)
    for a in range(2):
        within = jnp.dot(u_ref[a].astype(BF16), t_ref[a], preferred_element_type=F32)
        y_ref[a] = within + y[:, a * S5_ROW:(a + 1) * S5_ROW]


def _s5_output(u_rows, t_pair, xin_f, xin_b, v_pair, n_chunks):
    g, rows, _ = u_rows.shape
    n_pairs = g // 2
    col = pl.BlockSpec((2,) + xin_f.shape[1:], lambda i: (i, 0, 0))
    return pl.pallas_call(
        functools.partial(_s5_output_kernel, n_chunks=n_chunks),
        out_shape=jax.ShapeDtypeStruct((g, rows, S5_ROW), F32),
        grid=(n_pairs,),
        in_specs=[pl.BlockSpec((2, rows, S5_ROW), lambda i: (i, 0, 0)),
                  pl.BlockSpec((2, S5_ROW, S5_ROW), lambda i: (i, 0, 0)),
                  col, col,
                  pl.BlockSpec((2, 1, 2 * S5_ROW, 2 * S5_PAIR), lambda i: (0, i, 0, 0))],
        out_specs=pl.BlockSpec((2, rows, S5_ROW), lambda i: (i, 0, 0)),
        compiler_params=_params("arbitrary"),
        name="s5_output",
    )(u_rows, t_pair, xin_f, xin_b, v_pair)


def _s5_mixer(u_rows, ops, n_chunks, n_ctx):
    toeplitz, w_pair, vt_pair, a_step = ops
    xc = _s5_chunk_states(u_rows, w_pair, n_chunks)
    xin_f, xin_b = _s5_scan(xc, a_step, n_chunks, n_ctx)
    return _s5_output(u_rows, toeplitz, xin_f, xin_b, vt_pair, n_chunks)


def _merge_kernel(x_ref, mod_ref, gain_ref, hf_ref, hb_ref, ys_ref, wo_ref, gmh_ref, gsgu_ref,
                  wsgu_ref, bsgu_ref, wglu_ref, bglu_ref, wsc_ref, wum_ref, wug_ref, wus_ref,
                  wuc_ref, wout_ref, o_ref, ybuf_ref, *, ctx_len, tile0):
    t = pl.program_id(1) + tile0
    x = x_ref[0]
    mod = mod_ref[0, 0]
    h = _modulated_norm(x, gain_ref[...], mod[0:1], mod[1:2]).astype(BF16)
    tm = x.shape[0]

    def proj(start, width):
        return jnp.dot(h, wo_ref[:, start:start + width], preferred_element_type=F32)

    w = MLSTM_WIDTH
    z_o, z_su, z_sv, z_cb, z_cc, z_cx = (proj(i * w, w) for i in range(6))
    ys = jax.nn.gelu(_from_group_rows(ys_ref, ybuf_ref))
    z_glu = jnp.dot(ys.astype(BF16), wglu_ref[...], preferred_element_type=F32)
    gates = [proj(6 * w + j * D_MODEL, D_MODEL) for j in range(N_BRANCH)]

    hm = hf_ref[0] + hb_ref[0]
    parts = []
    for hd in range(HEADS):
        hh = hm[:, hd * HEAD_DIM:(hd + 1) * HEAD_DIM]
        parts.append(hh * lax.rsqrt(jnp.mean(hh * hh, axis=-1, keepdims=True) + RMS_EPS))
    y_a = (jnp.concatenate(parts, axis=-1) * gmh_ref[...]) * jax.nn.sigmoid(z_o)

    su = jax.nn.gelu(z_su)
    sv = jax.nn.gelu(z_sv)
    mu = jnp.mean(sv, axis=-1, keepdims=True)
    cen = sv - mu
    var = jnp.mean(cen * cen, axis=-1, keepdims=True)
    vn = (cen * lax.rsqrt(var + RMS_EPS) * gsgu_ref[...]).astype(BF16)
    bias = bsgu_ref[...]
    rows = []
    for n in range(tm // SGU_CHUNK):
        cols = []
        for gi in range(SGU_GROUPS):
            blk = vn[n * SGU_CHUNK:(n + 1) * SGU_CHUNK, gi * SGU_GROUP_DIM:(gi + 1) * SGU_GROUP_DIM]
            cols.append(jnp.dot(wsgu_ref[gi], blk, preferred_element_type=F32) + bias[:, gi:gi + 1])
        rows.append(jnp.concatenate(cols, axis=-1))
    y_b = su * jnp.concatenate(rows, axis=0)

    y_c = ys * jax.nn.sigmoid(z_glu + bglu_ref[...])

    first, last = _row_edges(t, tm, ctx_len)
    y_d = z_cb * _conv3(z_cc * z_cx, wsc_ref[...], first, last)

    acc = None
    for y, up_ref, gate in zip((y_a, y_b, y_c, y_d), (wum_ref, wug_ref, wus_ref, wuc_ref), gates):
        term = jax.nn.sigmoid(gate) * jnp.dot(y.astype(BF16), up_ref[...], preferred_element_type=F32)
        acc = term if acc is None else acc + term
    o_ref[0] = x + mod[2:3] * jnp.dot(acc.astype(BF16), wout_ref[...], preferred_element_type=F32)


def _merge(xs, mods, gain, h_fwd, h_bwd, y_s5, lw, ctx_len, tile0):
    b, s, d = xs.shape
    tm = TOKEN_TILE
    tiles = s // tm
    tok = lambda width: pl.BlockSpec((1, tm, width), lambda i, t: (i, t + tile0, 0))
    group_rows = pl.BlockSpec((S5_GROUPS, tm // S5_CHUNK, S5_ROW),
                              lambda i, t: (0, i * tiles + t + tile0, 0))
    weights = (lw['w_out_side'], lw['g_mh'], lw['g_sgu'], lw['w_sgu'], lw['b_sgu_t'], lw['w_glu'],
               lw['b_glu'], lw['w_sconv'], lw['w_up_mlstm'], lw['w_up_sgu'], lw['w_up_s5'],
               lw['w_up_sconv'], lw['w_out'])
    return pl.pallas_call(
        functools.partial(_merge_kernel, ctx_len=ctx_len, tile0=tile0),
        out_shape=jax.ShapeDtypeStruct((b, s - tile0 * tm, d), F32),
        grid=(b, tiles - tile0),
        in_specs=[tok(d),
                  pl.BlockSpec((1, 1, 6, d), lambda i, t: (i, jnp.minimum(t + tile0, 1), 0, 0)),
                  _resident((1, d)), tok(MLSTM_WIDTH), tok(MLSTM_WIDTH), group_rows]
                 + [_resident(w.shape) for w in weights],
        out_specs=pl.BlockSpec((1, tm, d), lambda i, t: (i, t, 0)),
        scratch_shapes=[pltpu.VMEM((S5_WIDTH // LANES, tm, LANES), F32)],
        compiler_params=_params("arbitrary", "arbitrary"),
        name="merge",
    )(xs, mods, gain, h_fwd, h_bwd, y_s5, *weights)


def _ffn_kernel(x_ref, mod_ref, gain_ref, wg_ref, wu_ref, wd_ref, gf_ref, o_ref, *, final):
    x = x_ref[0]
    mod = mod_ref[0, 0]
    h = _modulated_norm(x, gain_ref[...], mod[3:4], mod[4:5]).astype(BF16)
    a = jnp.dot(h, wg_ref[...], preferred_element_type=F32)
    b = jnp.dot(h, wu_ref[...], preferred_element_type=F32)
    hid = ((a * jax.nn.sigmoid(a)) * b).astype(BF16)
    y = x + mod[5:6] * jnp.dot(hid, wd_ref[...], preferred_element_type=F32)
    if final:
        y = y * lax.rsqrt(jnp.mean(y * y, axis=-1, keepdims=True) + RMS_EPS) * gf_ref[...]
    o_ref[0] = y


def _ffn(xs, mods, gain, w_gate, w_up, w_down, g_final, tile0, final):
    b, s, d = xs.shape
    tm = TOKEN_TILE
    return pl.pallas_call(
        functools.partial(_ffn_kernel, final=final),
        out_shape=jax.ShapeDtypeStruct((b, s, d), F32),
        grid=(b, s // tm),
        in_specs=[pl.BlockSpec((1, tm, d), lambda i, t: (i, t, 0)),
                  pl.BlockSpec((1, 1, 6, d), lambda i, t: (i, jnp.minimum(t + tile0, 1), 0, 0)),
                  _resident((1, d)), _resident(w_gate.shape), _resident(w_up.shape),
                  _resident(w_down.shape), _resident((1, d))],
        out_specs=pl.BlockSpec((1, tm, d), lambda i, t: (i, t, 0)),
        compiler_params=_params("arbitrary", "arbitrary"),
        name="ffn_final" if final else "ffn",
    )(xs, mods, gain, w_gate, w_up, w_down, g_final)


def _layer_weights(l, w_in, b_gates, w_conv_qk, g_mh, g_sgu, w_sgu, b_sgu, w_glu, b_glu, w_sconv,
                   w_up_mlstm, w_up_sgu, w_up_s5, w_up_sconv, w_out, w_ffn_gate, w_ffn_up, w_ffn_down):
    w = w_in[l]
    gate_pad = ((0, 0), (0, LANES - 2 * HEADS))
    w_gates = jnp.concatenate([
        jnp.pad(w[:, GATE_COL:GATE_COL + 2 * HEADS], gate_pad),
        jnp.pad(w[:, GATE_COL + 2 * HEADS:GATE_COL + 4 * HEADS], gate_pad)], axis=1).astype(BF16)
    w_state = jnp.concatenate([
        w[:, :2 * MLSTM_WIDTH].astype(BF16), w_gates,
        w[:, GATE_COL + 4 * HEADS:STATE_DIM].astype(BF16)], axis=1)
    bg = b_gates[l]
    b_pad = (0, LANES - 2 * HEADS)
    bg = jnp.concatenate([jnp.pad(bg[:2 * HEADS], b_pad), jnp.pad(bg[2 * HEADS:], b_pad)])
    return dict(
        w_state=w_state,
        w_v_t=w[:, 2 * MLSTM_WIDTH:GATE_COL].T.astype(BF16),
        w_gates_t=w_gates.T,
        b_gates=bg.reshape(1, 2 * LANES),
        b_gates_t=jnp.broadcast_to(bg[:, None], (2 * LANES, TOKEN_TILE)),
        w_conv=w_conv_qk[l].reshape(2 * CONV_K, MLSTM_WIDTH),
        w_out_side=w[:, STATE_DIM:].astype(BF16),
        g_mh=g_mh[l].reshape(1, -1),
        g_sgu=g_sgu[l].reshape(1, -1),
        w_sgu=w_sgu[l].astype(BF16),
        b_sgu_t=b_sgu[l].T,
        w_glu=w_glu[l].astype(BF16),
        b_glu=b_glu[l].reshape(1, -1),
        w_sconv=w_sconv[l],
        w_up_mlstm=w_up_mlstm[l].astype(BF16),
        w_up_sgu=w_up_sgu[l].astype(BF16),
        w_up_s5=w_up_s5[l].astype(BF16),
        w_up_sconv=w_up_sconv[l].astype(BF16),
        w_out=w_out[l].astype(BF16),
        w_ffn_gate=w_ffn_gate[l].astype(BF16),
        w_ffn_up=w_ffn_up[l].astype(BF16),
        w_ffn_down=w_ffn_down[l].astype(BF16),
    )


def kernel(x, c, ctx, c_ctx, w_mod, b_mod, g_norm_mix, g_norm_ffn, w_in, b_gates, w_conv_qk, g_mh, g_sgu, w_sgu, b_sgu, s5_a_re, s5_a_im, s5_log_dt, s5_b_re, s5_b_im, s5_c_re, s5_c_im, s5_d, w_glu, b_glu, w_sconv, w_up_mlstm, w_up_sgu, w_up_s5, w_up_sconv, w_out, w_ffn_gate, w_ffn_up, w_ffn_down, g_final):
    batch, seq, d = x.shape
    ctx_len = ctx.shape[1]
    depth = w_mod.shape[0]
    assert d == D_MODEL and ctx_len == TOKEN_TILE and seq % TOKEN_TILE == 0 and batch == S5_BATCH
    ctx_tiles = ctx_len // TOKEN_TILE

    c_rows = jnp.zeros((8, d), F32).at[:batch].set(c).at[batch].set(c_ctx)
    mod_all = _modulation(c_rows, w_mod, b_mod).reshape(depth, 8, 6, d)
    xs = jnp.concatenate([ctx, x], axis=1)
    s5_ops_all = _s5_operators(s5_a_re, s5_a_im, s5_log_dt, s5_b_re, s5_b_im, s5_c_re, s5_c_im, s5_d)

    out = None
    for l in range(depth):
        last = l == depth - 1
        mod_x = mod_all[l, :batch]
        mod_c = jnp.broadcast_to(mod_all[l, batch][None], (batch, 6, d))
        mods = jnp.stack([mod_c, mod_x], axis=1)
        lw = _layer_weights(l, w_in, b_gates, w_conv_qk, g_mh, g_sgu, w_sgu, b_sgu, w_glu, b_glu,
                            w_sconv, w_up_mlstm, w_up_sgu, w_up_s5, w_up_sconv, w_out,
                            w_ffn_gate, w_ffn_up, w_ffn_down)
        s5_ops = tuple(op[l] for op in s5_ops_all)
        gain_mix = g_norm_mix[l].reshape(1, d)
        gain_ffn = g_norm_ffn[l].reshape(1, d)

        qt, k, vt, gates, gates_t, u_rows = _state_proj(xs, mods, gain_mix, lw, ctx_len)
        h_fwd, h_bwd = _mlstm(qt, k, vt, gates, gates_t, ctx_len)
        y_s5 = _s5_mixer(u_rows, s5_ops, (ctx_len + seq) // S5_CHUNK, ctx_len // S5_CHUNK)
        tile0 = ctx_tiles if last else 0
        x_mid = _merge(xs, mods, gain_mix, h_fwd, h_bwd, y_s5, lw, ctx_len, tile0)
        xs = _ffn(x_mid, mods, gain_ffn, lw['w_ffn_gate'], lw['w_ffn_up'], lw['w_ffn_down'],
                  g_final.reshape(1, d), tile0, last)
        out = xs
    return out
```

```python
import functools
import math

import jax
import jax.numpy as jnp
from jax import lax
from jax.experimental import pallas as pl
from jax.experimental.pallas import tpu as pltpu

F32 = jnp.float32
BF16 = jnp.bfloat16

D_MODEL = 1024
GRID_W = 64
N_BRANCH = 4
RMS_EPS = 1e-6
CONV_K = 3
HEADS = 4
HEAD_DIM = 128
MLSTM_WIDTH = HEADS * HEAD_DIM
MLSTM_CHUNK = 128
MLSTM_BATCH = 2
SGU_GROUPS = 4
SGU_GROUP_DIM = 128
SGU_WIDTH = SGU_GROUPS * SGU_GROUP_DIM
SGU_CHUNK = 128
S5_GROUP_DIM = 16
S5_GROUPS = 24
S5_WIDTH = S5_GROUPS * S5_GROUP_DIM
S5_STATE = 64
SCONV_WIDTH = 512
STATE_DIM = 3 * MLSTM_WIDTH + 4 * HEADS + S5_WIDTH
GATE_COL = 3 * MLSTM_WIDTH

LANES = 128
TOKEN_TILE = 256
S5_CHUNK = 16
S5_ROW = S5_CHUNK * S5_GROUP_DIM
S5_PAIR = 2 * S5_STATE
S5_BATCH = 4
Z_GATE_COL = 2 * MLSTM_WIDTH
U_COL = Z_GATE_COL + 2 * LANES
STATE_COLS = U_COL + S5_WIDTH
VMEM_LIMIT = 52 * 1024 * 1024


def _resident(shape):
    zeros = (0,) * len(shape)
    return pl.BlockSpec(shape, lambda *_: zeros, pipeline_mode=pl.Buffered(1))


def _params(*semantics, flags=None):
    return pltpu.CompilerParams(dimension_semantics=semantics, vmem_limit_bytes=VMEM_LIMIT, flags=flags)


def _sigmoid(x):
    return 0.5 * jnp.tanh(0.5 * x) + 0.5


def _modulated_norm(x, gain, shift, scale):
    y = x * lax.rsqrt(jnp.mean(x * x, axis=-1, keepdims=True) + RMS_EPS) * gain
    return y * (1.0 + scale) + shift


def _split3(a):
    hi = a.astype(BF16)
    r1 = a - hi.astype(F32)
    mid = r1.astype(BF16)
    lo = (r1 - mid.astype(F32)).astype(BF16)
    return hi, mid, lo


def _conv3(a, w, first, last):
    n = a.shape[0]
    prev = jnp.where(first, 0.0, pltpu.roll(a, 1, 0))
    nxt = jnp.where(last, 0.0, pltpu.roll(a, n - 1, 0))
    return prev * w[0:1] + a * w[1:2] + nxt * w[2:3]


def _row_edges(tile_idx, n_rows, ctx_len):
    period = jnp.where(tile_idx == 0, ctx_len, GRID_W)
    pos = lax.broadcasted_iota(jnp.int32, (n_rows, 1), 0) & (period - 1)
    return pos == 0, pos == period - 1


def _mod_kernel(c_ref, w_ref, b_ref, o_ref):
    a = c_ref[...]
    a = a * _sigmoid(a)
    a_hi = a.astype(BF16)
    a_lo = (a - a_hi.astype(F32)).astype(BF16)
    w = w_ref[0]
    w_hi = w.astype(BF16)
    w_lo = (w - w_hi.astype(F32)).astype(BF16)
    acc = jnp.dot(a_hi, w_hi, preferred_element_type=F32)
    acc = acc + jnp.dot(a_lo, w_hi, preferred_element_type=F32)
    acc = acc + jnp.dot(a_hi, w_lo, preferred_element_type=F32)
    o_ref[0] = acc + b_ref[0]


def _modulation(c_rows, w_mod, b_mod):
    depth, d, n = w_mod.shape
    tn = 1536
    return pl.pallas_call(
        _mod_kernel,
        out_shape=jax.ShapeDtypeStruct((depth, 8, n), F32),
        grid=(depth, n // tn),
        in_specs=[
            pl.BlockSpec((8, d), lambda l, j: (0, 0)),
            pl.BlockSpec((1, d, tn), lambda l, j: (l, 0, j)),
            pl.BlockSpec((1, 1, tn), lambda l, j: (l, 0, j)),
        ],
        out_specs=pl.BlockSpec((1, 8, tn), lambda l, j: (l, 0, j)),
        compiler_params=_params("arbitrary", "arbitrary"),
        name="modulation",
    )(c_rows, w_mod, b_mod.reshape(depth, 1, n))


def _to_group_rows(z, u_ref, buf_ref):
    n_chunks = z.shape[0] // S5_CHUNK
    per_block = LANES // S5_GROUP_DIM
    for gb in range(S5_WIDTH // LANES):
        buf_ref[gb] = z[:, gb * LANES:(gb + 1) * LANES]
    for s in range(S5_CHUNK):
        for gb in range(S5_WIDTH // LANES):
            zs = buf_ref[gb, pl.ds(s, n_chunks, stride=S5_CHUNK), :]
            for gl in range(per_block):
                u_ref[gb * per_block + gl, :, s * S5_GROUP_DIM:(s + 1) * S5_GROUP_DIM] = (
                    zs[:, gl * S5_GROUP_DIM:(gl + 1) * S5_GROUP_DIM])


def _from_group_rows(y_ref, buf_ref):
    n_chunks = y_ref.shape[1]
    per_block = LANES // S5_GROUP_DIM
    for s in range(S5_CHUNK):
        for gb in range(S5_WIDTH // LANES):
            pieces = [y_ref[gb * per_block + gl, :, s * S5_GROUP_DIM:(s + 1) * S5_GROUP_DIM]
                      for gl in range(per_block)]
            buf_ref[gb, pl.ds(s, n_chunks, stride=S5_CHUNK), :] = jnp.concatenate(pieces, axis=-1)
    return jnp.concatenate([buf_ref[gb] for gb in range(S5_WIDTH // LANES)], axis=-1)


def _stream_specs(stream, tile0):
    ctx_src, lat_src, offset = stream
    d = ctx_src.shape[-1]
    specs = [pl.BlockSpec((1, TOKEN_TILE, d), lambda i, t: (i, 0, 0)),
             pl.BlockSpec((1, TOKEN_TILE, d), lambda i, t: (i, jnp.maximum(t + tile0 - offset, 0), 0))]
    return specs, (ctx_src, lat_src)


def _stream_tile(t, ctx_ref, lat_ref):
    return jnp.where(t == 0, ctx_ref[0], lat_ref[0])


def _state_proj_kernel(xc_ref, xl_ref, mod_ref, gain_ref, w_ref, wvt_ref, wgt_ref, bg_ref, bgt_ref, wc_ref,
                       eye_ref, qt_ref, k_ref, vt_ref, g_ref, gt_ref, u_ref, ubuf_ref, *, ctx_len):
    t = pl.program_id(1)
    mod = mod_ref[0, 0]
    h = _modulated_norm(_stream_tile(t, xc_ref, xl_ref), gain_ref[...], mod[0:1], mod[1:2]).astype(BF16)
    z = jnp.dot(h, w_ref[...], preferred_element_type=F32)
    contract_last = (((1,), (1,)), ((), ()))
    vt_ref[0] = lax.dot_general(wvt_ref[...], h, contract_last, preferred_element_type=F32).astype(BF16)
    gt_ref[0] = lax.dot_general(wgt_ref[...], h, contract_last, preferred_element_type=F32) + bgt_ref[...]
    g_ref[0] = z[:, Z_GATE_COL:Z_GATE_COL + 2 * LANES] + bg_ref[...]
    first, last = _row_edges(t, z.shape[0], ctx_len)
    wc = wc_ref[...]
    w = MLSTM_WIDTH
    q = _conv3(z[:, 0:w], wc[0:3], first, last)
    k = _conv3(z[:, w:2 * w], wc[3:6], first, last)
    q = (q * _sigmoid(q)).astype(BF16)
    qt_ref[0] = lax.dot_general(eye_ref[...], q, contract_last, preferred_element_type=F32).astype(BF16)
    k_ref[0] = ((k * _sigmoid(k)) * (HEAD_DIM ** -0.5)).astype(BF16)
    _to_group_rows(z[:, U_COL:U_COL + S5_WIDTH], u_ref, ubuf_ref)


def _state_proj(stream, s, mods, gain, lw, ctx_len):
    b, _, d = stream[0].shape
    tm = TOKEN_TILE
    tiles = s // tm
    stream_specs, stream_args = _stream_specs(stream, 0)
    tok = lambda width: pl.BlockSpec((1, tm, width), lambda i, t: (i, t, 0))
    chan = lambda height: pl.BlockSpec((1, height, tm), lambda i, t: (i, 0, t))
    weights = (lw['w_state'], lw['w_v_t'], lw['w_gates_t'], lw['b_gates'], lw['b_gates_t'], lw['w_conv'],
               jnp.eye(MLSTM_WIDTH, dtype=BF16))
    return pl.pallas_call(
        functools.partial(_state_proj_kernel, ctx_len=ctx_len),
        out_shape=(
            jax.ShapeDtypeStruct((b, MLSTM_WIDTH, s), BF16),
            jax.ShapeDtypeStruct((b, s, MLSTM_WIDTH), BF16),
            jax.ShapeDtypeStruct((b, MLSTM_WIDTH, s), BF16),
            jax.ShapeDtypeStruct((b, s, 2 * LANES), F32),
            jax.ShapeDtypeStruct((b, 2 * LANES, s), F32),
            jax.ShapeDtypeStruct((S5_GROUPS, b * s // S5_CHUNK, S5_ROW), F32),
        ),
        grid=(b, s // tm),
        in_specs=stream_specs + [
            pl.BlockSpec((1, 1, 6, d), lambda i, t: (i, jnp.minimum(t, 1), 0, 0)),
            _resident((1, d)),
        ] + [_resident(w.shape) for w in weights],
        out_specs=(chan(MLSTM_WIDTH), tok(MLSTM_WIDTH), chan(MLSTM_WIDTH), tok(2 * LANES), chan(2 * LANES),
                   pl.BlockSpec((S5_GROUPS, tm // S5_CHUNK, S5_ROW), lambda i, t: (0, i * tiles + t, 0))),
        scratch_shapes=[pltpu.VMEM((S5_WIDTH // LANES, tm, LANES), F32)],
        compiler_params=_params("arbitrary", "arbitrary"),
        name="state_proj",
    )(*stream_args, mods, gain, *weights)


def _mlstm_chunk(direction, bi, qt_ref, k_ref, vt_ref, g_ref, gt_ref, h_ref, ct_ref, n_ref, m_ref):
    lc = MLSTM_CHUNK
    sign = 1 - 2 * direction
    row = lax.broadcasted_iota(jnp.int32, (lc, lc), 0)
    col = lax.broadcasted_iota(jnp.int32, (lc, lc), 1)
    tri = (col - row) * sign >= 0
    tri_b = jnp.where(tri, 1.0, 0.0).astype(BF16)
    tri_t = jnp.where((row - col) * sign >= 0, 1.0, 0.0).astype(BF16)

    g = g_ref[bi]
    lf = jax.nn.log_sigmoid(g)
    lf_t = jax.nn.log_sigmoid(gt_ref[bi])
    bcum = sum(jnp.dot(tri_t, part, preferred_element_type=F32) for part in _split3(lf))
    bcum_t = sum(jnp.dot(part, tri_b, preferred_element_type=F32) for part in _split3(lf_t))
    total = jnp.sum(lf, axis=0, keepdims=True)

    qt = qt_ref[bi]
    k = k_ref[bi]
    vt = vt_ref[bi]
    heads = []
    for hd in range(HEADS):
        sl = slice(hd * HEAD_DIM, (hd + 1) * HEAD_DIM)
        heads.append(dict(
            bi=bi, sl=sl, qt=qt[sl, :], k=k[:, sl], vt=vt[sl, :], tri=tri,
            a_col=g[:, hd:hd + 1] - bcum[:, HEADS + hd:HEADS + hd + 1],
            b_row=bcum_t[HEADS + hd:HEADS + hd + 1, :],
            b_last=total[:, HEADS + hd:HEADS + hd + 1],
            m=m_ref[hd][:, 0:1], ct=ct_ref[hd][...], n=n_ref[hd][...],
            h_ref=h_ref, ct_ref=ct_ref[hd], n_ref=n_ref[hd], m_ref=m_ref[hd]))
    return heads


def _mlstm_scores(hd):
    hd['kq'] = jnp.dot(hd['k'], hd['qt'], preferred_element_type=F32)
    n_rows = jnp.broadcast_to(hd['n'], (8, HEAD_DIM)).astype(BF16)
    hd['qn'] = jnp.dot(n_rows, hd['qt'], preferred_element_type=F32)[0:1]


def _mlstm_weights(hd):
    lc = MLSTM_CHUNK
    a_bc = jnp.broadcast_to(hd['a_col'], (lc, lc))
    b_row, b_last, m = hd['b_row'], hd['b_last'], hd['m']
    m_new = b_last + jnp.maximum(m, jnp.max(a_bc, axis=0, keepdims=True)[:, 0:1])
    w_bc = jnp.exp(a_bc + (b_last - m_new))
    hd['decay'] = jnp.exp(b_last + m - m_new)
    hd['m_new'] = m_new

    log_w = jnp.where(hd['tri'], a_bc + b_row, -jnp.inf)
    log_inter = b_row + m
    m_t = jnp.maximum(log_inter, jnp.max(log_w, axis=0, keepdims=True))
    inter = jnp.exp(log_inter - m_t)
    s_t = hd['kq'] * jnp.exp(log_w - m_t)
    den = inter * hd['qn'] + jnp.sum(s_t, axis=0, keepdims=True)
    hd['r_den'] = 1.0 / jnp.maximum(jnp.abs(den), jnp.exp(-m_t))
    hd['rhs'] = jnp.concatenate([s_t.astype(BF16), (hd['qt'].astype(F32) * inter).astype(BF16)], axis=0)
    wk = w_bc * hd['k'].astype(F32)
    hd['wk'] = wk.astype(BF16)
    hd['wk_sum'] = jnp.sum(wk, axis=0, keepdims=True)


def _mlstm_outputs(hd):
    lhs = jnp.concatenate([hd['vt'], hd['ct'].astype(BF16)], axis=1)
    hd['h_ref'][hd['bi'], :, hd['sl']] = (
        jnp.dot(lhs, hd['rhs'], preferred_element_type=F32) * hd['r_den']).T
    hd['ct_ref'][...] = hd['decay'] * hd['ct'] + jnp.dot(hd['vt'], hd['wk'], preferred_element_type=F32)
    hd['n_ref'][...] = hd['decay'] * hd['n'] + hd['wk_sum']
    hd['m_ref'][...] = jnp.broadcast_to(hd['m_new'], (1, LANES))


def _mlstm_kernel(qtf_ref, kf_ref, vtf_ref, gf_ref, gtf_ref, qtb_ref, kb_ref, vtb_ref, gb_ref, gtb_ref,
                  hf_ref, hb_ref, *state_refs):
    per = MLSTM_BATCH * 2 * HEADS
    ct_refs, n_refs, m_refs = state_refs[:per], state_refs[per:2 * per], state_refs[2 * per:]

    @pl.when(pl.program_id(1) == 0)
    def _():
        for ref in state_refs:
            ref[...] = jnp.zeros_like(ref)

    heads = []
    for bi in range(MLSTM_BATCH):
        for direction, refs in enumerate(((qtf_ref, kf_ref, vtf_ref, gf_ref, gtf_ref, hf_ref),
                                          (qtb_ref, kb_ref, vtb_ref, gb_ref, gtb_ref, hb_ref))):
            own = slice((2 * bi + direction) * HEADS, (2 * bi + direction + 1) * HEADS)
            heads += _mlstm_chunk(direction, bi, *refs, ct_refs[own], n_refs[own], m_refs[own])
    for phase in (_mlstm_scores, _mlstm_weights, _mlstm_outputs):
        for hd in heads:
            phase(hd)


def _mlstm(qt, k, vt, gates, gates_t, ctx_len):
    b, s, w = k.shape
    lc = MLSTM_CHUNK
    n_chunks = s // lc
    n_ctx = ctx_len // lc

    def back(j):
        return jnp.where(j < n_ctx, n_ctx - 1 - j, n_chunks + n_ctx - 1 - j)

    nb = MLSTM_BATCH

    def specs(direction, chunk):
        return [pl.BlockSpec((nb, w, lc), lambda i, j: (i, 0, chunk(j))),
                pl.BlockSpec((nb, lc, w), lambda i, j: (i, chunk(j), 0)),
                pl.BlockSpec((nb, w, lc), lambda i, j: (i, 0, chunk(j))),
                pl.BlockSpec((nb, lc, LANES), lambda i, j: (i, chunk(j), direction)),
                pl.BlockSpec((nb, LANES, lc), lambda i, j: (i, direction, chunk(j)))]

    fwd = lambda j: j
    out = jax.ShapeDtypeStruct((b, s, w), F32)
    chains = nb * 2 * HEADS
    return pl.pallas_call(
        _mlstm_kernel,
        out_shape=(out, out),
        grid=(b // nb, n_chunks),
        in_specs=specs(0, fwd) + specs(1, back),
        out_specs=(pl.BlockSpec((nb, lc, w), lambda i, j: (i, j, 0)),
                   pl.BlockSpec((nb, lc, w), lambda i, j: (i, back(j), 0))),
        scratch_shapes=([pltpu.VMEM((HEAD_DIM, HEAD_DIM), F32)] * chains
                        + [pltpu.VMEM((1, HEAD_DIM), F32)] * chains
                        + [pltpu.VMEM((1, LANES), F32)] * chains),
        compiler_params=_params("arbitrary", "arbitrary"),
        name="mlstm",
    )(qt, k, vt, gates, gates_t, qt, k, vt, gates, gates_t)


def _s5_operators(a_re, a_im, log_dt, b_re, b_im, c_re, c_im, d_skip):
    hp = lax.Precision.HIGHEST
    lch = S5_CHUNK
    depth = a_re.shape[0]
    a_re = a_re.astype(F32)
    a_im = a_im.astype(F32)
    dt = jnp.exp(log_dt.astype(F32))[..., None]
    la_re, la_im = dt * a_re, dt * a_im
    mag = jnp.exp(la_re)
    ab_re, ab_im = mag * jnp.cos(la_im), mag * jnp.sin(la_im)
    nr, ni = ab_re - 1.0, ab_im
    den = a_re * a_re + a_im * a_im
    f_re = ((nr * a_re + ni * a_im) / den)[:, :, :, None, :]
    f_im = ((ni * a_re - nr * a_im) / den)[:, :, :, None, :]
    bt_re = jnp.swapaxes(b_re.astype(F32), -1, -2)[:, None]
    bt_im = jnp.swapaxes(b_im.astype(F32), -1, -2)[:, None]
    bb_re = f_re * bt_re - f_im * bt_im
    bb_im = f_re * bt_im + f_im * bt_re
    c_re = c_re.astype(F32)
    c_im = c_im.astype(F32)

    n = jnp.arange(-(lch - 1), lch + 1, dtype=F32)[:, None]
    mg = jnp.exp(n * la_re[:, :, :, None, :])
    ang = n * la_im[:, :, :, None, :]
    pw_re, pw_im = mg * jnp.cos(ang), mg * jnp.sin(ang)
    zero = lch - 1

    def powers(direction, first, step):
        start = zero + first
        stop = start + step * lch
        sl = slice(start, stop if stop >= 0 else None, step)
        return pw_re[:, direction, :, sl, None, :], pw_im[:, direction, :, sl, None, :]

    def times(x_re, x_im, p):
        return x_re * p[0] - x_im * p[1], x_re * p[1] + x_im * p[0]

    def rows256(x):
        return x.reshape(depth, S5_GROUPS, S5_ROW, S5_STATE)

    bbf = (bb_re[:, 0, :, None], bb_im[:, 0, :, None])
    bbb = (bb_re[:, 1, :, None], bb_im[:, 1, :, None])
    cc = (c_re[:, :, None], c_im[:, :, None])

    def response(bb, direction, sign):
        l_re, l_im = times(*bb, powers(direction, 0, -sign))
        r_re, r_im = times(*cc, powers(direction, 0, sign))
        left = jnp.concatenate([rows256(l_re), rows256(l_im)], axis=-1)
        right = jnp.concatenate([rows256(r_re), -rows256(r_im)], axis=-1)
        return jnp.einsum('lgap,lgbp->lgab', left, right, precision=hp)

    src = (jnp.arange(S5_ROW) // S5_GROUP_DIM)[:, None]
    tgt = (jnp.arange(S5_ROW) // S5_GROUP_DIM)[None, :]
    d_rows = jnp.tile(d_skip.astype(F32).reshape(depth, S5_GROUPS, 1, S5_GROUP_DIM), (1, 1, 1, lch))
    toeplitz = (jnp.where(tgt >= src, response(bbf, 0, 1), 0.0)
                + jnp.where(src >= tgt, response(bbb, 1, -1), 0.0)
                + jnp.eye(S5_ROW, dtype=F32) * d_rows)

    def pair_block(x_re, x_im):
        x_re = x_re.reshape(depth, S5_GROUPS // 2, 2, S5_ROW, S5_STATE)
        x_im = x_im.reshape(depth, S5_GROUPS // 2, 2, S5_ROW, S5_STATE)
        z = jnp.zeros_like(x_re[:, :, 0])
        top = jnp.concatenate([x_re[:, :, 0], z, x_im[:, :, 0], z], axis=-1)
        bottom = jnp.concatenate([z, x_re[:, :, 1], z, x_im[:, :, 1]], axis=-1)
        return jnp.concatenate([top, bottom], axis=-2)

    def state_map(x, p):
        re, im = times(*x, p)
        return rows256(re), rows256(im)

    w_f = pair_block(*state_map(bbf, powers(0, lch - 1, -1)))
    w_b = pair_block(*state_map(bbb, powers(1, 0, 1)))
    w_pair = jnp.concatenate([w_f, w_b], axis=-1)
    vf_re, vf_im = state_map(cc, powers(0, 1, 1))
    vb_re, vb_im = state_map(cc, powers(1, lch, -1))
    vt_pair = jnp.stack([pair_block(vf_re, -vf_im), pair_block(vb_re, -vb_im)], axis=1)

    al_re, al_im = pw_re[:, :, :, zero + lch], pw_im[:, :, :, zero + lch]
    def rows(a):
        a = a.reshape(depth, 2, 1, S5_GROUPS * S5_STATE)
        return jnp.broadcast_to(a, (depth, 2, S5_BATCH, S5_GROUPS * S5_STATE)).reshape(depth, 2 * S5_BATCH, -1)
    a_step = jnp.stack([rows(al_re), rows(al_im)], axis=1)
    return toeplitz.astype(BF16), w_pair.astype(BF16), vt_pair.astype(BF16), a_step


def _s5_chunk_state_kernel(u_ref, w_ref, x_ref, *, n_chunks):
    u = jnp.concatenate([u_ref[0], u_ref[1]], axis=-1).astype(BF16)
    x = jnp.dot(u, w_ref[0], preferred_element_type=F32)
    for b in range(S5_BATCH):
        xb = x[b * n_chunks:(b + 1) * n_chunks]
        for direction in range(2):
            for part in range(2):
                lo = (2 * direction + part) * S5_PAIR
                x_ref[part, pl.ds(direction * S5_BATCH + b, n_chunks, stride=2 * S5_BATCH), :] = (
                    xb[:, lo:lo + S5_PAIR])


def _s5_scan_kernel(x_ref, a_ref, of_ref, ob_ref, *, n_chunks, n_ctx):
    tile = 2 * S5_BATCH
    n_slabs = x_ref.shape[0]
    a_re = a_ref[0]
    a_im = a_ref[1]
    fwd_rows = lax.broadcasted_iota(jnp.int32, (tile, S5_PAIR), 0) < S5_BATCH

    def body(i, state):
        cb = jnp.where(i < n_ctx, n_ctx - 1 - i, n_chunks + n_ctx - 1 - i)
        rf = pl.multiple_of(i * tile, tile)
        rb = pl.multiple_of(cb * tile, tile)
        new_state = []
        for p in range(n_slabs // 2):
            re, im = state[2 * p], state[2 * p + 1]
            for j, val in ((2 * p, re), (2 * p + 1, im)):
                of_ref[j, pl.ds(rf, tile), :] = val
                ob_ref[j, pl.ds(rb, tile), :] = val
            in_re = jnp.where(fwd_rows, x_ref[2 * p, pl.ds(rf, tile), :], x_ref[2 * p, pl.ds(rb, tile), :])
            in_im = jnp.where(fwd_rows, x_ref[2 * p + 1, pl.ds(rf, tile), :],
                              x_ref[2 * p + 1, pl.ds(rb, tile), :])
            ar, ai = a_re[:, p * S5_PAIR:(p + 1) * S5_PAIR], a_im[:, p * S5_PAIR:(p + 1) * S5_PAIR]
            new_state.append(ar * re - ai * im + in_re)
            new_state.append(ar * im + ai * re + in_im)
        return tuple(new_state)

    zero = jnp.zeros((tile, S5_PAIR), F32)
    lax.fori_loop(0, n_chunks, body, (zero,) * n_slabs, unroll=4)


def _s5_output_kernel(u_ref, t_ref, xf_ref, xb_ref, v_ref, y_ref, *, n_chunks):
    stride = 2 * S5_BATCH

    def batch_rows(ref, first_row):
        return jnp.concatenate(
            [jnp.concatenate([ref[part, pl.ds(first_row + b, n_chunks, stride=stride), :]
                              for part in range(2)], axis=-1) for b in range(S5_BATCH)], axis=0)

    xf = batch_rows(xf_ref, 0)
    xb = batch_rows(xb_ref, S5_BATCH)
    last = (((1,), (1,)), ((), ()))
    y = lax.dot_general(xf.astype(BF16), v_ref[0, 0], last, preferred_element_type=F32)
    y = y + lax.dot_general(xb.astype(BF16), v_ref[1, 0], last, preferred_element_type=F32)
    for a in range(2):
        within = jnp.dot(u_ref[a].astype(BF16), t_ref[a], preferred_element_type=F32)
        y_ref[a] = within + y[:, a * S5_ROW:(a + 1) * S5_ROW]


def _s5_pair_kernel(u_ref, w_ref, t_ref, v_ref, a_ref, y_ref, xc_ref, xf_ref, xb_ref, *, n_chunks, n_ctx):
    _s5_chunk_state_kernel(u_ref, w_ref, xc_ref, n_chunks=n_chunks)
    _s5_scan_kernel(xc_ref, a_ref, xf_ref, xb_ref, n_chunks=n_chunks, n_ctx=n_ctx)
    _s5_output_kernel(u_ref, t_ref, xf_ref, xb_ref, v_ref, y_ref, n_chunks=n_chunks)


def _s5_mixer(u_rows, ops, n_chunks, n_ctx):
    toeplitz, w_pair, vt_pair, a_step = ops
    g, rows, _ = u_rows.shape
    n_pairs = g // 2
    states = pltpu.VMEM((2, n_chunks * 2 * S5_BATCH, S5_PAIR), F32)
    return pl.pallas_call(
        functools.partial(_s5_pair_kernel, n_chunks=n_chunks, n_ctx=n_ctx),
        out_shape=jax.ShapeDtypeStruct((g, rows, S5_ROW), F32),
        grid=(n_pairs,),
        in_specs=[pl.BlockSpec((2, rows, S5_ROW), lambda i: (i, 0, 0)),
                  pl.BlockSpec((1, 2 * S5_ROW, 4 * S5_PAIR), lambda i: (i, 0, 0)),
                  pl.BlockSpec((2, S5_ROW, S5_ROW), lambda i: (i, 0, 0)),
                  pl.BlockSpec((2, 1, 2 * S5_ROW, 2 * S5_PAIR), lambda i: (0, i, 0, 0)),
                  pl.BlockSpec((2, 2 * S5_BATCH, S5_PAIR), lambda i: (0, 0, i))],
        out_specs=pl.BlockSpec((2, rows, S5_ROW), lambda i: (i, 0, 0)),
        scratch_shapes=[states, states, states],
        compiler_params=_params("arbitrary"),
        name="s5_pair",
    )(u_rows, w_pair, toeplitz, vt_pair, a_step)


def _merge_kernel(xc_ref, xl_ref, mod_ref, gain_ref, hf_ref, hb_ref, ys_ref, wo_ref, gmh_ref, gsgu_ref,
                  wsgu_ref, bsgu_ref, wglu_ref, bglu_ref, wsc_ref, wum_ref, wug_ref, wus_ref,
                  wuc_ref, wout_ref, o_ref, ybuf_ref, *, ctx_len, tile0):
    t = pl.program_id(1) + tile0
    x = _stream_tile(t, xc_ref, xl_ref)
    mod = mod_ref[0, 0]
    h = _modulated_norm(x, gain_ref[...], mod[0:1], mod[1:2]).astype(BF16)
    tm = x.shape[0]

    def proj(start, width):
        return jnp.dot(h, wo_ref[:, start:start + width], preferred_element_type=F32)

    w = MLSTM_WIDTH
    z_o, z_su, z_sv, z_cb, z_cc, z_cx = (proj(i * w, w) for i in range(6))
    ys = jax.nn.gelu(_from_group_rows(ys_ref, ybuf_ref))
    z_glu = jnp.dot(ys.astype(BF16), wglu_ref[...], preferred_element_type=F32)
    gates = [proj(6 * w + j * D_MODEL, D_MODEL) for j in range(N_BRANCH)]

    hm = hf_ref[0] + hb_ref[0]
    parts = []
    for hd in range(HEADS):
        hh = hm[:, hd * HEAD_DIM:(hd + 1) * HEAD_DIM]
        parts.append(hh * lax.rsqrt(jnp.mean(hh * hh, axis=-1, keepdims=True) + RMS_EPS))
    y_a = (jnp.concatenate(parts, axis=-1) * gmh_ref[...]) * _sigmoid(z_o)

    su = jax.nn.gelu(z_su)
    sv = jax.nn.gelu(z_sv)
    mu = jnp.mean(sv, axis=-1, keepdims=True)
    cen = sv - mu
    var = jnp.mean(cen * cen, axis=-1, keepdims=True)
    vn = (cen * lax.rsqrt(var + RMS_EPS) * gsgu_ref[...]).astype(BF16)
    bias = bsgu_ref[...]
    rows = []
    for n in range(tm // SGU_CHUNK):
        cols = []
        for gi in range(SGU_GROUPS):
            blk = vn[n * SGU_CHUNK:(n + 1) * SGU_CHUNK, gi * SGU_GROUP_DIM:(gi + 1) * SGU_GROUP_DIM]
            cols.append(jnp.dot(wsgu_ref[gi], blk, preferred_element_type=F32) + bias[:, gi:gi + 1])
        rows.append(jnp.concatenate(cols, axis=-1))
    y_b = su * jnp.concatenate(rows, axis=0)

    y_c = ys * _sigmoid(z_glu + bglu_ref[...])

    first, last = _row_edges(t, tm, ctx_len)
    y_d = z_cb * _conv3(z_cc * z_cx, wsc_ref[...], first, last)

    acc = None
    for y, up_ref, gate in zip((y_a, y_b, y_c, y_d), (wum_ref, wug_ref, wus_ref, wuc_ref), gates):
        term = _sigmoid(gate) * jnp.dot(y.astype(BF16), up_ref[...], preferred_element_type=F32)
        acc = term if acc is None else acc + term
    o_ref[0] = x + mod[2:3] * jnp.dot(acc.astype(BF16), wout_ref[...], preferred_element_type=F32)


def _merge(stream, s, mods, gain, h_fwd, h_bwd, y_s5, lw, ctx_len, tile0):
    b, _, d = stream[0].shape
    tm = TOKEN_TILE
    tiles = s // tm
    stream_specs, stream_args = _stream_specs(stream, tile0)
    tok = lambda width: pl.BlockSpec((1, tm, width), lambda i, t: (i, t + tile0, 0))
    group_rows = pl.BlockSpec((S5_GROUPS, tm // S5_CHUNK, S5_ROW),
                              lambda i, t: (0, i * tiles + t + tile0, 0))
    weights = (lw['w_out_side'], lw['g_mh'], lw['g_sgu'], lw['w_sgu'], lw['b_sgu_t'], lw['w_glu'],
               lw['b_glu'], lw['w_sconv'], lw['w_up_mlstm'], lw['w_up_sgu'], lw['w_up_s5'],
               lw['w_up_sconv'], lw['w_out'])
    return pl.pallas_call(
        functools.partial(_merge_kernel, ctx_len=ctx_len, tile0=tile0),
        out_shape=jax.ShapeDtypeStruct((b, s - tile0 * tm, d), F32),
        grid=(b, tiles - tile0),
        in_specs=stream_specs
                 + [pl.BlockSpec((1, 1, 6, d), lambda i, t: (i, jnp.minimum(t + tile0, 1), 0, 0)),
                    _resident((1, d)), tok(MLSTM_WIDTH), tok(MLSTM_WIDTH), group_rows]
                 + [_resident(w.shape) for w in weights],
        out_specs=pl.BlockSpec((1, tm, d), lambda i, t: (i, t, 0)),
        scratch_shapes=[pltpu.VMEM((S5_WIDTH // LANES, tm, LANES), F32)],
        compiler_params=_params("arbitrary", "arbitrary"),
        name="merge",
    )(*stream_args, mods, gain, h_fwd, h_bwd, y_s5, *weights)


def _ffn_kernel(x_ref, mod_ref, gain_ref, wg_ref, wu_ref, wd_ref, gf_ref, o_ref, *, final):
    x = x_ref[0]
    mod = mod_ref[0, 0]
    h = _modulated_norm(x, gain_ref[...], mod[3:4], mod[4:5]).astype(BF16)
    a = jnp.dot(h, wg_ref[...], preferred_element_type=F32)
    b = jnp.dot(h, wu_ref[...], preferred_element_type=F32)
    hid = ((a * _sigmoid(a)) * b).astype(BF16)
    y = x + mod[5:6] * jnp.dot(hid, wd_ref[...], preferred_element_type=F32)
    if final:
        y = y * lax.rsqrt(jnp.mean(y * y, axis=-1, keepdims=True) + RMS_EPS) * gf_ref[...]
    o_ref[0] = y


def _ffn(xs, mods, gain, w_gate, w_up, w_down, g_final, tile0, final):
    b, s, d = xs.shape
    tm = TOKEN_TILE
    return pl.pallas_call(
        functools.partial(_ffn_kernel, final=final),
        out_shape=jax.ShapeDtypeStruct((b, s, d), F32),
        grid=(b, s // tm),
        in_specs=[pl.BlockSpec((1, tm, d), lambda i, t: (i, t, 0)),
                  pl.BlockSpec((1, 1, 6, d), lambda i, t: (i, jnp.minimum(t + tile0, 1), 0, 0)),
                  _resident((1, d)), _resident(w_gate.shape), _resident(w_up.shape),
                  _resident(w_down.shape), _resident((1, d))],
        out_specs=pl.BlockSpec((1, tm, d), lambda i, t: (i, t, 0)),
        compiler_params=_params("arbitrary", "arbitrary"),
        name="ffn_final" if final else "ffn",
    )(xs, mods, gain, w_gate, w_up, w_down, g_final)


def _layer_weights(l, w_in, b_gates, w_conv_qk, g_mh, g_sgu, w_sgu, b_sgu, w_glu, b_glu, w_sconv,
                   w_up_mlstm, w_up_sgu, w_up_s5, w_up_sconv, w_out, w_ffn_gate, w_ffn_up, w_ffn_down):
    w = w_in[l]
    gate_pad = ((0, 0), (0, LANES - 2 * HEADS))
    w_gates = jnp.concatenate([
        jnp.pad(w[:, GATE_COL:GATE_COL + 2 * HEADS], gate_pad),
        jnp.pad(w[:, GATE_COL + 2 * HEADS:GATE_COL + 4 * HEADS], gate_pad)], axis=1).astype(BF16)
    w_state = jnp.concatenate([
        w[:, :2 * MLSTM_WIDTH].astype(BF16), w_gates,
        w[:, GATE_COL + 4 * HEADS:STATE_DIM].astype(BF16)], axis=1)
    bg = b_gates[l]
    b_pad = (0, LANES - 2 * HEADS)
    bg = jnp.concatenate([jnp.pad(bg[:2 * HEADS], b_pad), jnp.pad(bg[2 * HEADS:], b_pad)])
    return dict(
        w_state=w_state,
        w_v_t=w[:, 2 * MLSTM_WIDTH:GATE_COL].T.astype(BF16),
        w_gates_t=w_gates.T,
        b_gates=bg.reshape(1, 2 * LANES),
        b_gates_t=jnp.broadcast_to(bg[:, None], (2 * LANES, TOKEN_TILE)),
        w_conv=w_conv_qk[l].reshape(2 * CONV_K, MLSTM_WIDTH),
        w_out_side=w[:, STATE_DIM:].astype(BF16),
        g_mh=g_mh[l].reshape(1, -1),
        g_sgu=g_sgu[l].reshape(1, -1),
        w_sgu=w_sgu[l].astype(BF16),
        b_sgu_t=b_sgu[l].T,
        w_glu=w_glu[l].astype(BF16),
        b_glu=b_glu[l].reshape(1, -1),
        w_sconv=w_sconv[l],
        w_up_mlstm=w_up_mlstm[l].astype(BF16),
        w_up_sgu=w_up_sgu[l].astype(BF16),
        w_up_s5=w_up_s5[l].astype(BF16),
        w_up_sconv=w_up_sconv[l].astype(BF16),
        w_out=w_out[l].astype(BF16),
        w_ffn_gate=w_ffn_gate[l].astype(BF16),
        w_ffn_up=w_ffn_up[l].astype(BF16),
        w_ffn_down=w_ffn_down[l].astype(BF16),
    )


def kernel(x, c, ctx, c_ctx, w_mod, b_mod, g_norm_mix, g_norm_ffn, w_in, b_gates, w_conv_qk, g_mh, g_sgu, w_sgu, b_sgu, s5_a_re, s5_a_im, s5_log_dt, s5_b_re, s5_b_im, s5_c_re, s5_c_im, s5_d, w_glu, b_glu, w_sconv, w_up_mlstm, w_up_sgu, w_up_s5, w_up_sconv, w_out, w_ffn_gate, w_ffn_up, w_ffn_down, g_final):
    batch, seq, d = x.shape
    ctx_len = ctx.shape[1]
    depth = w_mod.shape[0]
    assert d == D_MODEL and ctx_len == TOKEN_TILE and seq % TOKEN_TILE == 0 and batch == S5_BATCH
    ctx_tiles = ctx_len // TOKEN_TILE

    c_rows = jnp.zeros((8, d), F32).at[:batch].set(c).at[batch].set(c_ctx)
    mod_all = _modulation(c_rows, w_mod, b_mod).reshape(depth, 8, 6, d)
    stream = (ctx, x, ctx_tiles)
    stream_len = ctx_len + seq
    s5_ops_all = _s5_operators(s5_a_re, s5_a_im, s5_log_dt, s5_b_re, s5_b_im, s5_c_re, s5_c_im, s5_d)

    out = None
    for l in range(depth):
        last = l == depth - 1
        mod_x = mod_all[l, :batch]
        mod_c = jnp.broadcast_to(mod_all[l, batch][None], (batch, 6, d))
        mods = jnp.stack([mod_c, mod_x], axis=1)
        lw = _layer_weights(l, w_in, b_gates, w_conv_qk, g_mh, g_sgu, w_sgu, b_sgu, w_glu, b_glu,
                            w_sconv, w_up_mlstm, w_up_sgu, w_up_s5, w_up_sconv, w_out,
                            w_ffn_gate, w_ffn_up, w_ffn_down)
        s5_ops = tuple(op[l] for op in s5_ops_all)
        gain_mix = g_norm_mix[l].reshape(1, d)
        gain_ffn = g_norm_ffn[l].reshape(1, d)

        qt, k, vt, gates, gates_t, u_rows = _state_proj(stream, stream_len, mods, gain_mix, lw, ctx_len)
        h_fwd, h_bwd = _mlstm(qt, k, vt, gates, gates_t, ctx_len)
        y_s5 = _s5_mixer(u_rows, s5_ops, stream_len // S5_CHUNK, ctx_len // S5_CHUNK)
        tile0 = ctx_tiles if last else 0
        x_mid = _merge(stream, stream_len, mods, gain_mix, h_fwd, h_bwd, y_s5, lw, ctx_len, tile0)
        out = _ffn(x_mid, mods, gain_ffn, lw['w_ffn_gate'], lw['w_ffn_up'], lw['w_ffn_down'],
                   g_final.reshape(1, d), tile0, last)
        stream = (out, out, 0)
    return out
```

```python
import functools
import math

import jax
import jax.numpy as jnp
from jax import lax
from jax.experimental import pallas as pl
from jax.experimental.pallas import tpu as pltpu

F32 = jnp.float32
BF16 = jnp.bfloat16

D_MODEL = 1024
GRID_W = 64
N_BRANCH = 4
RMS_EPS = 1e-6
CONV_K = 3
HEADS = 4
HEAD_DIM = 128
MLSTM_WIDTH = HEADS * HEAD_DIM
MLSTM_CHUNK = 128
MLSTM_BATCH = 2
SGU_GROUPS = 4
SGU_GROUP_DIM = 128
SGU_WIDTH = SGU_GROUPS * SGU_GROUP_DIM
SGU_CHUNK = 128
S5_GROUP_DIM = 16
S5_GROUPS = 24
S5_WIDTH = S5_GROUPS * S5_GROUP_DIM
S5_STATE = 64
SCONV_WIDTH = 512
STATE_DIM = 3 * MLSTM_WIDTH + 4 * HEADS + S5_WIDTH
GATE_COL = 3 * MLSTM_WIDTH

LANES = 128
TOKEN_TILE = 256
S5_CHUNK = 16
S5_ROW = S5_CHUNK * S5_GROUP_DIM
S5_PAIR = 2 * S5_STATE
S5_BATCH = 4
Z_GATE_COL = 2 * MLSTM_WIDTH
U_COL = Z_GATE_COL + 2 * LANES
STATE_COLS = U_COL + S5_WIDTH
VMEM_LIMIT = 52 * 1024 * 1024


def _resident(shape):
    zeros = (0,) * len(shape)
    return pl.BlockSpec(shape, lambda *_: zeros, pipeline_mode=pl.Buffered(1))


def _layer_resident(stacked, layer):
    zeros = (0,) * (stacked.ndim - 1)
    return pl.BlockSpec((None,) + stacked.shape[1:], lambda *_: (layer,) + zeros,
                        pipeline_mode=pl.Buffered(1))


def _mod_spec(d, layer, tile0):
    return pl.BlockSpec((None, 1, 1, 6, d), lambda i, t: (layer, i, jnp.minimum(t + tile0, 1), 0, 0))


def _params(*semantics, flags=None):
    return pltpu.CompilerParams(dimension_semantics=semantics, vmem_limit_bytes=VMEM_LIMIT, flags=flags)


def _sigmoid(x):
    return 0.5 * jnp.tanh(0.5 * x) + 0.5


def _modulated_norm(x, gain, shift, scale):
    y = x * lax.rsqrt(jnp.mean(x * x, axis=-1, keepdims=True) + RMS_EPS) * gain
    return y * (1.0 + scale) + shift


def _split3(a):
    hi = a.astype(BF16)
    r1 = a - hi.astype(F32)
    mid = r1.astype(BF16)
    lo = (r1 - mid.astype(F32)).astype(BF16)
    return hi, mid, lo


def _conv3(a, w, first, last):
    n = a.shape[0]
    prev = jnp.where(first, 0.0, pltpu.roll(a, 1, 0))
    nxt = jnp.where(last, 0.0, pltpu.roll(a, n - 1, 0))
    return prev * w[0:1] + a * w[1:2] + nxt * w[2:3]


def _row_edges(tile_idx, n_rows, ctx_len):
    period = jnp.where(tile_idx == 0, ctx_len, GRID_W)
    pos = lax.broadcasted_iota(jnp.int32, (n_rows, 1), 0) & (period - 1)
    return pos == 0, pos == period - 1


def _mod_kernel(c_ref, w_ref, b_ref, o_ref):
    a = c_ref[...]
    a = a * _sigmoid(a)
    a_hi = a.astype(BF16)
    a_lo = (a - a_hi.astype(F32)).astype(BF16)
    w = w_ref[0]
    w_hi = w.astype(BF16)
    w_lo = (w - w_hi.astype(F32)).astype(BF16)
    acc = jnp.dot(a_hi, w_hi, preferred_element_type=F32)
    acc = acc + jnp.dot(a_lo, w_hi, preferred_element_type=F32)
    acc = acc + jnp.dot(a_hi, w_lo, preferred_element_type=F32)
    o_ref[0] = acc + b_ref[0]


def _modulation(c_rows, w_mod, b_mod):
    depth, d, n = w_mod.shape
    tn = 1536
    return pl.pallas_call(
        _mod_kernel,
        out_shape=jax.ShapeDtypeStruct((depth, 8, n), F32),
        grid=(depth, n // tn),
        in_specs=[
            pl.BlockSpec((8, d), lambda l, j: (0, 0)),
            pl.BlockSpec((1, d, tn), lambda l, j: (l, 0, j)),
            pl.BlockSpec((1, 1, tn), lambda l, j: (l, 0, j)),
        ],
        out_specs=pl.BlockSpec((1, 8, tn), lambda l, j: (l, 0, j)),
        compiler_params=_params("arbitrary", "arbitrary"),
        name="modulation",
    )(c_rows, w_mod, b_mod.reshape(depth, 1, n))


def _to_group_rows(z, u_ref, buf_ref):
    n_chunks = z.shape[0] // S5_CHUNK
    per_block = LANES // S5_GROUP_DIM
    for gb in range(S5_WIDTH // LANES):
        buf_ref[gb] = z[:, gb * LANES:(gb + 1) * LANES]
    for s in range(S5_CHUNK):
        for gb in range(S5_WIDTH // LANES):
            zs = buf_ref[gb, pl.ds(s, n_chunks, stride=S5_CHUNK), :]
            for gl in range(per_block):
                u_ref[gb * per_block + gl, :, s * S5_GROUP_DIM:(s + 1) * S5_GROUP_DIM] = (
                    zs[:, gl * S5_GROUP_DIM:(gl + 1) * S5_GROUP_DIM])


def _from_group_rows(y_ref, buf_ref):
    n_chunks = y_ref.shape[1]
    per_block = LANES // S5_GROUP_DIM
    for s in range(S5_CHUNK):
        for gb in range(S5_WIDTH // LANES):
            pieces = [y_ref[gb * per_block + gl, :, s * S5_GROUP_DIM:(s + 1) * S5_GROUP_DIM]
                      for gl in range(per_block)]
            buf_ref[gb, pl.ds(s, n_chunks, stride=S5_CHUNK), :] = jnp.concatenate(pieces, axis=-1)
    return jnp.concatenate([buf_ref[gb] for gb in range(S5_WIDTH // LANES)], axis=-1)


def _stream_specs(stream, tile0):
    ctx_src, lat_src, offset = stream
    d = ctx_src.shape[-1]
    specs = [pl.BlockSpec((1, TOKEN_TILE, d), lambda i, t: (i, 0, 0)),
             pl.BlockSpec((1, TOKEN_TILE, d), lambda i, t: (i, jnp.maximum(t + tile0 - offset, 0), 0))]
    return specs, (ctx_src, lat_src)


def _stream_tile(t, ctx_ref, lat_ref):
    return jnp.where(t == 0, ctx_ref[0], lat_ref[0])


def _state_proj_kernel(xc_ref, xl_ref, mod_ref, gain_ref, w_ref, wvt_ref, wgt_ref, bg_ref, bgt_ref, wc_ref,
                       eye_ref, qt_ref, k_ref, vt_ref, g_ref, gt_ref, u_ref, ubuf_ref, *, ctx_len):
    t = pl.program_id(1)
    mod = mod_ref[0, 0]
    h = _modulated_norm(_stream_tile(t, xc_ref, xl_ref), gain_ref[...], mod[0:1], mod[1:2]).astype(BF16)
    z = jnp.dot(h, w_ref[...], preferred_element_type=F32)
    contract_last = (((1,), (1,)), ((), ()))
    vt_ref[0] = lax.dot_general(wvt_ref[...], h, contract_last, preferred_element_type=F32).astype(BF16)
    gt_ref[0] = lax.dot_general(wgt_ref[...], h, contract_last, preferred_element_type=F32) + bgt_ref[...]
    g_ref[0] = z[:, Z_GATE_COL:Z_GATE_COL + 2 * LANES] + bg_ref[...]
    first, last = _row_edges(t, z.shape[0], ctx_len)
    wc = wc_ref[...]
    w = MLSTM_WIDTH
    q = _conv3(z[:, 0:w], wc[0:3], first, last)
    k = _conv3(z[:, w:2 * w], wc[3:6], first, last)
    q = (q * _sigmoid(q)).astype(BF16)
    qt_ref[0] = lax.dot_general(eye_ref[...], q, contract_last, preferred_element_type=F32).astype(BF16)
    k_ref[0] = ((k * _sigmoid(k)) * (HEAD_DIM ** -0.5)).astype(BF16)
    _to_group_rows(z[:, U_COL:U_COL + S5_WIDTH], u_ref, ubuf_ref)


def _state_proj(stream, s, mods, lw, layer, ctx_len):
    b, _, d = stream[0].shape
    tm = TOKEN_TILE
    tiles = s // tm
    stream_specs, stream_args = _stream_specs(stream, 0)
    tok = lambda width: pl.BlockSpec((1, tm, width), lambda i, t: (i, t, 0))
    chan = lambda height: pl.BlockSpec((1, height, tm), lambda i, t: (i, 0, t))
    weights = tuple(lw[name] for name in ('gain_mix', 'w_state', 'w_v_t', 'w_gates_t', 'b_gates',
                                           'b_gates_t', 'w_conv'))
    eye = jnp.eye(MLSTM_WIDTH, dtype=BF16)
    return pl.pallas_call(
        functools.partial(_state_proj_kernel, ctx_len=ctx_len),
        out_shape=(
            jax.ShapeDtypeStruct((b, MLSTM_WIDTH, s), BF16),
            jax.ShapeDtypeStruct((b, s, MLSTM_WIDTH), BF16),
            jax.ShapeDtypeStruct((b, MLSTM_WIDTH, s), BF16),
            jax.ShapeDtypeStruct((b, s, 2 * LANES), F32),
            jax.ShapeDtypeStruct((b, 2 * LANES, s), F32),
            jax.ShapeDtypeStruct((S5_GROUPS, b * s // S5_CHUNK, S5_ROW), F32),
        ),
        grid=(b, s // tm),
        in_specs=stream_specs + [_mod_spec(d, layer, 0)]
                 + [_layer_resident(w, layer) for w in weights] + [_resident(eye.shape)],
        out_specs=(chan(MLSTM_WIDTH), tok(MLSTM_WIDTH), chan(MLSTM_WIDTH), tok(2 * LANES), chan(2 * LANES),
                   pl.BlockSpec((S5_GROUPS, tm // S5_CHUNK, S5_ROW), lambda i, t: (0, i * tiles + t, 0))),
        scratch_shapes=[pltpu.VMEM((S5_WIDTH // LANES, tm, LANES), F32)],
        compiler_params=_params("arbitrary", "arbitrary"),
        name="state_proj",
    )(*stream_args, mods, *weights, eye)


def _mlstm_chunk(direction, bi, qt_ref, k_ref, vt_ref, g_ref, gt_ref, h_ref, ct_ref, n_ref, m_ref):
    lc = MLSTM_CHUNK
    sign = 1 - 2 * direction
    row = lax.broadcasted_iota(jnp.int32, (lc, lc), 0)
    col = lax.broadcasted_iota(jnp.int32, (lc, lc), 1)
    tri = (col - row) * sign >= 0
    tri_b = jnp.where(tri, 1.0, 0.0).astype(BF16)
    tri_t = jnp.where((row - col) * sign >= 0, 1.0, 0.0).astype(BF16)

    g = g_ref[bi]
    lf = jax.nn.log_sigmoid(g)
    lf_t = jax.nn.log_sigmoid(gt_ref[bi])
    bcum = sum(jnp.dot(tri_t, part, preferred_element_type=F32) for part in _split3(lf))
    bcum_t = sum(jnp.dot(part, tri_b, preferred_element_type=F32) for part in _split3(lf_t))
    total = jnp.sum(lf, axis=0, keepdims=True)

    qt = qt_ref[bi]
    k = k_ref[bi]
    vt = vt_ref[bi]
    heads = []
    for hd in range(HEADS):
        sl = slice(hd * HEAD_DIM, (hd + 1) * HEAD_DIM)
        heads.append(dict(
            bi=bi, sl=sl, qt=qt[sl, :], k=k[:, sl], vt=vt[sl, :], tri=tri,
            a_col=g[:, hd:hd + 1] - bcum[:, HEADS + hd:HEADS + hd + 1],
            b_row=bcum_t[HEADS + hd:HEADS + hd + 1, :],
            b_last=total[:, HEADS + hd:HEADS + hd + 1],
            m=m_ref[hd][:, 0:1], ct=ct_ref[hd][...], n=n_ref[hd][...],
            h_ref=h_ref, ct_ref=ct_ref[hd], n_ref=n_ref[hd], m_ref=m_ref[hd]))
    return heads


def _mlstm_scores(hd):
    hd['kq'] = jnp.dot(hd['k'], hd['qt'], preferred_element_type=F32)
    n_rows = jnp.broadcast_to(hd['n'], (8, HEAD_DIM)).astype(BF16)
    hd['qn'] = jnp.dot(n_rows, hd['qt'], preferred_element_type=F32)[0:1]


def _mlstm_weights(hd):
    lc = MLSTM_CHUNK
    a_bc = jnp.broadcast_to(hd['a_col'], (lc, lc))
    b_row, b_last, m = hd['b_row'], hd['b_last'], hd['m']
    m_new = b_last + jnp.maximum(m, jnp.max(a_bc, axis=0, keepdims=True)[:, 0:1])
    w_bc = jnp.exp(a_bc + (b_last - m_new))
    hd['decay'] = jnp.exp(b_last + m - m_new)
    hd['m_new'] = m_new

    log_w = jnp.where(hd['tri'], a_bc + b_row, -jnp.inf)
    log_inter = b_row + m
    m_t = jnp.maximum(log_inter, jnp.max(log_w, axis=0, keepdims=True))
    inter = jnp.exp(log_inter - m_t)
    s_t = hd['kq'] * jnp.exp(log_w - m_t)
    den = inter * hd['qn'] + jnp.sum(s_t, axis=0, keepdims=True)
    hd['r_den'] = 1.0 / jnp.maximum(jnp.abs(den), jnp.exp(-m_t))
    hd['rhs'] = jnp.concatenate([s_t.astype(BF16), (hd['qt'].astype(F32) * inter).astype(BF16)], axis=0)
    wk = w_bc * hd['k'].astype(F32)
    hd['wk'] = wk.astype(BF16)
    hd['wk_sum'] = jnp.sum(wk, axis=0, keepdims=True)


def _mlstm_outputs(hd):
    lhs = jnp.concatenate([hd['vt'], hd['ct'].astype(BF16)], axis=1)
    hd['h_ref'][hd['bi'], :, hd['sl']] = (
        jnp.dot(lhs, hd['rhs'], preferred_element_type=F32) * hd['r_den']).T
    hd['ct_ref'][...] = hd['decay'] * hd['ct'] + jnp.dot(hd['vt'], hd['wk'], preferred_element_type=F32)
    hd['n_ref'][...] = hd['decay'] * hd['n'] + hd['wk_sum']
    hd['m_ref'][...] = jnp.broadcast_to(hd['m_new'], (1, LANES))


def _mlstm_kernel(qtf_ref, kf_ref, vtf_ref, gf_ref, gtf_ref, qtb_ref, kb_ref, vtb_ref, gb_ref, gtb_ref,
                  hf_ref, hb_ref, *state_refs):
    per = MLSTM_BATCH * 2 * HEADS
    ct_refs, n_refs, m_refs = state_refs[:per], state_refs[per:2 * per], state_refs[2 * per:]

    @pl.when(pl.program_id(1) == 0)
    def _():
        for ref in state_refs:
            ref[...] = jnp.zeros_like(ref)

    heads = []
    for bi in range(MLSTM_BATCH):
        for direction, refs in enumerate(((qtf_ref, kf_ref, vtf_ref, gf_ref, gtf_ref, hf_ref),
                                          (qtb_ref, kb_ref, vtb_ref, gb_ref, gtb_ref, hb_ref))):
            own = slice((2 * bi + direction) * HEADS, (2 * bi + direction + 1) * HEADS)
            heads += _mlstm_chunk(direction, bi, *refs, ct_refs[own], n_refs[own], m_refs[own])
    for phase in (_mlstm_scores, _mlstm_weights, _mlstm_outputs):
        for hd in heads:
            phase(hd)


def _mlstm(qt, k, vt, gates, gates_t, ctx_len):
    b, s, w = k.shape
    lc = MLSTM_CHUNK
    n_chunks = s // lc
    n_ctx = ctx_len // lc

    def back(j):
        return jnp.where(j < n_ctx, n_ctx - 1 - j, n_chunks + n_ctx - 1 - j)

    nb = MLSTM_BATCH

    def specs(direction, chunk):
        return [pl.BlockSpec((nb, w, lc), lambda i, j: (i, 0, chunk(j))),
                pl.BlockSpec((nb, lc, w), lambda i, j: (i, chunk(j), 0)),
                pl.BlockSpec((nb, w, lc), lambda i, j: (i, 0, chunk(j))),
                pl.BlockSpec((nb, lc, LANES), lambda i, j: (i, chunk(j), direction)),
                pl.BlockSpec((nb, LANES, lc), lambda i, j: (i, direction, chunk(j)))]

    fwd = lambda j: j
    out = jax.ShapeDtypeStruct((b, s, w), F32)
    chains = nb * 2 * HEADS
    return pl.pallas_call(
        _mlstm_kernel,
        out_shape=(out, out),
        grid=(b // nb, n_chunks),
        in_specs=specs(0, fwd) + specs(1, back),
        out_specs=(pl.BlockSpec((nb, lc, w), lambda i, j: (i, j, 0)),
                   pl.BlockSpec((nb, lc, w), lambda i, j: (i, back(j), 0))),
        scratch_shapes=([pltpu.VMEM((HEAD_DIM, HEAD_DIM), F32)] * chains
                        + [pltpu.VMEM((1, HEAD_DIM), F32)] * chains
                        + [pltpu.VMEM((1, LANES), F32)] * chains),
        compiler_params=_params("arbitrary", "arbitrary"),
        name="mlstm",
    )(qt, k, vt, gates, gates_t, qt, k, vt, gates, gates_t)


def _s5_operators(a_re, a_im, log_dt, b_re, b_im, c_re, c_im, d_skip):
    hp = lax.Precision.HIGHEST
    lch = S5_CHUNK
    depth = a_re.shape[0]
    a_re = a_re.astype(F32)
    a_im = a_im.astype(F32)
    dt = jnp.exp(log_dt.astype(F32))[..., None]
    la_re, la_im = dt * a_re, dt * a_im
    mag = jnp.exp(la_re)
    ab_re, ab_im = mag * jnp.cos(la_im), mag * jnp.sin(la_im)
    nr, ni = ab_re - 1.0, ab_im
    den = a_re * a_re + a_im * a_im
    f_re = ((nr * a_re + ni * a_im) / den)[:, :, :, None, :]
    f_im = ((ni * a_re - nr * a_im) / den)[:, :, :, None, :]
    bt_re = jnp.swapaxes(b_re.astype(F32), -1, -2)[:, None]
    bt_im = jnp.swapaxes(b_im.astype(F32), -1, -2)[:, None]
    bb_re = f_re * bt_re - f_im * bt_im
    bb_im = f_re * bt_im + f_im * bt_re
    c_re = c_re.astype(F32)
    c_im = c_im.astype(F32)

    n = jnp.arange(-(lch - 1), lch + 1, dtype=F32)[:, None]
    mg = jnp.exp(n * la_re[:, :, :, None, :])
    ang = n * la_im[:, :, :, None, :]
    pw_re, pw_im = mg * jnp.cos(ang), mg * jnp.sin(ang)
    zero = lch - 1

    def powers(direction, first, step):
        start = zero + first
        stop = start + step * lch
        sl = slice(start, stop if stop >= 0 else None, step)
        return pw_re[:, direction, :, sl, None, :], pw_im[:, direction, :, sl, None, :]

    def times(x_re, x_im, p):
        return x_re * p[0] - x_im * p[1], x_re * p[1] + x_im * p[0]

    def rows256(x):
        return x.reshape(depth, S5_GROUPS, S5_ROW, S5_STATE)

    bbf = (bb_re[:, 0, :, None], bb_im[:, 0, :, None])
    bbb = (bb_re[:, 1, :, None], bb_im[:, 1, :, None])
    cc = (c_re[:, :, None], c_im[:, :, None])

    def response(bb, direction, sign):
        l_re, l_im = times(*bb, powers(direction, 0, -sign))
        r_re, r_im = times(*cc, powers(direction, 0, sign))
        left = jnp.concatenate([rows256(l_re), rows256(l_im)], axis=-1)
        right = jnp.concatenate([rows256(r_re), -rows256(r_im)], axis=-1)
        return jnp.einsum('lgap,lgbp->lgab', left, right, precision=hp)

    src = (jnp.arange(S5_ROW) // S5_GROUP_DIM)[:, None]
    tgt = (jnp.arange(S5_ROW) // S5_GROUP_DIM)[None, :]
    d_rows = jnp.tile(d_skip.astype(F32).reshape(depth, S5_GROUPS, 1, S5_GROUP_DIM), (1, 1, 1, lch))
    toeplitz = (jnp.where(tgt >= src, response(bbf, 0, 1), 0.0)
                + jnp.where(src >= tgt, response(bbb, 1, -1), 0.0)
                + jnp.eye(S5_ROW, dtype=F32) * d_rows)

    def pair_block(x_re, x_im):
        x_re = x_re.reshape(depth, S5_GROUPS // 2, 2, S5_ROW, S5_STATE)
        x_im = x_im.reshape(depth, S5_GROUPS // 2, 2, S5_ROW, S5_STATE)
        z = jnp.zeros_like(x_re[:, :, 0])
        top = jnp.concatenate([x_re[:, :, 0], z, x_im[:, :, 0], z], axis=-1)
        bottom = jnp.concatenate([z, x_re[:, :, 1], z, x_im[:, :, 1]], axis=-1)
        return jnp.concatenate([top, bottom], axis=-2)

    def state_map(x, p):
        re, im = times(*x, p)
        return rows256(re), rows256(im)

    w_f = pair_block(*state_map(bbf, powers(0, lch - 1, -1)))
    w_b = pair_block(*state_map(bbb, powers(1, 0, 1)))
    w_pair = jnp.concatenate([w_f, w_b], axis=-1)
    vf_re, vf_im = state_map(cc, powers(0, 1, 1))
    vb_re, vb_im = state_map(cc, powers(1, lch, -1))
    vt_pair = jnp.stack([pair_block(vf_re, -vf_im), pair_block(vb_re, -vb_im)], axis=1)

    al_re, al_im = pw_re[:, :, :, zero + lch], pw_im[:, :, :, zero + lch]
    def rows(a):
        a = a.reshape(depth, 2, 1, S5_GROUPS * S5_STATE)
        return jnp.broadcast_to(a, (depth, 2, S5_BATCH, S5_GROUPS * S5_STATE)).reshape(depth, 2 * S5_BATCH, -1)
    a_step = jnp.stack([rows(al_re), rows(al_im)], axis=1)
    return toeplitz.astype(BF16), w_pair.astype(BF16), vt_pair.astype(BF16), a_step


def _s5_chunk_state_kernel(u_ref, w_ref, x_ref, *, n_chunks):
    u = jnp.concatenate([u_ref[0], u_ref[1]], axis=-1).astype(BF16)
    x = jnp.dot(u, w_ref[0], preferred_element_type=F32)
    for b in range(S5_BATCH):
        xb = x[b * n_chunks:(b + 1) * n_chunks]
        for direction in range(2):
            for part in range(2):
                lo = (2 * direction + part) * S5_PAIR
                x_ref[part, pl.ds(direction * S5_BATCH + b, n_chunks, stride=2 * S5_BATCH), :] = (
                    xb[:, lo:lo + S5_PAIR])


def _s5_scan_kernel(x_ref, a_ref, of_ref, ob_ref, *, n_chunks, n_ctx):
    tile = 2 * S5_BATCH
    n_slabs = x_ref.shape[0]
    a_re = a_ref[0]
    a_im = a_ref[1]
    fwd_rows = lax.broadcasted_iota(jnp.int32, (tile, S5_PAIR), 0) < S5_BATCH

    def body(i, state):
        cb = jnp.where(i < n_ctx, n_ctx - 1 - i, n_chunks + n_ctx - 1 - i)
        rf = pl.multiple_of(i * tile, tile)
        rb = pl.multiple_of(cb * tile, tile)
        new_state = []
        for p in range(n_slabs // 2):
            re, im = state[2 * p], state[2 * p + 1]
            for j, val in ((2 * p, re), (2 * p + 1, im)):
                of_ref[j, pl.ds(rf, tile), :] = val
                ob_ref[j, pl.ds(rb, tile), :] = val
            in_re = jnp.where(fwd_rows, x_ref[2 * p, pl.ds(rf, tile), :], x_ref[2 * p, pl.ds(rb, tile), :])
            in_im = jnp.where(fwd_rows, x_ref[2 * p + 1, pl.ds(rf, tile), :],
                              x_ref[2 * p + 1, pl.ds(rb, tile), :])
            ar, ai = a_re[:, p * S5_PAIR:(p + 1) * S5_PAIR], a_im[:, p * S5_PAIR:(p + 1) * S5_PAIR]
            new_state.append(ar * re - ai * im + in_re)
            new_state.append(ar * im + ai * re + in_im)
        return tuple(new_state)

    zero = jnp.zeros((tile, S5_PAIR), F32)
    lax.fori_loop(0, n_chunks, body, (zero,) * n_slabs, unroll=4)


def _s5_output_kernel(u_ref, t_ref, xf_ref, xb_ref, v_ref, y_ref, *, n_chunks):
    stride = 2 * S5_BATCH

    def batch_rows(ref, first_row):
        return jnp.concatenate(
            [jnp.concatenate([ref[part, pl.ds(first_row + b, n_chunks, stride=stride), :]
                              for part in range(2)], axis=-1) for b in range(S5_BATCH)], axis=0)

    xf = batch_rows(xf_ref, 0)
    xb = batch_rows(xb_ref, S5_BATCH)
    last = (((1,), (1,)), ((), ()))
    y = lax.dot_general(xf.astype(BF16), v_ref[0, 0], last, preferred_element_type=F32)
    y = y + lax.dot_general(xb.astype(BF16), v_ref[1, 0], last, preferred_element_type=F32)
    for a in range(2):
        within = jnp.dot(u_ref[a].astype(BF16), t_ref[a], preferred_element_type=F32)
        y_ref[a] = within + y[:, a * S5_ROW:(a + 1) * S5_ROW]


def _s5_pair_kernel(u_ref, w_ref, t_ref, v_ref, a_ref, y_ref, xc_ref, xf_ref, xb_ref, *, n_chunks, n_ctx):
    _s5_chunk_state_kernel(u_ref, w_ref, xc_ref, n_chunks=n_chunks)
    _s5_scan_kernel(xc_ref, a_ref, xf_ref, xb_ref, n_chunks=n_chunks, n_ctx=n_ctx)
    _s5_output_kernel(u_ref, t_ref, xf_ref, xb_ref, v_ref, y_ref, n_chunks=n_chunks)


def _s5_mixer(u_rows, ops, layer, n_chunks, n_ctx):
    toeplitz, w_pair, vt_pair, a_step = ops
    g, rows, _ = u_rows.shape
    n_pairs = g // 2
    states = pltpu.VMEM((2, n_chunks * 2 * S5_BATCH, S5_PAIR), F32)
    return pl.pallas_call(
        functools.partial(_s5_pair_kernel, n_chunks=n_chunks, n_ctx=n_ctx),
        out_shape=jax.ShapeDtypeStruct((g, rows, S5_ROW), F32),
        grid=(n_pairs,),
        in_specs=[pl.BlockSpec((2, rows, S5_ROW), lambda i: (i, 0, 0)),
                  pl.BlockSpec((None, 1, 2 * S5_ROW, 4 * S5_PAIR), lambda i: (layer, i, 0, 0)),
                  pl.BlockSpec((None, 2, S5_ROW, S5_ROW), lambda i: (layer, i, 0, 0)),
                  pl.BlockSpec((None, 2, 1, 2 * S5_ROW, 2 * S5_PAIR), lambda i: (layer, 0, i, 0, 0)),
                  pl.BlockSpec((None, 2, 2 * S5_BATCH, S5_PAIR), lambda i: (layer, 0, 0, i))],
        out_specs=pl.BlockSpec((2, rows, S5_ROW), lambda i: (i, 0, 0)),
        scratch_shapes=[states, states, states],
        compiler_params=_params("arbitrary"),
        name="s5_pair",
    )(u_rows, w_pair, toeplitz, vt_pair, a_step)


def _merge_kernel(xc_ref, xl_ref, mod_ref, gain_ref, hf_ref, hb_ref, ys_ref, wo_ref, gmh_ref, gsgu_ref,
                  wsgu_ref, bsgu_ref, wglu_ref, bglu_ref, wsc_ref, wum_ref, wug_ref, wus_ref,
                  wuc_ref, wout_ref, o_ref, ybuf_ref, *, ctx_len, tile0):
    t = pl.program_id(1) + tile0
    x = _stream_tile(t, xc_ref, xl_ref)
    mod = mod_ref[0, 0]
    h = _modulated_norm(x, gain_ref[...], mod[0:1], mod[1:2]).astype(BF16)
    tm = x.shape[0]

    def proj(start, width):
        return jnp.dot(h, wo_ref[:, start:start + width], preferred_element_type=F32)

    w = MLSTM_WIDTH
    z_o, z_su, z_sv, z_cb, z_cc, z_cx = (proj(i * w, w) for i in range(6))
    ys = jax.nn.gelu(_from_group_rows(ys_ref, ybuf_ref))
    z_glu = jnp.dot(ys.astype(BF16), wglu_ref[...], preferred_element_type=F32)
    gates = [proj(6 * w + j * D_MODEL, D_MODEL) for j in range(N_BRANCH)]

    hm = hf_ref[0] + hb_ref[0]
    parts = []
    for hd in range(HEADS):
        hh = hm[:, hd * HEAD_DIM:(hd + 1) * HEAD_DIM]
        parts.append(hh * lax.rsqrt(jnp.mean(hh * hh, axis=-1, keepdims=True) + RMS_EPS))
    y_a = (jnp.concatenate(parts, axis=-1) * gmh_ref[...]) * _sigmoid(z_o)

    su = jax.nn.gelu(z_su)
    sv = jax.nn.gelu(z_sv)
    mu = jnp.mean(sv, axis=-1, keepdims=True)
    cen = sv - mu
    var = jnp.mean(cen * cen, axis=-1, keepdims=True)
    vn = (cen * lax.rsqrt(var + RMS_EPS) * gsgu_ref[...]).astype(BF16)
    bias = bsgu_ref[...]
    rows = []
    for n in range(tm // SGU_CHUNK):
        cols = []
        for gi in range(SGU_GROUPS):
            blk = vn[n * SGU_CHUNK:(n + 1) * SGU_CHUNK, gi * SGU_GROUP_DIM:(gi + 1) * SGU_GROUP_DIM]
            cols.append(jnp.dot(wsgu_ref[gi], blk, preferred_element_type=F32) + bias[:, gi:gi + 1])
        rows.append(jnp.concatenate(cols, axis=-1))
    y_b = su * jnp.concatenate(rows, axis=0)

    y_c = ys * _sigmoid(z_glu + bglu_ref[...])

    first, last = _row_edges(t, tm, ctx_len)
    y_d = z_cb * _conv3(z_cc * z_cx, wsc_ref[...], first, last)

    acc = None
    for y, up_ref, gate in zip((y_a, y_b, y_c, y_d), (wum_ref, wug_ref, wus_ref, wuc_ref), gates):
        term = _sigmoid(gate) * jnp.dot(y.astype(BF16), up_ref[...], preferred_element_type=F32)
        acc = term if acc is None else acc + term
    o_ref[0] = x + mod[2:3] * jnp.dot(acc.astype(BF16), wout_ref[...], preferred_element_type=F32)


def _merge(stream, s, mods, h_fwd, h_bwd, y_s5, lw, layer, ctx_len, tile0):
    b, _, d = stream[0].shape
    tm = TOKEN_TILE
    tiles = s // tm
    stream_specs, stream_args = _stream_specs(stream, tile0)
    tok = lambda width: pl.BlockSpec((1, tm, width), lambda i, t: (i, t + tile0, 0))
    group_rows = pl.BlockSpec((S5_GROUPS, tm // S5_CHUNK, S5_ROW),
                              lambda i, t: (0, i * tiles + t + tile0, 0))
    weights = tuple(lw[name] for name in (
        'w_out_side', 'g_mh', 'g_sgu', 'w_sgu', 'b_sgu_t', 'w_glu', 'b_glu', 'w_sconv',
        'w_up_mlstm', 'w_up_sgu', 'w_up_s5', 'w_up_sconv', 'w_out'))
    return pl.pallas_call(
        functools.partial(_merge_kernel, ctx_len=ctx_len, tile0=tile0),
        out_shape=jax.ShapeDtypeStruct((b, s - tile0 * tm, d), F32),
        grid=(b, tiles - tile0),
        in_specs=stream_specs
                 + [_mod_spec(d, layer, tile0),
                    _layer_resident(lw['gain_mix'], layer), tok(MLSTM_WIDTH), tok(MLSTM_WIDTH), group_rows]
                 + [_layer_resident(w, layer) for w in weights],
        out_specs=pl.BlockSpec((1, tm, d), lambda i, t: (i, t, 0)),
        scratch_shapes=[pltpu.VMEM((S5_WIDTH // LANES, tm, LANES), F32)],
        compiler_params=_params("arbitrary", "arbitrary"),
        name="merge",
    )(*stream_args, mods, lw['gain_mix'], h_fwd, h_bwd, y_s5, *weights)


def _ffn_kernel(x_ref, mod_ref, gain_ref, wg_ref, wu_ref, wd_ref, gf_ref, o_ref, *, final):
    x = x_ref[0]
    mod = mod_ref[0, 0]
    h = _modulated_norm(x, gain_ref[...], mod[3:4], mod[4:5]).astype(BF16)
    a = jnp.dot(h, wg_ref[...], preferred_element_type=F32)
    b = jnp.dot(h, wu_ref[...], preferred_element_type=F32)
    hid = ((a * _sigmoid(a)) * b).astype(BF16)
    y = x + mod[5:6] * jnp.dot(hid, wd_ref[...], preferred_element_type=F32)
    if final:
        y = y * lax.rsqrt(jnp.mean(y * y, axis=-1, keepdims=True) + RMS_EPS) * gf_ref[...]
    o_ref[0] = y


def _ffn(xs, mods, lw, layer, g_final, tile0, final):
    b, s, d = xs.shape
    tm = TOKEN_TILE
    weights = tuple(lw[name] for name in ('gain_ffn', 'w_ffn_gate', 'w_ffn_up', 'w_ffn_down'))
    return pl.pallas_call(
        functools.partial(_ffn_kernel, final=final),
        out_shape=jax.ShapeDtypeStruct((b, s, d), F32),
        grid=(b, s // tm),
        in_specs=[pl.BlockSpec((1, tm, d), lambda i, t: (i, t, 0)), _mod_spec(d, layer, tile0)]
                 + [_layer_resident(w, layer) for w in weights] + [_resident((1, d))],
        out_specs=pl.BlockSpec((1, tm, d), lambda i, t: (i, t, 0)),
        compiler_params=_params("arbitrary", "arbitrary"),
        name="ffn_final" if final else "ffn",
    )(xs, mods, *weights, g_final)


def _stacked_weights(g_norm_mix, g_norm_ffn, w_in, b_gates, w_conv_qk, g_mh, g_sgu, w_sgu, b_sgu, w_glu,
                     b_glu, w_sconv, w_up_mlstm, w_up_sgu, w_up_s5, w_up_sconv, w_out,
                     w_ffn_gate, w_ffn_up, w_ffn_down):
    depth = w_in.shape[0]
    row = lambda a: a[:, None, :]
    gate_pad = ((0, 0), (0, 0), (0, LANES - 2 * HEADS))
    w_gates = jnp.concatenate([
        jnp.pad(w_in[:, :, GATE_COL:GATE_COL + 2 * HEADS], gate_pad),
        jnp.pad(w_in[:, :, GATE_COL + 2 * HEADS:GATE_COL + 4 * HEADS], gate_pad)], axis=2).astype(BF16)
    w_state = jnp.concatenate([
        w_in[:, :, :2 * MLSTM_WIDTH].astype(BF16), w_gates,
        w_in[:, :, GATE_COL + 4 * HEADS:STATE_DIM].astype(BF16)], axis=2)
    b_pad = ((0, 0), (0, LANES - 2 * HEADS))
    bg = jnp.concatenate([jnp.pad(b_gates[:, :2 * HEADS], b_pad), jnp.pad(b_gates[:, 2 * HEADS:], b_pad)],
                         axis=1)
    return dict(
        gain_mix=row(g_norm_mix),
        gain_ffn=row(g_norm_ffn),
        w_state=w_state,
        w_v_t=jnp.swapaxes(w_in[:, :, 2 * MLSTM_WIDTH:GATE_COL], 1, 2).astype(BF16),
        w_gates_t=jnp.swapaxes(w_gates, 1, 2),
        b_gates=row(bg),
        b_gates_t=jnp.broadcast_to(bg[:, :, None], (depth, 2 * LANES, TOKEN_TILE)),
        w_conv=w_conv_qk.reshape(depth, 2 * CONV_K, MLSTM_WIDTH),
        w_out_side=w_in[:, :, STATE_DIM:].astype(BF16),
        g_mh=row(g_mh),
        g_sgu=row(g_sgu),
        w_sgu=w_sgu.astype(BF16),
        b_sgu_t=jnp.swapaxes(b_sgu, 1, 2),
        w_glu=w_glu.astype(BF16),
        b_glu=row(b_glu),
        w_sconv=w_sconv,
        w_up_mlstm=w_up_mlstm.astype(BF16),
        w_up_sgu=w_up_sgu.astype(BF16),
        w_up_s5=w_up_s5.astype(BF16),
        w_up_sconv=w_up_sconv.astype(BF16),
        w_out=w_out.astype(BF16),
        w_ffn_gate=w_ffn_gate.astype(BF16),
        w_ffn_up=w_ffn_up.astype(BF16),
        w_ffn_down=w_ffn_down.astype(BF16),
    )


def kernel(x, c, ctx, c_ctx, w_mod, b_mod, g_norm_mix, g_norm_ffn, w_in, b_gates, w_conv_qk, g_mh, g_sgu, w_sgu, b_sgu, s5_a_re, s5_a_im, s5_log_dt, s5_b_re, s5_b_im, s5_c_re, s5_c_im, s5_d, w_glu, b_glu, w_sconv, w_up_mlstm, w_up_sgu, w_up_s5, w_up_sconv, w_out, w_ffn_gate, w_ffn_up, w_ffn_down, g_final):
    batch, seq, d = x.shape
    ctx_len = ctx.shape[1]
    depth = w_mod.shape[0]
    assert d == D_MODEL and ctx_len == TOKEN_TILE and seq % TOKEN_TILE == 0 and batch == S5_BATCH
    ctx_tiles = ctx_len // TOKEN_TILE

    c_rows = jnp.zeros((8, d), F32).at[:batch].set(c).at[batch].set(c_ctx)
    mod_all = _modulation(c_rows, w_mod, b_mod).reshape(depth, 8, 6, d)
    stream = (ctx, x, ctx_tiles)
    stream_len = ctx_len + seq
    s5_ops = _s5_operators(s5_a_re, s5_a_im, s5_log_dt, s5_b_re, s5_b_im, s5_c_re, s5_c_im, s5_d)
    lw = _stacked_weights(g_norm_mix, g_norm_ffn, w_in, b_gates, w_conv_qk, g_mh, g_sgu, w_sgu, b_sgu,
                          w_glu, b_glu, w_sconv, w_up_mlstm, w_up_sgu, w_up_s5, w_up_sconv, w_out,
                          w_ffn_gate, w_ffn_up, w_ffn_down)

    mod_x = mod_all[:, :batch]
    mod_c = jnp.broadcast_to(mod_all[:, batch][:, None], mod_x.shape)
    mods = jnp.stack([mod_c, mod_x], axis=2)

    out = None
    for l in range(depth):
        last = l == depth - 1
        qt, k, vt, gates, gates_t, u_rows = _state_proj(stream, stream_len, mods, lw, l, ctx_len)
        h_fwd, h_bwd = _mlstm(qt, k, vt, gates, gates_t, ctx_len)
        y_s5 = _s5_mixer(u_rows, s5_ops, l, stream_len // S5_CHUNK, ctx_len // S5_CHUNK)
        tile0 = ctx_tiles if last else 0
        x_mid = _merge(stream, stream_len, mods, h_fwd, h_bwd, y_s5, lw, l, ctx_len, tile0)
        out = _ffn(x_mid, mods, lw, l, g_final.reshape(1, d), tile0, last)
        stream = (out, out, 0)
    return out
```

```python
import functools
import math

import jax
import jax.numpy as jnp
from jax import lax
from jax.experimental import pallas as pl
from jax.experimental.pallas import tpu as pltpu

F32 = jnp.float32
BF16 = jnp.bfloat16

D_MODEL = 1024
GRID_W = 64
N_BRANCH = 4
RMS_EPS = 1e-6
CONV_K = 3
HEADS = 4
HEAD_DIM = 128
MLSTM_WIDTH = HEADS * HEAD_DIM
MLSTM_CHUNK = 128
MLSTM_BATCH = 4
SGU_GROUPS = 4
SGU_GROUP_DIM = 128
SGU_WIDTH = SGU_GROUPS * SGU_GROUP_DIM
SGU_CHUNK = 128
S5_GROUP_DIM = 16
S5_GROUPS = 24
S5_WIDTH = S5_GROUPS * S5_GROUP_DIM
S5_STATE = 64
SCONV_WIDTH = 512
STATE_DIM = 3 * MLSTM_WIDTH + 4 * HEADS + S5_WIDTH
GATE_COL = 3 * MLSTM_WIDTH

LANES = 128
TOKEN_TILE = 256
S5_CHUNK = 16
S5_ROW = S5_CHUNK * S5_GROUP_DIM
S5_PAIR = 2 * S5_STATE
S5_BATCH = 4
U_COL = 2 * MLSTM_WIDTH
VMEM_LIMIT = 52 * 1024 * 1024


def _resident(shape):
    zeros = (0,) * len(shape)
    return pl.BlockSpec(shape, lambda *_: zeros, pipeline_mode=pl.Buffered(1))


def _layer_resident(stacked, layer):
    zeros = (0,) * (stacked.ndim - 1)
    return pl.BlockSpec((None,) + stacked.shape[1:], lambda *_: (layer,) + zeros,
                        pipeline_mode=pl.Buffered(1))


def _mod_spec(d, layer, tile0):
    return pl.BlockSpec((None, 1, 1, 6, d), lambda i, t: (layer, i, jnp.minimum(t + tile0, 1), 0, 0))


def _params(*semantics, flags=None):
    return pltpu.CompilerParams(dimension_semantics=semantics, vmem_limit_bytes=VMEM_LIMIT, flags=flags)


def _sigmoid(x):
    return 0.5 * jnp.tanh(0.5 * x) + 0.5


def _modulated_norm(x, gain, shift, scale):
    y = x * lax.rsqrt(jnp.mean(x * x, axis=-1, keepdims=True) + RMS_EPS) * gain
    return y * (1.0 + scale) + shift


def _conv3(a, w, first, last):
    n = a.shape[0]
    prev = jnp.where(first, 0.0, pltpu.roll(a, 1, 0))
    nxt = jnp.where(last, 0.0, pltpu.roll(a, n - 1, 0))
    return prev * w[0:1] + a * w[1:2] + nxt * w[2:3]


def _row_edges(tile_idx, n_rows, ctx_len):
    period = jnp.where(tile_idx == 0, ctx_len, GRID_W)
    pos = lax.broadcasted_iota(jnp.int32, (n_rows, 1), 0) & (period - 1)
    return pos == 0, pos == period - 1


def _mod_kernel(c_ref, w_ref, b_ref, o_ref):
    a = c_ref[...]
    a = a * _sigmoid(a)
    a_hi = a.astype(BF16)
    a_lo = (a - a_hi.astype(F32)).astype(BF16)
    w = w_ref[0]
    w_hi = w.astype(BF16)
    w_lo = (w - w_hi.astype(F32)).astype(BF16)
    acc = jnp.dot(a_hi, w_hi, preferred_element_type=F32)
    acc = acc + jnp.dot(a_lo, w_hi, preferred_element_type=F32)
    acc = acc + jnp.dot(a_hi, w_lo, preferred_element_type=F32)
    o_ref[0] = acc + b_ref[0]


def _modulation(c_rows, w_mod, b_mod):
    depth, d, n = w_mod.shape
    tn = 1536
    return pl.pallas_call(
        _mod_kernel,
        out_shape=jax.ShapeDtypeStruct((depth, 8, n), F32),
        grid=(depth, n // tn),
        in_specs=[
            pl.BlockSpec((8, d), lambda l, j: (0, 0)),
            pl.BlockSpec((1, d, tn), lambda l, j: (l, 0, j)),
            pl.BlockSpec((1, 1, tn), lambda l, j: (l, 0, j)),
        ],
        out_specs=pl.BlockSpec((1, 8, tn), lambda l, j: (l, 0, j)),
        compiler_params=_params("arbitrary", "arbitrary"),
        name="modulation",
    )(c_rows, w_mod, b_mod.reshape(depth, 1, n))


def _to_group_rows(z, u_ref, buf_ref):
    n_chunks = z.shape[0] // S5_CHUNK
    per_block = LANES // S5_GROUP_DIM
    for gb in range(S5_WIDTH // LANES):
        buf_ref[gb] = z[:, gb * LANES:(gb + 1) * LANES]
    for s in range(S5_CHUNK):
        for gb in range(S5_WIDTH // LANES):
            zs = buf_ref[gb, pl.ds(s, n_chunks, stride=S5_CHUNK), :]
            for gl in range(per_block):
                u_ref[gb * per_block + gl, :, s * S5_GROUP_DIM:(s + 1) * S5_GROUP_DIM] = (
                    zs[:, gl * S5_GROUP_DIM:(gl + 1) * S5_GROUP_DIM])


def _from_group_rows(y_ref, buf_ref):
    n_chunks = y_ref.shape[1]
    per_block = LANES // S5_GROUP_DIM
    for s in range(S5_CHUNK):
        for gb in range(S5_WIDTH // LANES):
            pieces = [y_ref[gb * per_block + gl, :, s * S5_GROUP_DIM:(s + 1) * S5_GROUP_DIM]
                      for gl in range(per_block)]
            buf_ref[gb, pl.ds(s, n_chunks, stride=S5_CHUNK), :] = jnp.concatenate(pieces, axis=-1)
    return jnp.concatenate([buf_ref[gb] for gb in range(S5_WIDTH // LANES)], axis=-1)


def _stream_specs(stream, tile0):
    ctx_src, lat_src, offset = stream
    d = ctx_src.shape[-1]
    specs = [pl.BlockSpec((1, TOKEN_TILE, d), lambda i, t: (i, 0, 0)),
             pl.BlockSpec((1, TOKEN_TILE, d), lambda i, t: (i, jnp.maximum(t + tile0 - offset, 0), 0))]
    return specs, (ctx_src, lat_src)


def _stream_tile(t, ctx_ref, lat_ref):
    return jnp.where(t == 0, ctx_ref[0], lat_ref[0])


def _state_proj_kernel(xc_ref, xl_ref, mod_ref, gain_ref, w_ref, wvt_ref, wgt_ref, bgt_ref, wc_ref,
                       eye_ref, qt_ref, k_ref, vt_ref, gt_ref, u_ref, ubuf_ref, *, ctx_len):
    t = pl.program_id(1)
    mod = mod_ref[0, 0]
    h = _modulated_norm(_stream_tile(t, xc_ref, xl_ref), gain_ref[...], mod[0:1], mod[1:2]).astype(BF16)
    z = jnp.dot(h, w_ref[...], preferred_element_type=F32)
    contract_last = (((1,), (1,)), ((), ()))
    vt_ref[0] = lax.dot_general(wvt_ref[...], h, contract_last, preferred_element_type=F32).astype(BF16)
    gt_ref[0] = lax.dot_general(wgt_ref[...], h, contract_last, preferred_element_type=F32) + bgt_ref[...]
    first, last = _row_edges(t, z.shape[0], ctx_len)
    wc = wc_ref[...]
    w = MLSTM_WIDTH
    q = _conv3(z[:, 0:w], wc[0:3], first, last)
    k = _conv3(z[:, w:2 * w], wc[3:6], first, last)
    q = (q * _sigmoid(q)).astype(BF16)
    qt_ref[0] = lax.dot_general(eye_ref[...], q, contract_last, preferred_element_type=F32).astype(BF16)
    k_ref[0] = ((k * _sigmoid(k)) * (HEAD_DIM ** -0.5)).astype(BF16)
    _to_group_rows(z[:, U_COL:U_COL + S5_WIDTH], u_ref, ubuf_ref)


def _state_proj(stream, s, mods, lw, layer, ctx_len):
    b, _, d = stream[0].shape
    tm = TOKEN_TILE
    tiles = s // tm
    stream_specs, stream_args = _stream_specs(stream, 0)
    tok = lambda width: pl.BlockSpec((1, tm, width), lambda i, t: (i, t, 0))
    chan = lambda height: pl.BlockSpec((1, height, tm), lambda i, t: (i, 0, t))
    weights = tuple(lw[name] for name in ('gain_mix', 'w_state', 'w_v_t', 'w_gates_t', 'b_gates_t', 'w_conv'))
    eye = jnp.eye(MLSTM_WIDTH, dtype=BF16)
    return pl.pallas_call(
        functools.partial(_state_proj_kernel, ctx_len=ctx_len),
        out_shape=(
            jax.ShapeDtypeStruct((b, MLSTM_WIDTH, s), BF16),
            jax.ShapeDtypeStruct((b, s, MLSTM_WIDTH), BF16),
            jax.ShapeDtypeStruct((b, MLSTM_WIDTH, s), BF16),
            jax.ShapeDtypeStruct((b, 2 * LANES, s), F32),
            jax.ShapeDtypeStruct((S5_GROUPS, b * s // S5_CHUNK, S5_ROW), F32),
        ),
        grid=(b, s // tm),
        in_specs=stream_specs + [_mod_spec(d, layer, 0)]
                 + [_layer_resident(w, layer) for w in weights] + [_resident(eye.shape)],
        out_specs=(chan(MLSTM_WIDTH), tok(MLSTM_WIDTH), chan(MLSTM_WIDTH), chan(2 * LANES),
                   pl.BlockSpec((S5_GROUPS, tm // S5_CHUNK, S5_ROW), lambda i, t: (0, i * tiles + t, 0))),
        scratch_shapes=[pltpu.VMEM((S5_WIDTH // LANES, tm, LANES), F32)],
        compiler_params=_params("arbitrary", "arbitrary"),
        name="state_proj",
    )(*stream_args, mods, *weights, eye)


def _mlstm_chunk(direction, bi, qt_ref, k_ref, vt_ref, gt_ref, h_ref, ct_ref, n_ref, m_ref):
    lc = MLSTM_CHUNK
    sign = 1 - 2 * direction
    row = lax.broadcasted_iota(jnp.int32, (lc, lc), 0)
    col = lax.broadcasted_iota(jnp.int32, (lc, lc), 1)
    tri = (col - row) * sign >= 0
    tri_b = jnp.where(tri, 1.0, 0.0).astype(BF16)

    gt = gt_ref[bi]
    lf_t = jax.nn.log_sigmoid(gt)
    hi = lf_t.astype(BF16)
    lo = (lf_t - hi.astype(F32)).astype(BF16)
    bcum_t = (jnp.dot(hi, tri_b, preferred_element_type=F32)
              + jnp.dot(lo, tri_b, preferred_element_type=F32))
    last = lc - 1 if direction == 0 else 0

    qt = qt_ref[bi]
    k = k_ref[bi]
    vt = vt_ref[bi]
    heads = []
    for hd in range(HEADS):
        sl = slice(hd * HEAD_DIM, (hd + 1) * HEAD_DIM)
        b_row = bcum_t[HEADS + hd:HEADS + hd + 1, :]
        heads.append(dict(
            bi=bi, sl=sl, qt=qt[sl, :], k=k[:, sl], vt=vt[sl, :], tri=tri,
            a_row=gt[hd:hd + 1, :] - b_row, b_row=b_row, b_last=b_row[:, last:last + 1],
            m=m_ref[hd][:, 0:1], ct=ct_ref[hd][...], n=n_ref[hd][...],
            h_ref=h_ref, ct_ref=ct_ref[hd], n_ref=n_ref[hd], m_ref=m_ref[hd]))
    return heads


def _mlstm_scores(hd):
    hd['kq'] = jnp.dot(hd['k'], hd['qt'], preferred_element_type=F32)
    n_rows = jnp.broadcast_to(hd['n'], (8, HEAD_DIM)).astype(BF16)
    hd['qn'] = jnp.dot(n_rows, hd['qt'], preferred_element_type=F32)[0:1]
    hd['cq'] = jnp.dot(hd['ct'].astype(BF16), hd['qt'], preferred_element_type=F32)


def _mlstm_weights(hd):
    lc = MLSTM_CHUNK
    a_row, b_row, b_last, m = hd['a_row'], hd['b_row'], hd['b_last'], hd['m']
    m_new = b_last + jnp.maximum(m, jnp.max(a_row, axis=1, keepdims=True))
    w_row = jnp.exp(a_row + (b_last - m_new))
    hd['decay'] = jnp.exp(b_last + m - m_new)
    hd['m_new'] = m_new
    hd['vt_w'] = (hd['vt'].astype(F32) * w_row).astype(BF16)
    hd['w_rows'] = jnp.broadcast_to(w_row, (8, lc)).astype(BF16)

    a_bc = jnp.broadcast_to(a_row, (lc, lc)).T
    log_w = jnp.where(hd['tri'], a_bc + b_row, -jnp.inf)
    log_inter = b_row + m
    m_t = jnp.maximum(log_inter, jnp.max(log_w, axis=0, keepdims=True))
    hd['inter'] = jnp.exp(log_inter - m_t)
    s_t = hd['kq'] * jnp.exp(log_w - m_t)
    den = hd['inter'] * hd['qn'] + jnp.sum(s_t, axis=0, keepdims=True)
    hd['r_den'] = 1.0 / jnp.maximum(jnp.abs(den), jnp.exp(-m_t))
    hd['s_t'] = s_t.astype(BF16)


def _mlstm_outputs(hd):
    num = jnp.dot(hd['vt'], hd['s_t'], preferred_element_type=F32) + hd['inter'] * hd['cq']
    hd['h_ref'][hd['bi'], :, hd['sl']] = (num * hd['r_den']).T
    hd['ct_ref'][...] = hd['decay'] * hd['ct'] + jnp.dot(hd['vt_w'], hd['k'], preferred_element_type=F32)
    hd['n_ref'][...] = (hd['decay'] * hd['n']
                        + jnp.dot(hd['w_rows'], hd['k'], preferred_element_type=F32)[0:1])
    hd['m_ref'][...] = jnp.broadcast_to(hd['m_new'], (1, LANES))


def _mlstm_kernel(qtf_ref, kf_ref, vtf_ref, gtf_ref, qtb_ref, kb_ref, vtb_ref, gtb_ref,
                  hf_ref, hb_ref, *state_refs):
    per = MLSTM_BATCH * 2 * HEADS
    ct_refs, n_refs, m_refs = state_refs[:per], state_refs[per:2 * per], state_refs[2 * per:]

    @pl.when(pl.program_id(1) == 0)
    def _():
        for ref in state_refs:
            ref[...] = jnp.zeros_like(ref)

    heads = []
    for bi in range(MLSTM_BATCH):
        for direction, refs in enumerate(((qtf_ref, kf_ref, vtf_ref, gtf_ref, hf_ref),
                                          (qtb_ref, kb_ref, vtb_ref, gtb_ref, hb_ref))):
            own = slice((2 * bi + direction) * HEADS, (2 * bi + direction + 1) * HEADS)
            heads += _mlstm_chunk(direction, bi, *refs, ct_refs[own], n_refs[own], m_refs[own])
    for phase in (_mlstm_scores, _mlstm_weights, _mlstm_outputs):
        for hd in heads:
            phase(hd)


def _mlstm(qt, k, vt, gates_t, ctx_len):
    b, s, w = k.shape
    lc = MLSTM_CHUNK
    n_chunks = s // lc
    n_ctx = ctx_len // lc

    def back(j):
        return jnp.where(j < n_ctx, n_ctx - 1 - j, n_chunks + n_ctx - 1 - j)

    nb = MLSTM_BATCH

    def specs(direction, chunk):
        return [pl.BlockSpec((nb, w, lc), lambda i, j: (i, 0, chunk(j))),
                pl.BlockSpec((nb, lc, w), lambda i, j: (i, chunk(j), 0)),
                pl.BlockSpec((nb, w, lc), lambda i, j: (i, 0, chunk(j))),
                pl.BlockSpec((nb, LANES, lc), lambda i, j: (i, direction, chunk(j)))]

    fwd = lambda j: j
    out = jax.ShapeDtypeStruct((b, s, w), F32)
    chains = nb * 2 * HEADS
    return pl.pallas_call(
        _mlstm_kernel,
        out_shape=(out, out),
        grid=(b // nb, n_chunks),
        in_specs=specs(0, fwd) + specs(1, back),
        out_specs=(pl.BlockSpec((nb, lc, w), lambda i, j: (i, j, 0)),
                   pl.BlockSpec((nb, lc, w), lambda i, j: (i, back(j), 0))),
        scratch_shapes=([pltpu.VMEM((HEAD_DIM, HEAD_DIM), F32)] * chains
                        + [pltpu.VMEM((1, HEAD_DIM), F32)] * chains
                        + [pltpu.VMEM((1, LANES), F32)] * chains),
        compiler_params=_params("arbitrary", "arbitrary"),
        name="mlstm",
    )(qt, k, vt, gates_t, qt, k, vt, gates_t)


def _s5_operators(a_re, a_im, log_dt, b_re, b_im, c_re, c_im, d_skip):
    hp = lax.Precision.HIGH
    lch = S5_CHUNK
    depth = a_re.shape[0]
    a_re = a_re.astype(F32)
    a_im = a_im.astype(F32)
    dt = jnp.exp(log_dt.astype(F32))[..., None]
    la_re, la_im = dt * a_re, dt * a_im
    mag = jnp.exp(la_re)
    ab_re, ab_im = mag * jnp.cos(la_im), mag * jnp.sin(la_im)
    nr, ni = ab_re - 1.0, ab_im
    den = a_re * a_re + a_im * a_im
    f_re = ((nr * a_re + ni * a_im) / den)[:, :, :, None, :]
    f_im = ((ni * a_re - nr * a_im) / den)[:, :, :, None, :]
    bt_re = jnp.swapaxes(b_re.astype(F32), -1, -2)[:, None]
    bt_im = jnp.swapaxes(b_im.astype(F32), -1, -2)[:, None]
    bb_re = f_re * bt_re - f_im * bt_im
    bb_im = f_re * bt_im + f_im * bt_re
    c_re = c_re.astype(F32)
    c_im = c_im.astype(F32)

    n = jnp.arange(-(lch - 1), lch + 1, dtype=F32)[:, None]
    mg = jnp.exp(n * la_re[:, :, :, None, :])
    ang = n * la_im[:, :, :, None, :]
    pw_re, pw_im = mg * jnp.cos(ang), mg * jnp.sin(ang)
    zero = lch - 1

    def powers(direction, first, step):
        start = zero + first
        stop = start + step * lch
        sl = slice(start, stop if stop >= 0 else None, step)
        return pw_re[:, direction, :, sl, None, :], pw_im[:, direction, :, sl, None, :]

    def times(x_re, x_im, p):
        return x_re * p[0] - x_im * p[1], x_re * p[1] + x_im * p[0]

    def rows256(x):
        return x.reshape(depth, S5_GROUPS, S5_ROW, S5_STATE)

    bbf = (bb_re[:, 0, :, None], bb_im[:, 0, :, None])
    bbb = (bb_re[:, 1, :, None], bb_im[:, 1, :, None])
    cc = (c_re[:, :, None], c_im[:, :, None])

    def response(bb, direction, sign):
        l_re, l_im = times(*bb, powers(direction, 0, -sign))
        r_re, r_im = times(*cc, powers(direction, 0, sign))
        left = jnp.concatenate([rows256(l_re), rows256(l_im)], axis=-1)
        right = jnp.concatenate([rows256(r_re), -rows256(r_im)], axis=-1)
        return jnp.einsum('lgap,lgbp->lgab', left, right, precision=hp)

    src = (jnp.arange(S5_ROW) // S5_GROUP_DIM)[:, None]
    tgt = (jnp.arange(S5_ROW) // S5_GROUP_DIM)[None, :]
    d_rows = jnp.tile(d_skip.astype(F32).reshape(depth, S5_GROUPS, 1, S5_GROUP_DIM), (1, 1, 1, lch))
    toeplitz = (jnp.where(tgt >= src, response(bbf, 0, 1), 0.0)
                + jnp.where(src >= tgt, response(bbb, 1, -1), 0.0)
                + jnp.eye(S5_ROW, dtype=F32) * d_rows)

    def pair_block(x_re, x_im):
        x_re = x_re.reshape(depth, S5_GROUPS // 2, 2, S5_ROW, S5_STATE)
        x_im = x_im.reshape(depth, S5_GROUPS // 2, 2, S5_ROW, S5_STATE)
        z = jnp.zeros_like(x_re[:, :, 0])
        top = jnp.concatenate([x_re[:, :, 0], z, x_im[:, :, 0], z], axis=-1)
        bottom = jnp.concatenate([z, x_re[:, :, 1], z, x_im[:, :, 1]], axis=-1)
        return jnp.concatenate([top, bottom], axis=-2)

    def state_map(x, p):
        re, im = times(*x, p)
        return rows256(re), rows256(im)

    w_f = pair_block(*state_map(bbf, powers(0, lch - 1, -1)))
    w_b = pair_block(*state_map(bbb, powers(1, 0, 1)))
    w_pair = jnp.concatenate([w_f, w_b], axis=-1)
    vf_re, vf_im = state_map(cc, powers(0, 1, 1))
    vb_re, vb_im = state_map(cc, powers(1, lch, -1))
    vt_pair = jnp.stack([pair_block(vf_re, -vf_im), pair_block(vb_re, -vb_im)], axis=1)

    al_re, al_im = pw_re[:, :, :, zero + lch], pw_im[:, :, :, zero + lch]
    def rows(a):
        a = a.reshape(depth, 2, 1, S5_GROUPS * S5_STATE)
        return jnp.broadcast_to(a, (depth, 2, S5_BATCH, S5_GROUPS * S5_STATE)).reshape(depth, 2 * S5_BATCH, -1)
    a_step = jnp.stack([rows(al_re), rows(al_im)], axis=1)
    return toeplitz.astype(BF16), w_pair.astype(BF16), vt_pair.astype(BF16), a_step


def _s5_chunk_state_kernel(u_ref, w_ref, x_ref, *, n_chunks):
    u = jnp.concatenate([u_ref[0], u_ref[1]], axis=-1).astype(BF16)
    x = jnp.dot(u, w_ref[0], preferred_element_type=F32)
    for b in range(S5_BATCH):
        xb = x[b * n_chunks:(b + 1) * n_chunks]
        for direction in range(2):
            for part in range(2):
                lo = (2 * direction + part) * S5_PAIR
                x_ref[part, pl.ds(direction * S5_BATCH + b, n_chunks, stride=2 * S5_BATCH), :] = (
                    xb[:, lo:lo + S5_PAIR])


def _s5_scan_kernel(x_ref, a_ref, of_ref, ob_ref, *, n_chunks, n_ctx):
    tile = 2 * S5_BATCH
    n_slabs = x_ref.shape[0]
    a_re = a_ref[0]
    a_im = a_ref[1]
    fwd_rows = lax.broadcasted_iota(jnp.int32, (tile, S5_PAIR), 0) < S5_BATCH

    def body(i, state):
        cb = jnp.where(i < n_ctx, n_ctx - 1 - i, n_chunks + n_ctx - 1 - i)
        rf = pl.multiple_of(i * tile, tile)
        rb = pl.multiple_of(cb * tile, tile)
        new_state = []
        for p in range(n_slabs // 2):
            re, im = state[2 * p], state[2 * p + 1]
            for j, val in ((2 * p, re), (2 * p + 1, im)):
                of_ref[j, pl.ds(rf, tile), :] = val
                ob_ref[j, pl.ds(rb, tile), :] = val
            in_re = jnp.where(fwd_rows, x_ref[2 * p, pl.ds(rf, tile), :], x_ref[2 * p, pl.ds(rb, tile), :])
            in_im = jnp.where(fwd_rows, x_ref[2 * p + 1, pl.ds(rf, tile), :],
                              x_ref[2 * p + 1, pl.ds(rb, tile), :])
            ar, ai = a_re[:, p * S5_PAIR:(p + 1) * S5_PAIR], a_im[:, p * S5_PAIR:(p + 1) * S5_PAIR]
            new_state.append(ar * re - ai * im + in_re)
            new_state.append(ar * im + ai * re + in_im)
        return tuple(new_state)

    zero = jnp.zeros((tile, S5_PAIR), F32)
    lax.fori_loop(0, n_chunks, body, (zero,) * n_slabs, unroll=4)


def _s5_output_kernel(u_ref, t_ref, xf_ref, xb_ref, v_ref, y_ref, *, n_chunks):
    stride = 2 * S5_BATCH

    def batch_rows(ref, first_row):
        return jnp.concatenate(
            [jnp.concatenate([ref[part, pl.ds(first_row + b, n_chunks, stride=stride), :]
                              for part in range(2)], axis=-1) for b in range(S5_BATCH)], axis=0)

    xf = batch_rows(xf_ref, 0)
    xb = batch_rows(xb_ref, S5_BATCH)
    last = (((1,), (1,)), ((), ()))
    y = lax.dot_general(xf.astype(BF16), v_ref[0, 0], last, preferred_element_type=F32)
    y = y + lax.dot_general(xb.astype(BF16), v_ref[1, 0], last, preferred_element_type=F32)
    for a in range(2):
        within = jnp.dot(u_ref[a].astype(BF16), t_ref[a], preferred_element_type=F32)
        y_ref[a] = within + y[:, a * S5_ROW:(a + 1) * S5_ROW]


def _s5_pair_kernel(u_ref, w_ref, t_ref, v_ref, a_ref, y_ref, xc_ref, xf_ref, xb_ref, *, n_chunks, n_ctx):
    _s5_chunk_state_kernel(u_ref, w_ref, xc_ref, n_chunks=n_chunks)
    _s5_scan_kernel(xc_ref, a_ref, xf_ref, xb_ref, n_chunks=n_chunks, n_ctx=n_ctx)
    _s5_output_kernel(u_ref, t_ref, xf_ref, xb_ref, v_ref, y_ref, n_chunks=n_chunks)


def _s5_mixer(u_rows, ops, layer, n_chunks, n_ctx):
    toeplitz, w_pair, vt_pair, a_step = ops
    g, rows, _ = u_rows.shape
    n_pairs = g // 2
    states = pltpu.VMEM((2, n_chunks * 2 * S5_BATCH, S5_PAIR), F32)
    return pl.pallas_call(
        functools.partial(_s5_pair_kernel, n_chunks=n_chunks, n_ctx=n_ctx),
        out_shape=jax.ShapeDtypeStruct((g, rows, S5_ROW), F32),
        grid=(n_pairs,),
        in_specs=[pl.BlockSpec((2, rows, S5_ROW), lambda i: (i, 0, 0)),
                  pl.BlockSpec((None, 1, 2 * S5_ROW, 4 * S5_PAIR), lambda i: (layer, i, 0, 0)),
                  pl.BlockSpec((None, 2, S5_ROW, S5_ROW), lambda i: (layer, i, 0, 0)),
                  pl.BlockSpec((None, 2, 1, 2 * S5_ROW, 2 * S5_PAIR), lambda i: (layer, 0, i, 0, 0)),
                  pl.BlockSpec((None, 2, 2 * S5_BATCH, S5_PAIR), lambda i: (layer, 0, 0, i))],
        out_specs=pl.BlockSpec((2, rows, S5_ROW), lambda i: (i, 0, 0)),
        scratch_shapes=[states, states, states],
        compiler_params=_params("arbitrary"),
        name="s5_pair",
    )(u_rows, w_pair, toeplitz, vt_pair, a_step)


def _merge_kernel(xc_ref, xl_ref, mod_ref, gain_ref, hf_ref, hb_ref, ys_ref, wo_ref, gmh_ref, gsgu_ref,
                  wsgu_ref, bsgu_ref, wglu_ref, bglu_ref, wsc_ref, wum_ref, wug_ref, wus_ref,
                  wuc_ref, wout_ref, o_ref, ybuf_ref, *, ctx_len, tile0):
    t = pl.program_id(1) + tile0
    x = _stream_tile(t, xc_ref, xl_ref)
    mod = mod_ref[0, 0]
    h = _modulated_norm(x, gain_ref[...], mod[0:1], mod[1:2]).astype(BF16)
    tm = x.shape[0]

    def proj(start, width):
        return jnp.dot(h, wo_ref[:, start:start + width], preferred_element_type=F32)

    w = MLSTM_WIDTH
    z_all = proj(0, 6 * w)
    z_o, z_su, z_sv, z_cb, z_cc, z_cx = (z_all[:, i * w:(i + 1) * w] for i in range(6))
    ys = jax.nn.gelu(_from_group_rows(ys_ref, ybuf_ref))
    z_glu = jnp.dot(ys.astype(BF16), wglu_ref[...], preferred_element_type=F32)
    g_all = proj(6 * w, N_BRANCH * D_MODEL)
    gates = [g_all[:, j * D_MODEL:(j + 1) * D_MODEL] for j in range(N_BRANCH)]

    hm = hf_ref[0] + hb_ref[0]
    parts = []
    for hd in range(HEADS):
        hh = hm[:, hd * HEAD_DIM:(hd + 1) * HEAD_DIM]
        parts.append(hh * lax.rsqrt(jnp.mean(hh * hh, axis=-1, keepdims=True) + RMS_EPS))
    y_a = (jnp.concatenate(parts, axis=-1) * gmh_ref[...]) * _sigmoid(z_o)

    su = jax.nn.gelu(z_su)
    sv = jax.nn.gelu(z_sv)
    mu = jnp.mean(sv, axis=-1, keepdims=True)
    cen = sv - mu
    var = jnp.mean(cen * cen, axis=-1, keepdims=True)
    vn = (cen * lax.rsqrt(var + RMS_EPS) * gsgu_ref[...]).astype(BF16)
    bias = bsgu_ref[...]
    rows = []
    for n in range(tm // SGU_CHUNK):
        cols = []
        for gi in range(SGU_GROUPS):
            blk = vn[n * SGU_CHUNK:(n + 1) * SGU_CHUNK, gi * SGU_GROUP_DIM:(gi + 1) * SGU_GROUP_DIM]
            cols.append(jnp.dot(wsgu_ref[gi], blk, preferred_element_type=F32) + bias[:, gi:gi + 1])
        rows.append(jnp.concatenate(cols, axis=-1))
    y_b = su * jnp.concatenate(rows, axis=0)

    y_c = ys * _sigmoid(z_glu + bglu_ref[...])

    first, last = _row_edges(t, tm, ctx_len)
    y_d = z_cb * _conv3(z_cc * z_cx, wsc_ref[...], first, last)

    acc = None
    for y, up_ref, gate in zip((y_a, y_b, y_c, y_d), (wum_ref, wug_ref, wus_ref, wuc_ref), gates):
        term = _sigmoid(gate) * jnp.dot(y.astype(BF16), up_ref[...], preferred_element_type=F32)
        acc = term if acc is None else acc + term
    o_ref[0] = x + mod[2:3] * jnp.dot(acc.astype(BF16), wout_ref[...], preferred_element_type=F32)


def _merge(stream, s, mods, h_fwd, h_bwd, y_s5, lw, layer, ctx_len, tile0):
    b, _, d = stream[0].shape
    tm = TOKEN_TILE
    tiles = s // tm
    stream_specs, stream_args = _stream_specs(stream, tile0)
    tok = lambda width: pl.BlockSpec((1, tm, width), lambda i, t: (i, t + tile0, 0))
    group_rows = pl.BlockSpec((S5_GROUPS, tm // S5_CHUNK, S5_ROW),
                              lambda i, t: (0, i * tiles + t + tile0, 0))
    weights = tuple(lw[name] for name in (
        'w_out_side', 'g_mh', 'g_sgu', 'w_sgu', 'b_sgu_t', 'w_glu', 'b_glu', 'w_sconv',
        'w_up_mlstm', 'w_up_sgu', 'w_up_s5', 'w_up_sconv', 'w_out'))
    return pl.pallas_call(
        functools.partial(_merge_kernel, ctx_len=ctx_len, tile0=tile0),
        out_shape=jax.ShapeDtypeStruct((b, s - tile0 * tm, d), F32),
        grid=(b, tiles - tile0),
        in_specs=stream_specs
                 + [_mod_spec(d, layer, tile0),
                    _layer_resident(lw['gain_mix'], layer), tok(MLSTM_WIDTH), tok(MLSTM_WIDTH), group_rows]
                 + [_layer_resident(w, layer) for w in weights],
        out_specs=pl.BlockSpec((1, tm, d), lambda i, t: (i, t, 0)),
        scratch_shapes=[pltpu.VMEM((S5_WIDTH // LANES, tm, LANES), F32)],
        compiler_params=_params("arbitrary", "arbitrary"),
        name="merge",
    )(*stream_args, mods, lw['gain_mix'], h_fwd, h_bwd, y_s5, *weights)


def _ffn_kernel(x_ref, mod_ref, gain_ref, wg_ref, wu_ref, wd_ref, gf_ref, o_ref, *, final):
    x = x_ref[0]
    mod = mod_ref[0, 0]
    h = _modulated_norm(x, gain_ref[...], mod[3:4], mod[4:5]).astype(BF16)
    a = jnp.dot(h, wg_ref[...], preferred_element_type=F32)
    b = jnp.dot(h, wu_ref[...], preferred_element_type=F32)
    hid = ((a * _sigmoid(a)) * b).astype(BF16)
    y = x + mod[5:6] * jnp.dot(hid, wd_ref[...], preferred_element_type=F32)
    if final:
        y = y * lax.rsqrt(jnp.mean(y * y, axis=-1, keepdims=True) + RMS_EPS) * gf_ref[...]
    o_ref[0] = y


def _ffn(xs, mods, lw, layer, g_final, tile0, final):
    b, s, d = xs.shape
    tm = TOKEN_TILE
    weights = tuple(lw[name] for name in ('gain_ffn', 'w_ffn_gate', 'w_ffn_up', 'w_ffn_down'))
    return pl.pallas_call(
        functools.partial(_ffn_kernel, final=final),
        out_shape=jax.ShapeDtypeStruct((b, s, d), F32),
        grid=(b, s // tm),
        in_specs=[pl.BlockSpec((1, tm, d), lambda i, t: (i, t, 0)), _mod_spec(d, layer, tile0)]
                 + [_layer_resident(w, layer) for w in weights] + [_resident((1, d))],
        out_specs=pl.BlockSpec((1, tm, d), lambda i, t: (i, t, 0)),
        compiler_params=_params("arbitrary", "arbitrary"),
        name="ffn_final" if final else "ffn",
    )(xs, mods, *weights, g_final)


def _stacked_weights(g_norm_mix, g_norm_ffn, w_in, b_gates, w_conv_qk, g_mh, g_sgu, w_sgu, b_sgu, w_glu,
                     b_glu, w_sconv, w_up_mlstm, w_up_sgu, w_up_s5, w_up_sconv, w_out,
                     w_ffn_gate, w_ffn_up, w_ffn_down):
    depth = w_in.shape[0]
    row = lambda a: a[:, None, :]
    gate_pad = ((0, 0), (0, 0), (0, LANES - 2 * HEADS))
    w_gates = jnp.concatenate([
        jnp.pad(w_in[:, :, GATE_COL:GATE_COL + 2 * HEADS], gate_pad),
        jnp.pad(w_in[:, :, GATE_COL + 2 * HEADS:GATE_COL + 4 * HEADS], gate_pad)], axis=2).astype(BF16)
    w_state = jnp.concatenate([
        w_in[:, :, :2 * MLSTM_WIDTH].astype(BF16),
        w_in[:, :, GATE_COL + 4 * HEADS:STATE_DIM].astype(BF16)], axis=2)
    b_pad = ((0, 0), (0, LANES - 2 * HEADS))
    bg = jnp.concatenate([jnp.pad(b_gates[:, :2 * HEADS], b_pad), jnp.pad(b_gates[:, 2 * HEADS:], b_pad)],
                         axis=1)
    return dict(
        gain_mix=row(g_norm_mix),
        gain_ffn=row(g_norm_ffn),
        w_state=w_state,
        w_v_t=jnp.swapaxes(w_in[:, :, 2 * MLSTM_WIDTH:GATE_COL], 1, 2).astype(BF16),
        w_gates_t=jnp.swapaxes(w_gates, 1, 2),
        b_gates_t=jnp.broadcast_to(bg[:, :, None], (depth, 2 * LANES, TOKEN_TILE)),
        w_conv=w_conv_qk.reshape(depth, 2 * CONV_K, MLSTM_WIDTH),
        w_out_side=w_in[:, :, STATE_DIM:].astype(BF16),
        g_mh=row(g_mh),
        g_sgu=row(g_sgu),
        w_sgu=w_sgu.astype(BF16),
        b_sgu_t=jnp.swapaxes(b_sgu, 1, 2),
        w_glu=w_glu.astype(BF16),
        b_glu=row(b_glu),
        w_sconv=w_sconv,
        w_up_mlstm=w_up_mlstm.astype(BF16),
        w_up_sgu=w_up_sgu.astype(BF16),
        w_up_s5=w_up_s5.astype(BF16),
        w_up_sconv=w_up_sconv.astype(BF16),
        w_out=w_out.astype(BF16),
        w_ffn_gate=w_ffn_gate.astype(BF16),
        w_ffn_up=w_ffn_up.astype(BF16),
        w_ffn_down=w_ffn_down.astype(BF16),
    )


def kernel(x, c, ctx, c_ctx, w_mod, b_mod, g_norm_mix, g_norm_ffn, w_in, b_gates, w_conv_qk, g_mh, g_sgu, w_sgu, b_sgu, s5_a_re, s5_a_im, s5_log_dt, s5_b_re, s5_b_im, s5_c_re, s5_c_im, s5_d, w_glu, b_glu, w_sconv, w_up_mlstm, w_up_sgu, w_up_s5, w_up_sconv, w_out, w_ffn_gate, w_ffn_up, w_ffn_down, g_final):
    batch, seq, d = x.shape
    ctx_len = ctx.shape[1]
    depth = w_mod.shape[0]
    assert d == D_MODEL and ctx_len == TOKEN_TILE and seq % TOKEN_TILE == 0 and batch == S5_BATCH
    ctx_tiles = ctx_len // TOKEN_TILE

    c_rows = jnp.zeros((8, d), F32).at[:batch].set(c).at[batch].set(c_ctx)
    mod_all = _modulation(c_rows, w_mod, b_mod).reshape(depth, 8, 6, d)
    stream = (ctx, x, ctx_tiles)
    stream_len = ctx_len + seq
    s5_ops = _s5_operators(s5_a_re, s5_a_im, s5_log_dt, s5_b_re, s5_b_im, s5_c_re, s5_c_im, s5_d)
    lw = _stacked_weights(g_norm_mix, g_norm_ffn, w_in, b_gates, w_conv_qk, g_mh, g_sgu, w_sgu, b_sgu,
                          w_glu, b_glu, w_sconv, w_up_mlstm, w_up_sgu, w_up_s5, w_up_sconv, w_out,
                          w_ffn_gate, w_ffn_up, w_ffn_down)

    mod_x = mod_all[:, :batch]
    mod_c = jnp.broadcast_to(mod_all[:, batch][:, None], mod_x.shape)
    mods = jnp.stack([mod_c, mod_x], axis=2)

    out = None
    for l in range(depth):
        last = l == depth - 1
        qt, k, vt, gates_t, u_rows = _state_proj(stream, stream_len, mods, lw, l, ctx_len)
        h_fwd, h_bwd = _mlstm(qt, k, vt, gates_t, ctx_len)
        y_s5 = _s5_mixer(u_rows, s5_ops, l, stream_len // S5_CHUNK, ctx_len // S5_CHUNK)
        tile0 = ctx_tiles if last else 0
        x_mid = _merge(stream, stream_len, mods, h_fwd, h_bwd, y_s5, lw, l, ctx_len, tile0)
        out = _ffn(x_mid, mods, lw, l, g_final.reshape(1, d), tile0, last)
        stream = (out, out, 0)
    return out
```

```python
import functools
import math

import jax
import jax.numpy as jnp
from jax import lax
from jax.experimental import pallas as pl
from jax.experimental.pallas import tpu as pltpu

F32 = jnp.float32
BF16 = jnp.bfloat16

D_MODEL = 1024
GRID_W = 64
N_BRANCH = 4
RMS_EPS = 1e-6
CONV_K = 3
HEADS = 4
HEAD_DIM = 128
MLSTM_WIDTH = HEADS * HEAD_DIM
MLSTM_CHUNK = 128
MLSTM_BATCH = 4
SGU_GROUPS = 4
SGU_GROUP_DIM = 128
SGU_WIDTH = SGU_GROUPS * SGU_GROUP_DIM
SGU_CHUNK = 128
S5_GROUP_DIM = 16
S5_GROUPS = 24
S5_WIDTH = S5_GROUPS * S5_GROUP_DIM
S5_STATE = 64
SCONV_WIDTH = 512
STATE_DIM = 3 * MLSTM_WIDTH + 4 * HEADS + S5_WIDTH
GATE_COL = 3 * MLSTM_WIDTH

LANES = 128
TOKEN_TILE = 256
S5_CHUNK = 16
S5_ROW = S5_CHUNK * S5_GROUP_DIM
S5_PAIR = 2 * S5_STATE
S5_BATCH = 4
U_COL = 2 * MLSTM_WIDTH
VMEM_LIMIT = 52 * 1024 * 1024
CONTRACT_LAST = (((1,), (1,)), ((), ()))


def _resident(shape):
    zeros = (0,) * len(shape)
    return pl.BlockSpec(shape, lambda *_: zeros, pipeline_mode=pl.Buffered(1))


def _layer_resident(stacked, layer):
    zeros = (0,) * (stacked.ndim - 1)
    return pl.BlockSpec((None,) + stacked.shape[1:], lambda *_: (layer,) + zeros,
                        pipeline_mode=pl.Buffered(1))


def _row_block(stacked, layer, start, size):
    cols = stacked.shape[-1]
    return pl.BlockSpec((pl.Element(1), pl.Element(size), pl.Element(cols)), lambda *_: (layer, start, 0),
                        pipeline_mode=pl.Buffered(1))


def _mod_spec(d, layer, tile0):
    return pl.BlockSpec((None, 1, 1, 6, d), lambda i, t: (layer, i, jnp.minimum(t + tile0, 1), 0, 0))


def _params(*semantics, flags=None):
    return pltpu.CompilerParams(dimension_semantics=semantics, vmem_limit_bytes=VMEM_LIMIT, flags=flags)


def _sigmoid(x):
    return 0.5 * jnp.tanh(0.5 * x) + 0.5


def _modulated_norm(x, gain, shift, scale):
    y = x * lax.rsqrt(jnp.mean(x * x, axis=-1, keepdims=True) + RMS_EPS) * gain
    return y * (1.0 + scale) + shift


def _conv3(a, w, first, last):
    n = a.shape[0]
    prev = jnp.where(first, 0.0, pltpu.roll(a, 1, 0))
    nxt = jnp.where(last, 0.0, pltpu.roll(a, n - 1, 0))
    return prev * w[0:1] + a * w[1:2] + nxt * w[2:3]


def _row_edges(tile_idx, n_rows, ctx_len):
    period = jnp.where(tile_idx == 0, ctx_len, GRID_W)
    pos = lax.broadcasted_iota(jnp.int32, (n_rows, 1), 0) & (period - 1)
    return pos == 0, pos == period - 1


def _mod_kernel(c_ref, w_ref, b_ref, o_ref):
    a = c_ref[...]
    a = a * _sigmoid(a)
    a_hi = a.astype(BF16)
    a_lo = (a - a_hi.astype(F32)).astype(BF16)
    w = w_ref[0]
    w_hi = w.astype(BF16)
    w_lo = (w - w_hi.astype(F32)).astype(BF16)
    acc = jnp.dot(a_hi, w_hi, preferred_element_type=F32)
    acc = acc + jnp.dot(a_lo, w_hi, preferred_element_type=F32)
    acc = acc + jnp.dot(a_hi, w_lo, preferred_element_type=F32)
    o_ref[0] = acc + b_ref[0]


def _modulation(c_rows, w_mod, b_mod):
    depth, d, n = w_mod.shape
    tn = 1536
    return pl.pallas_call(
        _mod_kernel,
        out_shape=jax.ShapeDtypeStruct((depth, 8, n), F32),
        grid=(depth, n // tn),
        in_specs=[
            pl.BlockSpec((8, d), lambda l, j: (0, 0)),
            pl.BlockSpec((1, d, tn), lambda l, j: (l, 0, j)),
            pl.BlockSpec((1, 1, tn), lambda l, j: (l, 0, j)),
        ],
        out_specs=pl.BlockSpec((1, 8, tn), lambda l, j: (l, 0, j)),
        compiler_params=_params("arbitrary", "arbitrary"),
        name="modulation",
    )(c_rows, w_mod, b_mod.reshape(depth, 1, n))


def _to_group_rows(z, u_ref, buf_ref):
    n_chunks = z.shape[0] // S5_CHUNK
    per_block = LANES // S5_GROUP_DIM
    for gb in range(S5_WIDTH // LANES):
        buf_ref[gb] = z[:, gb * LANES:(gb + 1) * LANES]
    for s in range(S5_CHUNK):
        for gb in range(S5_WIDTH // LANES):
            zs = buf_ref[gb, pl.ds(s, n_chunks, stride=S5_CHUNK), :]
            for gl in range(per_block):
                u_ref[gb * per_block + gl, :, s * S5_GROUP_DIM:(s + 1) * S5_GROUP_DIM] = (
                    zs[:, gl * S5_GROUP_DIM:(gl + 1) * S5_GROUP_DIM])


def _from_group_rows(y_ref, buf_ref):
    n_chunks = y_ref.shape[1]
    per_block = LANES // S5_GROUP_DIM
    for s in range(S5_CHUNK):
        for gb in range(S5_WIDTH // LANES):
            pieces = [y_ref[gb * per_block + gl, :, s * S5_GROUP_DIM:(s + 1) * S5_GROUP_DIM]
                      for gl in range(per_block)]
            buf_ref[gb, pl.ds(s, n_chunks, stride=S5_CHUNK), :] = jnp.concatenate(pieces, axis=-1)
    return jnp.concatenate([buf_ref[gb] for gb in range(S5_WIDTH // LANES)], axis=-1)


def _stream_specs(stream, tile0):
    ctx_src, lat_src, offset = stream
    d = ctx_src.shape[-1]
    specs = [pl.BlockSpec((1, TOKEN_TILE, d), lambda i, t: (i, 0, 0)),
             pl.BlockSpec((1, TOKEN_TILE, d), lambda i, t: (i, jnp.maximum(t + tile0 - offset, 0), 0))]
    return specs, (ctx_src, lat_src)


def _stream_tile(t, ctx_ref, lat_ref):
    return jnp.where(t == 0, ctx_ref[0], lat_ref[0])


def _state_proj_kernel(xc_ref, xl_ref, mod_ref, gain_ref, wqk_ref, wv_ref, wg_ref, wu_ref, bgt_ref, wc_ref,
                       eye_ref, hn_ref, qt_ref, k_ref, vt_ref, gt_ref, u_ref, ubuf_ref, *, ctx_len):
    t = pl.program_id(1)
    mod = mod_ref[0, 0]
    h = _modulated_norm(_stream_tile(t, xc_ref, xl_ref), gain_ref[...], mod[0:1], mod[1:2]).astype(BF16)
    hn_ref[0] = h
    z_qk = lax.dot_general(h, wqk_ref[0], CONTRACT_LAST, preferred_element_type=F32)
    z_u = lax.dot_general(h, wu_ref[0], CONTRACT_LAST, preferred_element_type=F32)
    vt_ref[0] = lax.dot_general(wv_ref[0], h, CONTRACT_LAST, preferred_element_type=F32).astype(BF16)
    gt_ref[0] = lax.dot_general(wg_ref[0], h, CONTRACT_LAST, preferred_element_type=F32) + bgt_ref[...]
    first, last = _row_edges(t, z_qk.shape[0], ctx_len)
    wc = wc_ref[...]
    w = MLSTM_WIDTH
    q = _conv3(z_qk[:, 0:w], wc[0:3], first, last)
    k = _conv3(z_qk[:, w:2 * w], wc[3:6], first, last)
    q = (q * _sigmoid(q)).astype(BF16)
    qt_ref[0] = lax.dot_general(eye_ref[...], q, CONTRACT_LAST, preferred_element_type=F32).astype(BF16)
    k_ref[0] = ((k * _sigmoid(k)) * (HEAD_DIM ** -0.5)).astype(BF16)
    _to_group_rows(z_u, u_ref, ubuf_ref)


def _state_proj(stream, s, mods, lw, layer, ctx_len):
    b, _, d = stream[0].shape
    tm = TOKEN_TILE
    tiles = s // tm
    stream_specs, stream_args = _stream_specs(stream, 0)
    tok = lambda width: pl.BlockSpec((1, tm, width), lambda i, t: (i, t, 0))
    chan = lambda height: pl.BlockSpec((1, height, tm), lambda i, t: (i, 0, t))
    w_t = lw['w_in_t']
    column_ranges = ((0, 2 * MLSTM_WIDTH), (2 * MLSTM_WIDTH, MLSTM_WIDTH), (GATE_COL, 4 * HEADS),
                     (GATE_COL + 4 * HEADS, S5_WIDTH))
    eye = jnp.eye(MLSTM_WIDTH, dtype=BF16)
    return pl.pallas_call(
        functools.partial(_state_proj_kernel, ctx_len=ctx_len),
        out_shape=(
            jax.ShapeDtypeStruct((b, s, d), BF16),
            jax.ShapeDtypeStruct((b, MLSTM_WIDTH, s), BF16),
            jax.ShapeDtypeStruct((b, s, MLSTM_WIDTH), BF16),
            jax.ShapeDtypeStruct((b, MLSTM_WIDTH, s), BF16),
            jax.ShapeDtypeStruct((b, 4 * HEADS, s), F32),
            jax.ShapeDtypeStruct((S5_GROUPS, b * s // S5_CHUNK, S5_ROW), F32),
        ),
        grid=(b, s // tm),
        in_specs=stream_specs + [_mod_spec(d, layer, 0), _layer_resident(lw['gain_mix'], layer)]
                 + [_row_block(w_t, layer, start, size) for start, size in column_ranges]
                 + [_layer_resident(lw['b_gates_t'], layer), _layer_resident(lw['w_conv'], layer),
                    _resident(eye.shape)],
        out_specs=(tok(d), chan(MLSTM_WIDTH), tok(MLSTM_WIDTH), chan(MLSTM_WIDTH), chan(4 * HEADS),
                   pl.BlockSpec((S5_GROUPS, tm // S5_CHUNK, S5_ROW), lambda i, t: (0, i * tiles + t, 0))),
        scratch_shapes=[pltpu.VMEM((S5_WIDTH // LANES, tm, LANES), F32)],
        compiler_params=_params("arbitrary", "arbitrary"),
        name="state_proj",
    )(*stream_args, mods, lw['gain_mix'], w_t, w_t, w_t, w_t, lw['b_gates_t'], lw['w_conv'], eye)


def _mlstm_chunk(direction, bi, qt_ref, k_ref, vt_ref, gt_ref, h_ref, ct_ref, n_ref, m_ref):
    lc = MLSTM_CHUNK
    sign = 1 - 2 * direction
    row = lax.broadcasted_iota(jnp.int32, (lc, lc), 0)
    col = lax.broadcasted_iota(jnp.int32, (lc, lc), 1)
    tri = (col - row) * sign >= 0
    tri_b = jnp.where(tri, 1.0, 0.0).astype(BF16)

    gt = gt_ref[bi]
    lf_t = jax.nn.log_sigmoid(gt)
    hi = lf_t.astype(BF16)
    lo = (lf_t - hi.astype(F32)).astype(BF16)
    bcum_t = (jnp.dot(hi, tri_b, preferred_element_type=F32)
              + jnp.dot(lo, tri_b, preferred_element_type=F32))
    last = lc - 1 if direction == 0 else 0

    qt = qt_ref[bi]
    k = k_ref[bi]
    vt = vt_ref[bi]
    heads = []
    for hd in range(HEADS):
        sl = slice(hd * HEAD_DIM, (hd + 1) * HEAD_DIM)
        b_row = bcum_t[HEADS + hd:HEADS + hd + 1, :]
        heads.append(dict(
            bi=bi, sl=sl, qt=qt[sl, :], k=k[:, sl], vt=vt[sl, :], tri=tri,
            a_row=gt[hd:hd + 1, :] - b_row, b_row=b_row, b_last=b_row[:, last:last + 1],
            m=m_ref[hd][:, 0:1], ct=ct_ref[hd][...], n=n_ref[hd][...],
            h_ref=h_ref, ct_ref=ct_ref[hd], n_ref=n_ref[hd], m_ref=m_ref[hd]))
    return heads


def _mlstm_scores(hd):
    hd['kq'] = jnp.dot(hd['k'], hd['qt'], preferred_element_type=F32)
    n_rows = jnp.broadcast_to(hd['n'], (8, HEAD_DIM)).astype(BF16)
    hd['qn'] = jnp.dot(n_rows, hd['qt'], preferred_element_type=F32)[0:1]
    hd['cq'] = jnp.dot(hd['ct'].astype(BF16), hd['qt'], preferred_element_type=F32)


def _mlstm_weights(hd):
    lc = MLSTM_CHUNK
    a_row, b_row, b_last, m = hd['a_row'], hd['b_row'], hd['b_last'], hd['m']
    m_new = b_last + jnp.maximum(m, jnp.max(a_row, axis=1, keepdims=True))
    w_row = jnp.exp(a_row + (b_last - m_new))
    hd['decay'] = jnp.exp(b_last + m - m_new)
    hd['m_new'] = m_new
    hd['vt_w'] = (hd['vt'].astype(F32) * w_row).astype(BF16)
    hd['w_rows'] = jnp.broadcast_to(w_row, (8, lc)).astype(BF16)

    a_bc = jnp.broadcast_to(a_row, (lc, lc)).T
    log_w = jnp.where(hd['tri'], a_bc + b_row, -jnp.inf)
    log_inter = b_row + m
    m_t = jnp.maximum(log_inter, jnp.max(log_w, axis=0, keepdims=True))
    hd['inter'] = jnp.exp(log_inter - m_t)
    s_t = hd['kq'] * jnp.exp(log_w - m_t)
    den = hd['inter'] * hd['qn'] + jnp.sum(s_t, axis=0, keepdims=True)
    hd['r_den'] = 1.0 / jnp.maximum(jnp.abs(den), jnp.exp(-m_t))
    hd['s_t'] = s_t.astype(BF16)


def _mlstm_outputs(hd):
    num = jnp.dot(hd['vt'], hd['s_t'], preferred_element_type=F32) + hd['inter'] * hd['cq']
    hd['h_ref'][hd['bi'], :, hd['sl']] = (num * hd['r_den']).T
    hd['ct_ref'][...] = hd['decay'] * hd['ct'] + jnp.dot(hd['vt_w'], hd['k'], preferred_element_type=F32)
    hd['n_ref'][...] = (hd['decay'] * hd['n']
                        + jnp.dot(hd['w_rows'], hd['k'], preferred_element_type=F32)[0:1])
    hd['m_ref'][...] = jnp.broadcast_to(hd['m_new'], (1, LANES))


def _mlstm_kernel(qtf_ref, kf_ref, vtf_ref, gtf_ref, qtb_ref, kb_ref, vtb_ref, gtb_ref,
                  hf_ref, hb_ref, *state_refs):
    per = MLSTM_BATCH * 2 * HEADS
    ct_refs, n_refs, m_refs = state_refs[:per], state_refs[per:2 * per], state_refs[2 * per:]

    @pl.when(pl.program_id(1) == 0)
    def _():
        for ref in state_refs:
            ref[...] = jnp.zeros_like(ref)

    heads = []
    for bi in range(MLSTM_BATCH):
        for direction, refs in enumerate(((qtf_ref, kf_ref, vtf_ref, gtf_ref, hf_ref),
                                          (qtb_ref, kb_ref, vtb_ref, gtb_ref, hb_ref))):
            own = slice((2 * bi + direction) * HEADS, (2 * bi + direction + 1) * HEADS)
            heads += _mlstm_chunk(direction, bi, *refs, ct_refs[own], n_refs[own], m_refs[own])
    for phase in (_mlstm_scores, _mlstm_weights, _mlstm_outputs):
        for hd in heads:
            phase(hd)


def _mlstm(qt, k, vt, gates_t, ctx_len):
    b, s, w = k.shape
    lc = MLSTM_CHUNK
    n_chunks = s // lc
    n_ctx = ctx_len // lc

    def back(j):
        return jnp.where(j < n_ctx, n_ctx - 1 - j, n_chunks + n_ctx - 1 - j)

    nb = MLSTM_BATCH

    def specs(direction, chunk):
        return [pl.BlockSpec((nb, w, lc), lambda i, j: (i, 0, chunk(j))),
                pl.BlockSpec((nb, lc, w), lambda i, j: (i, chunk(j), 0)),
                pl.BlockSpec((nb, w, lc), lambda i, j: (i, 0, chunk(j))),
                pl.BlockSpec((nb, 2 * HEADS, lc), lambda i, j: (i, direction, chunk(j)))]

    fwd = lambda j: j
    out = jax.ShapeDtypeStruct((b, s, w), F32)
    chains = nb * 2 * HEADS
    return pl.pallas_call(
        _mlstm_kernel,
        out_shape=(out, out),
        grid=(b // nb, n_chunks),
        in_specs=specs(0, fwd) + specs(1, back),
        out_specs=(pl.BlockSpec((nb, lc, w), lambda i, j: (i, j, 0)),
                   pl.BlockSpec((nb, lc, w), lambda i, j: (i, back(j), 0))),
        scratch_shapes=([pltpu.VMEM((HEAD_DIM, HEAD_DIM), F32)] * chains
                        + [pltpu.VMEM((1, HEAD_DIM), F32)] * chains
                        + [pltpu.VMEM((1, LANES), F32)] * chains),
        compiler_params=_params("arbitrary", "arbitrary"),
        name="mlstm",
    )(qt, k, vt, gates_t, qt, k, vt, gates_t)


def _s5_operators(a_re, a_im, log_dt, b_re, b_im, c_re, c_im, d_skip):
    hp = lax.Precision.HIGH
    lch = S5_CHUNK
    depth = a_re.shape[0]
    a_re = a_re.astype(F32)
    a_im = a_im.astype(F32)
    dt = jnp.exp(log_dt.astype(F32))[..., None]
    la_re, la_im = dt * a_re, dt * a_im
    mag = jnp.exp(la_re)
    ab_re, ab_im = mag * jnp.cos(la_im), mag * jnp.sin(la_im)
    nr, ni = ab_re - 1.0, ab_im
    den = a_re * a_re + a_im * a_im
    f_re = ((nr * a_re + ni * a_im) / den)[:, :, :, None, :]
    f_im = ((ni * a_re - nr * a_im) / den)[:, :, :, None, :]
    bt_re = jnp.swapaxes(b_re.astype(F32), -1, -2)[:, None]
    bt_im = jnp.swapaxes(b_im.astype(F32), -1, -2)[:, None]
    bb_re = f_re * bt_re - f_im * bt_im
    bb_im = f_re * bt_im + f_im * bt_re
    c_re = c_re.astype(F32)
    c_im = c_im.astype(F32)

    n = jnp.arange(-(lch - 1), lch + 1, dtype=F32)[:, None]
    mg = jnp.exp(n * la_re[:, :, :, None, :])
    ang = n * la_im[:, :, :, None, :]
    pw_re, pw_im = mg * jnp.cos(ang), mg * jnp.sin(ang)
    zero = lch - 1

    def powers(direction, first, step):
        start = zero + first
        stop = start + step * lch
        sl = slice(start, stop if stop >= 0 else None, step)
        return pw_re[:, direction, :, sl, None, :], pw_im[:, direction, :, sl, None, :]

    def times(x_re, x_im, p):
        return x_re * p[0] - x_im * p[1], x_re * p[1] + x_im * p[0]

    def rows256(x):
        return x.reshape(depth, S5_GROUPS, S5_ROW, S5_STATE)

    bbf = (bb_re[:, 0, :, None], bb_im[:, 0, :, None])
    bbb = (bb_re[:, 1, :, None], bb_im[:, 1, :, None])
    cc = (c_re[:, :, None], c_im[:, :, None])

    def response(bb, direction, sign):
        l_re, l_im = times(*bb, powers(direction, 0, -sign))
        r_re, r_im = times(*cc, powers(direction, 0, sign))
        left = jnp.concatenate([rows256(l_re), rows256(l_im)], axis=-1)
        right = jnp.concatenate([rows256(r_re), -rows256(r_im)], axis=-1)
        return jnp.einsum('lgap,lgbp->lgab', left, right, precision=hp)

    src = (jnp.arange(S5_ROW) // S5_GROUP_DIM)[:, None]
    tgt = (jnp.arange(S5_ROW) // S5_GROUP_DIM)[None, :]
    d_rows = jnp.tile(d_skip.astype(F32).reshape(depth, S5_GROUPS, 1, S5_GROUP_DIM), (1, 1, 1, lch))
    toeplitz = (jnp.where(tgt >= src, response(bbf, 0, 1), 0.0)
                + jnp.where(src >= tgt, response(bbb, 1, -1), 0.0)
                + jnp.eye(S5_ROW, dtype=F32) * d_rows)

    def pair_block(x_re, x_im):
        x_re = x_re.reshape(depth, S5_GROUPS // 2, 2, S5_ROW, S5_STATE)
        x_im = x_im.reshape(depth, S5_GROUPS // 2, 2, S5_ROW, S5_STATE)
        z = jnp.zeros_like(x_re[:, :, 0])
        top = jnp.concatenate([x_re[:, :, 0], z, x_im[:, :, 0], z], axis=-1)
        bottom = jnp.concatenate([z, x_re[:, :, 1], z, x_im[:, :, 1]], axis=-1)
        return jnp.concatenate([top, bottom], axis=-2)

    def state_map(x, p):
        re, im = times(*x, p)
        return rows256(re), rows256(im)

    w_f = pair_block(*state_map(bbf, powers(0, lch - 1, -1)))
    w_b = pair_block(*state_map(bbb, powers(1, 0, 1)))
    w_pair = jnp.concatenate([w_f, w_b], axis=-1)
    vf_re, vf_im = state_map(cc, powers(0, 1, 1))
    vb_re, vb_im = state_map(cc, powers(1, lch, -1))
    vt_pair = jnp.stack([pair_block(vf_re, -vf_im), pair_block(vb_re, -vb_im)], axis=1)

    al_re, al_im = pw_re[:, :, :, zero + lch], pw_im[:, :, :, zero + lch]
    def rows(a):
        a = a.reshape(depth, 2, 1, S5_GROUPS * S5_STATE)
        return jnp.broadcast_to(a, (depth, 2, S5_BATCH, S5_GROUPS * S5_STATE)).reshape(depth, 2 * S5_BATCH, -1)
    a_step = jnp.stack([rows(al_re), rows(al_im)], axis=1)
    return toeplitz.astype(BF16), w_pair.astype(BF16), vt_pair.astype(BF16), a_step


def _s5_chunk_state_kernel(u_ref, w_ref, x_ref, *, n_chunks):
    u = jnp.concatenate([u_ref[0], u_ref[1]], axis=-1).astype(BF16)
    x = jnp.dot(u, w_ref[0], preferred_element_type=F32)
    for b in range(S5_BATCH):
        xb = x[b * n_chunks:(b + 1) * n_chunks]
        for direction in range(2):
            for part in range(2):
                lo = (2 * direction + part) * S5_PAIR
                x_ref[part, pl.ds(direction * S5_BATCH + b, n_chunks, stride=2 * S5_BATCH), :] = (
                    xb[:, lo:lo + S5_PAIR])


def _s5_scan_kernel(x_ref, a_ref, of_ref, ob_ref, *, n_chunks, n_ctx):
    tile = 2 * S5_BATCH
    n_slabs = x_ref.shape[0]
    a_re = a_ref[0]
    a_im = a_ref[1]
    fwd_rows = lax.broadcasted_iota(jnp.int32, (tile, S5_PAIR), 0) < S5_BATCH

    def body(i, state):
        cb = jnp.where(i < n_ctx, n_ctx - 1 - i, n_chunks + n_ctx - 1 - i)
        rf = pl.multiple_of(i * tile, tile)
        rb = pl.multiple_of(cb * tile, tile)
        new_state = []
        for p in range(n_slabs // 2):
            re, im = state[2 * p], state[2 * p + 1]
            for j, val in ((2 * p, re), (2 * p + 1, im)):
                of_ref[j, pl.ds(rf, tile), :] = val
                ob_ref[j, pl.ds(rb, tile), :] = val
            in_re = jnp.where(fwd_rows, x_ref[2 * p, pl.ds(rf, tile), :], x_ref[2 * p, pl.ds(rb, tile), :])
            in_im = jnp.where(fwd_rows, x_ref[2 * p + 1, pl.ds(rf, tile), :],
                              x_ref[2 * p + 1, pl.ds(rb, tile), :])
            ar, ai = a_re[:, p * S5_PAIR:(p + 1) * S5_PAIR], a_im[:, p * S5_PAIR:(p + 1) * S5_PAIR]
            new_state.append(ar * re - ai * im + in_re)
            new_state.append(ar * im + ai * re + in_im)
        return tuple(new_state)

    zero = jnp.zeros((tile, S5_PAIR), F32)
    lax.fori_loop(0, n_chunks, body, (zero,) * n_slabs, unroll=4)


def _s5_output_kernel(u_ref, t_ref, xf_ref, xb_ref, v_ref, y_ref, *, n_chunks):
    stride = 2 * S5_BATCH

    def batch_rows(ref, first_row):
        return jnp.concatenate(
            [jnp.concatenate([ref[part, pl.ds(first_row + b, n_chunks, stride=stride), :]
                              for part in range(2)], axis=-1) for b in range(S5_BATCH)], axis=0)

    xf = batch_rows(xf_ref, 0)
    xb = batch_rows(xb_ref, S5_BATCH)
    last = (((1,), (1,)), ((), ()))
    y = lax.dot_general(xf.astype(BF16), v_ref[0, 0], last, preferred_element_type=F32)
    y = y + lax.dot_general(xb.astype(BF16), v_ref[1, 0], last, preferred_element_type=F32)
    for a in range(2):
        within = jnp.dot(u_ref[a].astype(BF16), t_ref[a], preferred_element_type=F32)
        y_ref[a] = within + y[:, a * S5_ROW:(a + 1) * S5_ROW]


def _s5_pair_kernel(u_ref, w_ref, t_ref, v_ref, a_ref, y_ref, xc_ref, xf_ref, xb_ref, *, n_chunks, n_ctx):
    _s5_chunk_state_kernel(u_ref, w_ref, xc_ref, n_chunks=n_chunks)
    _s5_scan_kernel(xc_ref, a_ref, xf_ref, xb_ref, n_chunks=n_chunks, n_ctx=n_ctx)
    _s5_output_kernel(u_ref, t_ref, xf_ref, xb_ref, v_ref, y_ref, n_chunks=n_chunks)


def _s5_mixer(u_rows, ops, layer, n_chunks, n_ctx):
    toeplitz, w_pair, vt_pair, a_step = ops
    g, rows, _ = u_rows.shape
    n_pairs = g // 2
    states = pltpu.VMEM((2, n_chunks * 2 * S5_BATCH, S5_PAIR), F32)
    return pl.pallas_call(
        functools.partial(_s5_pair_kernel, n_chunks=n_chunks, n_ctx=n_ctx),
        out_shape=jax.ShapeDtypeStruct((g, rows, S5_ROW), F32),
        grid=(n_pairs,),
        in_specs=[pl.BlockSpec((2, rows, S5_ROW), lambda i: (i, 0, 0)),
                  pl.BlockSpec((None, 1, 2 * S5_ROW, 4 * S5_PAIR), lambda i: (layer, i, 0, 0)),
                  pl.BlockSpec((None, 2, S5_ROW, S5_ROW), lambda i: (layer, i, 0, 0)),
                  pl.BlockSpec((None, 2, 1, 2 * S5_ROW, 2 * S5_PAIR), lambda i: (layer, 0, i, 0, 0)),
                  pl.BlockSpec((None, 2, 2 * S5_BATCH, S5_PAIR), lambda i: (layer, 0, 0, i))],
        out_specs=pl.BlockSpec((2, rows, S5_ROW), lambda i: (i, 0, 0)),
        scratch_shapes=[states, states, states],
        compiler_params=_params("arbitrary"),
        name="s5_pair",
    )(u_rows, w_pair, toeplitz, vt_pair, a_step)


def _merge_kernel(xc_ref, xl_ref, mod_ref, hn_ref, hf_ref, hb_ref, ys_ref, wo_ref, gmh_ref, gsgu_ref,
                  wsgu_ref, bsgu_ref, wglu_ref, bglu_ref, wsc_ref, wum_ref, wug_ref, wus_ref,
                  wuc_ref, wout_ref, o_ref, ybuf_ref, *, ctx_len, tile0):
    t = pl.program_id(1) + tile0
    mod = mod_ref[0, 0]
    h = hn_ref[0]
    tm = h.shape[0]

    def proj(start, width):
        return lax.dot_general(h, wo_ref[0, start:start + width, :], CONTRACT_LAST,
                               preferred_element_type=F32)

    w = MLSTM_WIDTH
    z_all = proj(0, 6 * w)
    z_o, z_su, z_sv, z_cb, z_cc, z_cx = (z_all[:, i * w:(i + 1) * w] for i in range(6))
    ys = jax.nn.gelu(_from_group_rows(ys_ref, ybuf_ref))
    z_glu = jnp.dot(ys.astype(BF16), wglu_ref[...], preferred_element_type=F32)

    hm = hf_ref[0] + hb_ref[0]
    parts = []
    for hd in range(HEADS):
        hh = hm[:, hd * HEAD_DIM:(hd + 1) * HEAD_DIM]
        parts.append(hh * lax.rsqrt(jnp.mean(hh * hh, axis=-1, keepdims=True) + RMS_EPS))
    y_a = (jnp.concatenate(parts, axis=-1) * gmh_ref[...]) * _sigmoid(z_o)

    su = jax.nn.gelu(z_su)
    sv = jax.nn.gelu(z_sv)
    mu = jnp.mean(sv, axis=-1, keepdims=True)
    cen = sv - mu
    var = jnp.mean(cen * cen, axis=-1, keepdims=True)
    vn = (cen * lax.rsqrt(var + RMS_EPS) * gsgu_ref[...]).astype(BF16)
    bias = bsgu_ref[...]
    rows = []
    for n in range(tm // SGU_CHUNK):
        cols = []
        for gi in range(SGU_GROUPS):
            blk = vn[n * SGU_CHUNK:(n + 1) * SGU_CHUNK, gi * SGU_GROUP_DIM:(gi + 1) * SGU_GROUP_DIM]
            cols.append(jnp.dot(wsgu_ref[gi], blk, preferred_element_type=F32) + bias[:, gi:gi + 1])
        rows.append(jnp.concatenate(cols, axis=-1))
    y_b = su * jnp.concatenate(rows, axis=0)

    y_c = ys * _sigmoid(z_glu + bglu_ref[...])

    first, last = _row_edges(t, tm, ctx_len)
    y_d = z_cb * _conv3(z_cc * z_cx, wsc_ref[...], first, last)

    acc = None
    for j, (y, up_ref) in enumerate(zip((y_a, y_b, y_c, y_d), (wum_ref, wug_ref, wus_ref, wuc_ref))):
        gate = proj(6 * w + j * D_MODEL, D_MODEL)
        term = _sigmoid(gate) * jnp.dot(y.astype(BF16), up_ref[...], preferred_element_type=F32)
        acc = term if acc is None else acc + term
    mixed = jnp.dot(acc.astype(BF16), wout_ref[...], preferred_element_type=F32)
    o_ref[0] = _stream_tile(t, xc_ref, xl_ref) + mod[2:3] * mixed


def _merge(stream, s, mods, h_norm, h_fwd, h_bwd, y_s5, lw, layer, ctx_len, tile0):
    b, _, d = stream[0].shape
    tm = TOKEN_TILE
    tiles = s // tm
    stream_specs, stream_args = _stream_specs(stream, tile0)
    tok = lambda width: pl.BlockSpec((1, tm, width), lambda i, t: (i, t + tile0, 0))
    group_rows = pl.BlockSpec((S5_GROUPS, tm // S5_CHUNK, S5_ROW),
                              lambda i, t: (0, i * tiles + t + tile0, 0))
    weights = tuple(lw[name] for name in (
        'g_mh', 'g_sgu', 'w_sgu', 'b_sgu_t', 'w_glu', 'b_glu', 'w_sconv',
        'w_up_mlstm', 'w_up_sgu', 'w_up_s5', 'w_up_sconv', 'w_out'))
    w_t = lw['w_in_t']
    out_side = _row_block(w_t, layer, STATE_DIM, w_t.shape[1] - STATE_DIM)
    return pl.pallas_call(
        functools.partial(_merge_kernel, ctx_len=ctx_len, tile0=tile0),
        out_shape=jax.ShapeDtypeStruct((b, s - tile0 * tm, d), F32),
        grid=(b, tiles - tile0),
        in_specs=stream_specs
                 + [_mod_spec(d, layer, tile0), tok(d), tok(MLSTM_WIDTH), tok(MLSTM_WIDTH), group_rows, out_side]
                 + [_layer_resident(w, layer) for w in weights],
        out_specs=pl.BlockSpec((1, tm, d), lambda i, t: (i, t, 0)),
        scratch_shapes=[pltpu.VMEM((S5_WIDTH // LANES, tm, LANES), F32)],
        compiler_params=_params("arbitrary", "arbitrary"),
        name="merge",
    )(*stream_args, mods, h_norm, h_fwd, h_bwd, y_s5, w_t, *weights)


def _ffn_kernel(x_ref, mod_ref, gain_ref, wg_ref, wu_ref, wd_ref, gf_ref, o_ref, *, final):
    x = x_ref[0]
    mod = mod_ref[0, 0]
    h = _modulated_norm(x, gain_ref[...], mod[3:4], mod[4:5]).astype(BF16)
    a = jnp.dot(h, wg_ref[...], preferred_element_type=F32)
    b = jnp.dot(h, wu_ref[...], preferred_element_type=F32)
    hid = ((a * _sigmoid(a)) * b).astype(BF16)
    y = x + mod[5:6] * jnp.dot(hid, wd_ref[...], preferred_element_type=F32)
    if final:
        y = y * lax.rsqrt(jnp.mean(y * y, axis=-1, keepdims=True) + RMS_EPS) * gf_ref[...]
    o_ref[0] = y


def _ffn(xs, mods, lw, layer, g_final, tile0, final):
    b, s, d = xs.shape
    tm = 2 * TOKEN_TILE if tile0 > 0 and s % (2 * TOKEN_TILE) == 0 else TOKEN_TILE
    weights = tuple(lw[name] for name in ('gain_ffn', 'w_ffn_gate', 'w_ffn_up', 'w_ffn_down'))
    return pl.pallas_call(
        functools.partial(_ffn_kernel, final=final),
        out_shape=jax.ShapeDtypeStruct((b, s, d), F32),
        grid=(b, s // tm),
        in_specs=[pl.BlockSpec((1, tm, d), lambda i, t: (i, t, 0)), _mod_spec(d, layer, tile0)]
                 + [_layer_resident(w, layer) for w in weights] + [_resident((1, d))],
        out_specs=pl.BlockSpec((1, tm, d), lambda i, t: (i, t, 0)),
        compiler_params=_params("arbitrary", "arbitrary"),
        name="ffn_final" if final else "ffn",
    )(xs, mods, *weights, g_final)


def _stacked_weights(g_norm_mix, g_norm_ffn, w_in, b_gates, w_conv_qk, g_mh, g_sgu, w_sgu, b_sgu, w_glu,
                     b_glu, w_sconv, w_up_mlstm, w_up_sgu, w_up_s5, w_up_sconv, w_out,
                     w_ffn_gate, w_ffn_up, w_ffn_down):
    depth = w_in.shape[0]
    row = lambda a: a[:, None, :]
    return dict(
        gain_mix=row(g_norm_mix),
        gain_ffn=row(g_norm_ffn),
        w_in_t=jnp.swapaxes(w_in, 1, 2).astype(BF16),
        b_gates_t=jnp.broadcast_to(b_gates[:, :, None], (depth, 4 * HEADS, TOKEN_TILE)),
        w_conv=w_conv_qk.reshape(depth, 2 * CONV_K, MLSTM_WIDTH),
        g_mh=row(g_mh),
        g_sgu=row(g_sgu),
        w_sgu=w_sgu.astype(BF16),
        b_sgu_t=jnp.swapaxes(b_sgu, 1, 2),
        w_glu=w_glu.astype(BF16),
        b_glu=row(b_glu),
        w_sconv=w_sconv,
        w_up_mlstm=w_up_mlstm.astype(BF16),
        w_up_sgu=w_up_sgu.astype(BF16),
        w_up_s5=w_up_s5.astype(BF16),
        w_up_sconv=w_up_sconv.astype(BF16),
        w_out=w_out.astype(BF16),
        w_ffn_gate=w_ffn_gate.astype(BF16),
        w_ffn_up=w_ffn_up.astype(BF16),
        w_ffn_down=w_ffn_down.astype(BF16),
    )


def kernel(x, c, ctx, c_ctx, w_mod, b_mod, g_norm_mix, g_norm_ffn, w_in, b_gates, w_conv_qk, g_mh, g_sgu, w_sgu, b_sgu, s5_a_re, s5_a_im, s5_log_dt, s5_b_re, s5_b_im, s5_c_re, s5_c_im, s5_d, w_glu, b_glu, w_sconv, w_up_mlstm, w_up_sgu, w_up_s5, w_up_sconv, w_out, w_ffn_gate, w_ffn_up, w_ffn_down, g_final):
    batch, seq, d = x.shape
    ctx_len = ctx.shape[1]
    depth = w_mod.shape[0]
    assert d == D_MODEL and ctx_len == TOKEN_TILE and seq % TOKEN_TILE == 0 and batch == S5_BATCH
    ctx_tiles = ctx_len // TOKEN_TILE

    c_rows = jnp.zeros((8, d), F32).at[:batch].set(c).at[batch].set(c_ctx)
    mod_all = _modulation(c_rows, w_mod, b_mod).reshape(depth, 8, 6, d)
    stream = (ctx, x, ctx_tiles)
    stream_len = ctx_len + seq
    s5_ops = _s5_operators(s5_a_re, s5_a_im, s5_log_dt, s5_b_re, s5_b_im, s5_c_re, s5_c_im, s5_d)
    lw = _stacked_weights(g_norm_mix, g_norm_ffn, w_in, b_gates, w_conv_qk, g_mh, g_sgu, w_sgu, b_sgu,
                          w_glu, b_glu, w_sconv, w_up_mlstm, w_up_sgu, w_up_s5, w_up_sconv, w_out,
                          w_ffn_gate, w_ffn_up, w_ffn_down)

    mod_x = mod_all[:, :batch]
    mod_c = jnp.broadcast_to(mod_all[:, batch][:, None], mod_x.shape)
    mods = jnp.stack([mod_c, mod_x], axis=2)

    out = None
    for l in range(depth):
        last = l == depth - 1
        h_norm, qt, k, vt, gates_t, u_rows = _state_proj(stream, stream_len, mods, lw, l, ctx_len)
        h_fwd, h_bwd = _mlstm(qt, k, vt, gates_t, ctx_len)
        y_s5 = _s5_mixer(u_rows, s5_ops, l, stream_len // S5_CHUNK, ctx_len // S5_CHUNK)
        tile0 = ctx_tiles if last else 0
        x_mid = _merge(stream, stream_len, mods, h_norm, h_fwd, h_bwd, y_s5, lw, l, ctx_len, tile0)
        out = _ffn(x_mid, mods, lw, l, g_final.reshape(1, d), tile0, last)
        stream = (out, out, 0)
    return out
```

```python
import functools
import math

import jax
import jax.numpy as jnp
from jax import lax
from jax.experimental import pallas as pl
from jax.experimental.pallas import tpu as pltpu

F32 = jnp.float32
BF16 = jnp.bfloat16

D_MODEL = 1024
GRID_W = 64
N_BRANCH = 4
RMS_EPS = 1e-6
CONV_K = 3
HEADS = 4
HEAD_DIM = 128
MLSTM_WIDTH = HEADS * HEAD_DIM
MLSTM_CHUNK = 128
MLSTM_BATCH = 4
SGU_GROUPS = 4
SGU_GROUP_DIM = 128
SGU_WIDTH = SGU_GROUPS * SGU_GROUP_DIM
SGU_CHUNK = 128
S5_GROUP_DIM = 16
S5_GROUPS = 24
S5_WIDTH = S5_GROUPS * S5_GROUP_DIM
S5_STATE = 64
SCONV_WIDTH = 512
STATE_DIM = 3 * MLSTM_WIDTH + 4 * HEADS + S5_WIDTH
GATE_COL = 3 * MLSTM_WIDTH

LANES = 128
TOKEN_TILE = 256
S5_CHUNK = 16
S5_ROW = S5_CHUNK * S5_GROUP_DIM
S5_PAIR = 2 * S5_STATE
S5_BATCH = 4
U_COL = 2 * MLSTM_WIDTH
VMEM_LIMIT = 52 * 1024 * 1024
CONTRACT_LAST = (((1,), (1,)), ((), ()))


def _resident(shape):
    zeros = (0,) * len(shape)
    return pl.BlockSpec(shape, lambda *_: zeros, pipeline_mode=pl.Buffered(1))


def _layer_resident(stacked, layer):
    zeros = (0,) * (stacked.ndim - 1)
    return pl.BlockSpec((None,) + stacked.shape[1:], lambda *_: (layer,) + zeros,
                        pipeline_mode=pl.Buffered(1))


def _row_block(stacked, layer, start, size):
    cols = stacked.shape[-1]
    return pl.BlockSpec((pl.Element(1), pl.Element(size), pl.Element(cols)), lambda *_: (layer, start, 0),
                        pipeline_mode=pl.Buffered(1))


def _mod_spec(d, layer, tile0):
    return pl.BlockSpec((None, 1, 1, 6, d), lambda i, t: (layer, i, jnp.minimum(t + tile0, 1), 0, 0))


def _params(*semantics, flags=None):
    return pltpu.CompilerParams(dimension_semantics=semantics, vmem_limit_bytes=VMEM_LIMIT, flags=flags)


def _sigmoid(x):
    return 0.5 * jnp.tanh(0.5 * x) + 0.5


def _modulated_norm(x, gain, shift, scale):
    y = x * lax.rsqrt(jnp.mean(x * x, axis=-1, keepdims=True) + RMS_EPS) * gain
    return y * (1.0 + scale) + shift


def _conv3(a, w, first, last):
    n = a.shape[0]
    prev = jnp.where(first, 0.0, pltpu.roll(a, 1, 0))
    nxt = jnp.where(last, 0.0, pltpu.roll(a, n - 1, 0))
    return prev * w[0:1] + a * w[1:2] + nxt * w[2:3]


def _row_edges(tile_idx, n_rows, ctx_len):
    period = jnp.where(tile_idx == 0, ctx_len, GRID_W)
    pos = lax.broadcasted_iota(jnp.int32, (n_rows, 1), 0) & (period - 1)
    return pos == 0, pos == period - 1


def _mod_kernel(c_ref, w_ref, b_ref, o_ref):
    a = c_ref[...]
    a = a * _sigmoid(a)
    a_hi = a.astype(BF16)
    a_lo = (a - a_hi.astype(F32)).astype(BF16)
    w = w_ref[0]
    w_hi = w.astype(BF16)
    w_lo = (w - w_hi.astype(F32)).astype(BF16)
    acc = jnp.dot(a_hi, w_hi, preferred_element_type=F32)
    acc = acc + jnp.dot(a_lo, w_hi, preferred_element_type=F32)
    acc = acc + jnp.dot(a_hi, w_lo, preferred_element_type=F32)
    o_ref[0] = acc + b_ref[0]


def _modulation(c_rows, w_mod, b_mod):
    depth, d, n = w_mod.shape
    tn = 1536
    return pl.pallas_call(
        _mod_kernel,
        out_shape=jax.ShapeDtypeStruct((depth, 8, n), F32),
        grid=(depth, n // tn),
        in_specs=[
            pl.BlockSpec((8, d), lambda l, j: (0, 0)),
            pl.BlockSpec((1, d, tn), lambda l, j: (l, 0, j)),
            pl.BlockSpec((1, 1, tn), lambda l, j: (l, 0, j)),
        ],
        out_specs=pl.BlockSpec((1, 8, tn), lambda l, j: (l, 0, j)),
        compiler_params=_params("arbitrary", "arbitrary"),
        name="modulation",
    )(c_rows, w_mod, b_mod.reshape(depth, 1, n))


def _swap_lane_blocks(tiles):
    n = LANES // S5_GROUP_DIM
    tiles = list(tiles)
    block = lax.broadcasted_iota(jnp.int32, tiles[0].shape, 1) // S5_GROUP_DIM
    dist = n // 2
    while dist:
        keep = (block & dist) == 0
        for i in range(n):
            if i & dist:
                continue
            a, b = tiles[i], tiles[i + dist]
            tiles[i] = jnp.where(keep, a, pltpu.roll(b, dist * S5_GROUP_DIM, 1))
            tiles[i + dist] = jnp.where(keep, pltpu.roll(a, LANES - dist * S5_GROUP_DIM, 1), b)
        dist //= 2
    return tiles


def _to_group_rows(z, u_ref, buf_ref):
    n_chunks = z.shape[0] // S5_CHUNK
    per_block = LANES // S5_GROUP_DIM
    for gb in range(S5_WIDTH // LANES):
        buf_ref[gb] = z[:, gb * LANES:(gb + 1) * LANES]
    for gb in range(S5_WIDTH // LANES):
        for half in range(S5_CHUNK // per_block):
            by_token = [buf_ref[gb, pl.ds(half * per_block + s, n_chunks, stride=S5_CHUNK), :]
                        for s in range(per_block)]
            for gl, tile in enumerate(_swap_lane_blocks(by_token)):
                u_ref[gb * per_block + gl, :, half * LANES:(half + 1) * LANES] = tile


def _from_group_rows(y_ref, buf_ref):
    n_chunks = y_ref.shape[1]
    per_block = LANES // S5_GROUP_DIM
    for gb in range(S5_WIDTH // LANES):
        for half in range(S5_CHUNK // per_block):
            by_group = [y_ref[gb * per_block + gl, :, half * LANES:(half + 1) * LANES]
                        for gl in range(per_block)]
            for s, tile in enumerate(_swap_lane_blocks(by_group)):
                buf_ref[gb, pl.ds(half * per_block + s, n_chunks, stride=S5_CHUNK), :] = tile
    return jnp.concatenate([buf_ref[gb] for gb in range(S5_WIDTH // LANES)], axis=-1)


def _stream_specs(stream, tile0):
    ctx_src, lat_src, offset = stream
    d = ctx_src.shape[-1]
    specs = [pl.BlockSpec((1, TOKEN_TILE, d), lambda i, t: (i, 0, 0)),
             pl.BlockSpec((1, TOKEN_TILE, d), lambda i, t: (i, jnp.maximum(t + tile0 - offset, 0), 0))]
    return specs, (ctx_src, lat_src)


def _stream_tile(t, ctx_ref, lat_ref):
    return jnp.where(t == 0, ctx_ref[0], lat_ref[0])


def _state_proj_kernel(xc_ref, xl_ref, mod_ref, gain_ref, wqk_ref, wv_ref, wg_ref, wu_ref, bgt_ref, wc_ref,
                       eye_ref, hn_ref, qt_ref, k_ref, vt_ref, gt_ref, u_ref, ubuf_ref, *, ctx_len):
    t = pl.program_id(1)
    mod = mod_ref[0, 0]
    h = _modulated_norm(_stream_tile(t, xc_ref, xl_ref), gain_ref[...], mod[0:1], mod[1:2]).astype(BF16)
    hn_ref[0] = h
    z_qk = lax.dot_general(h, wqk_ref[0], CONTRACT_LAST, preferred_element_type=F32)
    z_u = lax.dot_general(h, wu_ref[0], CONTRACT_LAST, preferred_element_type=F32)
    vt_ref[0] = lax.dot_general(wv_ref[0], h, CONTRACT_LAST, preferred_element_type=F32).astype(BF16)
    gt_ref[0] = lax.dot_general(wg_ref[0], h, CONTRACT_LAST, preferred_element_type=F32) + bgt_ref[...]
    first, last = _row_edges(t, z_qk.shape[0], ctx_len)
    wc = wc_ref[...]
    w = MLSTM_WIDTH
    q = _conv3(z_qk[:, 0:w], wc[0:3], first, last)
    k = _conv3(z_qk[:, w:2 * w], wc[3:6], first, last)
    q = (q * _sigmoid(q)).astype(BF16)
    qt_ref[0] = lax.dot_general(eye_ref[...], q, CONTRACT_LAST, preferred_element_type=F32).astype(BF16)
    k_ref[0] = ((k * _sigmoid(k)) * (HEAD_DIM ** -0.5)).astype(BF16)
    _to_group_rows(z_u, u_ref, ubuf_ref)


def _state_proj(stream, s, mods, lw, layer, ctx_len):
    b, _, d = stream[0].shape
    tm = TOKEN_TILE
    tiles = s // tm
    stream_specs, stream_args = _stream_specs(stream, 0)
    tok = lambda width: pl.BlockSpec((1, tm, width), lambda i, t: (i, t, 0))
    chan = lambda height: pl.BlockSpec((1, height, tm), lambda i, t: (i, 0, t))
    w_t = lw['w_in_t']
    column_ranges = ((0, 2 * MLSTM_WIDTH), (2 * MLSTM_WIDTH, MLSTM_WIDTH), (GATE_COL, 4 * HEADS),
                     (GATE_COL + 4 * HEADS, S5_WIDTH))
    eye = jnp.eye(MLSTM_WIDTH, dtype=BF16)
    return pl.pallas_call(
        functools.partial(_state_proj_kernel, ctx_len=ctx_len),
        out_shape=(
            jax.ShapeDtypeStruct((b, s, d), BF16),
            jax.ShapeDtypeStruct((b, MLSTM_WIDTH, s), BF16),
            jax.ShapeDtypeStruct((b, s, MLSTM_WIDTH), BF16),
            jax.ShapeDtypeStruct((b, MLSTM_WIDTH, s), BF16),
            jax.ShapeDtypeStruct((b, 4 * HEADS, s), F32),
            jax.ShapeDtypeStruct((S5_GROUPS, b * s // S5_CHUNK, S5_ROW), F32),
        ),
        grid=(b, s // tm),
        in_specs=stream_specs + [_mod_spec(d, layer, 0), _layer_resident(lw['gain_mix'], layer)]
                 + [_row_block(w_t, layer, start, size) for start, size in column_ranges]
                 + [_layer_resident(lw['b_gates_t'], layer), _layer_resident(lw['w_conv'], layer),
                    _resident(eye.shape)],
        out_specs=(tok(d), chan(MLSTM_WIDTH), tok(MLSTM_WIDTH), chan(MLSTM_WIDTH), chan(4 * HEADS),
                   pl.BlockSpec((S5_GROUPS, tm // S5_CHUNK, S5_ROW), lambda i, t: (0, i * tiles + t, 0))),
        scratch_shapes=[pltpu.VMEM((S5_WIDTH // LANES, tm, LANES), F32)],
        compiler_params=_params("arbitrary", "arbitrary"),
        name="state_proj",
    )(*stream_args, mods, lw['gain_mix'], w_t, w_t, w_t, w_t, lw['b_gates_t'], lw['w_conv'], eye)


def _mlstm_chunk(direction, bi, qt_ref, k_ref, vt_ref, gt_ref, h_ref, ct_ref, n_ref, m_ref):
    lc = MLSTM_CHUNK
    sign = 1 - 2 * direction
    row = lax.broadcasted_iota(jnp.int32, (lc, lc), 0)
    col = lax.broadcasted_iota(jnp.int32, (lc, lc), 1)
    tri = (col - row) * sign >= 0
    tri_b = jnp.where(tri, 1.0, 0.0).astype(BF16)

    gt = gt_ref[bi]
    lf_t = jax.nn.log_sigmoid(gt)
    hi = lf_t.astype(BF16)
    lo = (lf_t - hi.astype(F32)).astype(BF16)
    bcum_t = (jnp.dot(hi, tri_b, preferred_element_type=F32)
              + jnp.dot(lo, tri_b, preferred_element_type=F32))
    last = lc - 1 if direction == 0 else 0

    qt = qt_ref[bi]
    k = k_ref[bi]
    vt = vt_ref[bi]
    heads = []
    for hd in range(HEADS):
        sl = slice(hd * HEAD_DIM, (hd + 1) * HEAD_DIM)
        b_row = bcum_t[HEADS + hd:HEADS + hd + 1, :]
        heads.append(dict(
            bi=bi, sl=sl, qt=qt[sl, :], k=k[:, sl], vt=vt[sl, :], tri=tri,
            a_row=gt[hd:hd + 1, :] - b_row, b_row=b_row, b_last=b_row[:, last:last + 1],
            m=m_ref[hd][:, 0:1], ct=ct_ref[hd][...], n=n_ref[hd][...],
            h_ref=h_ref, ct_ref=ct_ref[hd], n_ref=n_ref[hd], m_ref=m_ref[hd]))
    return heads


def _mlstm_scores(hd):
    hd['kq'] = jnp.dot(hd['k'], hd['qt'], preferred_element_type=F32)
    n_rows = jnp.broadcast_to(hd['n'], (8, HEAD_DIM)).astype(BF16)
    hd['qn'] = jnp.dot(n_rows, hd['qt'], preferred_element_type=F32)[0:1]
    hd['cq'] = jnp.dot(hd['ct'].astype(BF16), hd['qt'], preferred_element_type=F32)


def _mlstm_weights(hd):
    lc = MLSTM_CHUNK
    a_row, b_row, b_last, m = hd['a_row'], hd['b_row'], hd['b_last'], hd['m']
    m_new = b_last + jnp.maximum(m, jnp.max(a_row, axis=1, keepdims=True))
    w_row = jnp.exp(a_row + (b_last - m_new))
    hd['decay'] = jnp.exp(b_last + m - m_new)
    hd['m_new'] = m_new
    hd['vt_w'] = (hd['vt'].astype(F32) * w_row).astype(BF16)
    hd['w_rows'] = jnp.broadcast_to(w_row, (8, lc)).astype(BF16)

    a_bc = jnp.broadcast_to(a_row, (lc, lc)).T
    log_w = jnp.where(hd['tri'], a_bc + b_row, -jnp.inf)
    log_inter = b_row + m
    m_t = jnp.maximum(log_inter, jnp.max(log_w, axis=0, keepdims=True))
    hd['inter'] = jnp.exp(log_inter - m_t)
    s_t = hd['kq'] * jnp.exp(log_w - m_t)
    den = hd['inter'] * hd['qn'] + jnp.sum(s_t, axis=0, keepdims=True)
    hd['r_den'] = 1.0 / jnp.maximum(jnp.abs(den), jnp.exp(-m_t))
    hd['s_t'] = s_t.astype(BF16)


def _mlstm_outputs(hd):
    num = jnp.dot(hd['vt'], hd['s_t'], preferred_element_type=F32) + hd['inter'] * hd['cq']
    hd['h_ref'][hd['bi'], :, hd['sl']] = (num * hd['r_den']).T
    hd['ct_ref'][...] = hd['decay'] * hd['ct'] + jnp.dot(hd['vt_w'], hd['k'], preferred_element_type=F32)
    hd['n_ref'][...] = (hd['decay'] * hd['n']
                        + jnp.dot(hd['w_rows'], hd['k'], preferred_element_type=F32)[0:1])
    hd['m_ref'][...] = jnp.broadcast_to(hd['m_new'], (1, LANES))


def _mlstm_kernel(qtf_ref, kf_ref, vtf_ref, gtf_ref, qtb_ref, kb_ref, vtb_ref, gtb_ref,
                  hf_ref, hb_ref, *state_refs):
    per = MLSTM_BATCH * 2 * HEADS
    ct_refs, n_refs, m_refs = state_refs[:per], state_refs[per:2 * per], state_refs[2 * per:]

    @pl.when(pl.program_id(1) == 0)
    def _():
        for ref in state_refs:
            ref[...] = jnp.zeros_like(ref)

    heads = []
    for bi in range(MLSTM_BATCH):
        for direction, refs in enumerate(((qtf_ref, kf_ref, vtf_ref, gtf_ref, hf_ref),
                                          (qtb_ref, kb_ref, vtb_ref, gtb_ref, hb_ref))):
            own = slice((2 * bi + direction) * HEADS, (2 * bi + direction + 1) * HEADS)
            heads += _mlstm_chunk(direction, bi, *refs, ct_refs[own], n_refs[own], m_refs[own])
    for phase in (_mlstm_scores, _mlstm_weights, _mlstm_outputs):
        for hd in heads:
            phase(hd)


def _mlstm(qt, k, vt, gates_t, ctx_len):
    b, s, w = k.shape
    lc = MLSTM_CHUNK
    n_chunks = s // lc
    n_ctx = ctx_len // lc

    def back(j):
        return jnp.where(j < n_ctx, n_ctx - 1 - j, n_chunks + n_ctx - 1 - j)

    nb = MLSTM_BATCH

    def specs(direction, chunk):
        return [pl.BlockSpec((nb, w, lc), lambda i, j: (i, 0, chunk(j))),
                pl.BlockSpec((nb, lc, w), lambda i, j: (i, chunk(j), 0)),
                pl.BlockSpec((nb, w, lc), lambda i, j: (i, 0, chunk(j))),
                pl.BlockSpec((nb, 2 * HEADS, lc), lambda i, j: (i, direction, chunk(j)))]

    fwd = lambda j: j
    out = jax.ShapeDtypeStruct((b, s, w), F32)
    chains = nb * 2 * HEADS
    return pl.pallas_call(
        _mlstm_kernel,
        out_shape=(out, out),
        grid=(b // nb, n_chunks),
        in_specs=specs(0, fwd) + specs(1, back),
        out_specs=(pl.BlockSpec((nb, lc, w), lambda i, j: (i, j, 0)),
                   pl.BlockSpec((nb, lc, w), lambda i, j: (i, back(j), 0))),
        scratch_shapes=([pltpu.VMEM((HEAD_DIM, HEAD_DIM), F32)] * chains
                        + [pltpu.VMEM((1, HEAD_DIM), F32)] * chains
                        + [pltpu.VMEM((1, LANES), F32)] * chains),
        compiler_params=_params("arbitrary", "arbitrary"),
        name="mlstm",
    )(qt, k, vt, gates_t, qt, k, vt, gates_t)


def _s5_operators(a_re, a_im, log_dt, b_re, b_im, c_re, c_im, d_skip):
    hp = lax.Precision.HIGH
    lch = S5_CHUNK
    depth = a_re.shape[0]
    a_re = a_re.astype(F32)
    a_im = a_im.astype(F32)
    dt = jnp.exp(log_dt.astype(F32))[..., None]
    la_re, la_im = dt * a_re, dt * a_im
    mag = jnp.exp(la_re)
    ab_re, ab_im = mag * jnp.cos(la_im), mag * jnp.sin(la_im)
    nr, ni = ab_re - 1.0, ab_im
    den = a_re * a_re + a_im * a_im
    f_re = ((nr * a_re + ni * a_im) / den)[:, :, :, None, :]
    f_im = ((ni * a_re - nr * a_im) / den)[:, :, :, None, :]
    bt_re = jnp.swapaxes(b_re.astype(F32), -1, -2)[:, None]
    bt_im = jnp.swapaxes(b_im.astype(F32), -1, -2)[:, None]
    bb_re = f_re * bt_re - f_im * bt_im
    bb_im = f_re * bt_im + f_im * bt_re
    c_re = c_re.astype(F32)
    c_im = c_im.astype(F32)

    n = jnp.arange(-(lch - 1), lch + 1, dtype=F32)[:, None]
    mg = jnp.exp(n * la_re[:, :, :, None, :])
    ang = n * la_im[:, :, :, None, :]
    pw_re, pw_im = mg * jnp.cos(ang), mg * jnp.sin(ang)
    zero = lch - 1

    def powers(direction, first, step):
        start = zero + first
        stop = start + step * lch
        sl = slice(start, stop if stop >= 0 else None, step)
        return pw_re[:, direction, :, sl, None, :], pw_im[:, direction, :, sl, None, :]

    def times(x_re, x_im, p):
        return x_re * p[0] - x_im * p[1], x_re * p[1] + x_im * p[0]

    def rows256(x):
        return x.reshape(depth, S5_GROUPS, S5_ROW, S5_STATE)

    bbf = (bb_re[:, 0, :, None], bb_im[:, 0, :, None])
    bbb = (bb_re[:, 1, :, None], bb_im[:, 1, :, None])
    cc = (c_re[:, :, None], c_im[:, :, None])

    def response(bb, direction, sign):
        l_re, l_im = times(*bb, powers(direction, 0, -sign))
        r_re, r_im = times(*cc, powers(direction, 0, sign))
        left = jnp.concatenate([rows256(l_re), rows256(l_im)], axis=-1)
        right = jnp.concatenate([rows256(r_re), -rows256(r_im)], axis=-1)
        return jnp.einsum('lgap,lgbp->lgab', left, right, precision=hp)

    src = (jnp.arange(S5_ROW) // S5_GROUP_DIM)[:, None]
    tgt = (jnp.arange(S5_ROW) // S5_GROUP_DIM)[None, :]
    d_rows = jnp.tile(d_skip.astype(F32).reshape(depth, S5_GROUPS, 1, S5_GROUP_DIM), (1, 1, 1, lch))
    toeplitz = (jnp.where(tgt >= src, response(bbf, 0, 1), 0.0)
                + jnp.where(src >= tgt, response(bbb, 1, -1), 0.0)
                + jnp.eye(S5_ROW, dtype=F32) * d_rows)

    def pair_block(x_re, x_im):
        x_re = x_re.reshape(depth, S5_GROUPS // 2, 2, S5_ROW, S5_STATE)
        x_im = x_im.reshape(depth, S5_GROUPS // 2, 2, S5_ROW, S5_STATE)
        z = jnp.zeros_like(x_re[:, :, 0])
        top = jnp.concatenate([x_re[:, :, 0], z, x_im[:, :, 0], z], axis=-1)
        bottom = jnp.concatenate([z, x_re[:, :, 1], z, x_im[:, :, 1]], axis=-1)
        return jnp.concatenate([top, bottom], axis=-2)

    def state_map(x, p):
        re, im = times(*x, p)
        return rows256(re), rows256(im)

    w_f = pair_block(*state_map(bbf, powers(0, lch - 1, -1)))
    w_b = pair_block(*state_map(bbb, powers(1, 0, 1)))
    w_pair = jnp.concatenate([w_f, w_b], axis=-1)
    vf_re, vf_im = state_map(cc, powers(0, 1, 1))
    vb_re, vb_im = state_map(cc, powers(1, lch, -1))
    vt_pair = jnp.stack([pair_block(vf_re, -vf_im), pair_block(vb_re, -vb_im)], axis=1)

    al_re, al_im = pw_re[:, :, :, zero + lch], pw_im[:, :, :, zero + lch]
    def rows(a):
        a = a.reshape(depth, 2, 1, S5_GROUPS * S5_STATE)
        return jnp.broadcast_to(a, (depth, 2, S5_BATCH, S5_GROUPS * S5_STATE)).reshape(depth, 2 * S5_BATCH, -1)
    a_step = jnp.stack([rows(al_re), rows(al_im)], axis=1)
    return toeplitz.astype(BF16), w_pair.astype(BF16), vt_pair.astype(BF16), a_step


def _s5_chunk_state_kernel(u_ref, w_ref, x_ref, *, n_chunks):
    u = jnp.concatenate([u_ref[0], u_ref[1]], axis=-1).astype(BF16)
    x = jnp.dot(u, w_ref[0], preferred_element_type=F32)
    for b in range(S5_BATCH):
        xb = x[b * n_chunks:(b + 1) * n_chunks]
        for direction in range(2):
            for part in range(2):
                lo = (2 * direction + part) * S5_PAIR
                x_ref[part, pl.ds(direction * S5_BATCH + b, n_chunks, stride=2 * S5_BATCH), :] = (
                    xb[:, lo:lo + S5_PAIR])


def _s5_scan_kernel(x_ref, a_ref, of_ref, ob_ref, *, n_chunks, n_ctx):
    tile = 2 * S5_BATCH
    n_slabs = x_ref.shape[0]
    a_re = a_ref[0]
    a_im = a_ref[1]
    fwd_rows = lax.broadcasted_iota(jnp.int32, (tile, S5_PAIR), 0) < S5_BATCH

    def body(i, state):
        cb = jnp.where(i < n_ctx, n_ctx - 1 - i, n_chunks + n_ctx - 1 - i)
        rf = pl.multiple_of(i * tile, tile)
        rb = pl.multiple_of(cb * tile, tile)
        new_state = []
        for p in range(n_slabs // 2):
            re, im = state[2 * p], state[2 * p + 1]
            for j, val in ((2 * p, re), (2 * p + 1, im)):
                of_ref[j, pl.ds(rf, tile), :] = val
                ob_ref[j, pl.ds(rb, tile), :] = val
            in_re = jnp.where(fwd_rows, x_ref[2 * p, pl.ds(rf, tile), :], x_ref[2 * p, pl.ds(rb, tile), :])
            in_im = jnp.where(fwd_rows, x_ref[2 * p + 1, pl.ds(rf, tile), :],
                              x_ref[2 * p + 1, pl.ds(rb, tile), :])
            ar, ai = a_re[:, p * S5_PAIR:(p + 1) * S5_PAIR], a_im[:, p * S5_PAIR:(p + 1) * S5_PAIR]
            new_state.append(ar * re - ai * im + in_re)
            new_state.append(ar * im + ai * re + in_im)
        return tuple(new_state)

    zero = jnp.zeros((tile, S5_PAIR), F32)
    lax.fori_loop(0, n_chunks, body, (zero,) * n_slabs, unroll=4)


def _s5_output_kernel(u_ref, t_ref, xf_ref, xb_ref, v_ref, y_ref, *, n_chunks):
    stride = 2 * S5_BATCH

    def batch_rows(ref, first_row):
        return jnp.concatenate(
            [jnp.concatenate([ref[part, pl.ds(first_row + b, n_chunks, stride=stride), :]
                              for part in range(2)], axis=-1) for b in range(S5_BATCH)], axis=0)

    xf = batch_rows(xf_ref, 0)
    xb = batch_rows(xb_ref, S5_BATCH)
    last = (((1,), (1,)), ((), ()))
    y = lax.dot_general(xf.astype(BF16), v_ref[0, 0], last, preferred_element_type=F32)
    y = y + lax.dot_general(xb.astype(BF16), v_ref[1, 0], last, preferred_element_type=F32)
    for a in range(2):
        within = jnp.dot(u_ref[a].astype(BF16), t_ref[a], preferred_element_type=F32)
        y_ref[a] = within + y[:, a * S5_ROW:(a + 1) * S5_ROW]


def _s5_pair_kernel(u_ref, w_ref, t_ref, v_ref, a_ref, y_ref, xc_ref, xf_ref, xb_ref, *, n_chunks, n_ctx):
    _s5_chunk_state_kernel(u_ref, w_ref, xc_ref, n_chunks=n_chunks)
    _s5_scan_kernel(xc_ref, a_ref, xf_ref, xb_ref, n_chunks=n_chunks, n_ctx=n_ctx)
    _s5_output_kernel(u_ref, t_ref, xf_ref, xb_ref, v_ref, y_ref, n_chunks=n_chunks)


def _s5_mixer(u_rows, ops, layer, n_chunks, n_ctx):
    toeplitz, w_pair, vt_pair, a_step = ops
    g, rows, _ = u_rows.shape
    n_pairs = g // 2
    states = pltpu.VMEM((2, n_chunks * 2 * S5_BATCH, S5_PAIR), F32)
    return pl.pallas_call(
        functools.partial(_s5_pair_kernel, n_chunks=n_chunks, n_ctx=n_ctx),
        out_shape=jax.ShapeDtypeStruct((g, rows, S5_ROW), F32),
        grid=(n_pairs,),
        in_specs=[pl.BlockSpec((2, rows, S5_ROW), lambda i: (i, 0, 0)),
                  pl.BlockSpec((None, 1, 2 * S5_ROW, 4 * S5_PAIR), lambda i: (layer, i, 0, 0)),
                  pl.BlockSpec((None, 2, S5_ROW, S5_ROW), lambda i: (layer, i, 0, 0)),
                  pl.BlockSpec((None, 2, 1, 2 * S5_ROW, 2 * S5_PAIR), lambda i: (layer, 0, i, 0, 0)),
                  pl.BlockSpec((None, 2, 2 * S5_BATCH, S5_PAIR), lambda i: (layer, 0, 0, i))],
        out_specs=pl.BlockSpec((2, rows, S5_ROW), lambda i: (i, 0, 0)),
        scratch_shapes=[states, states, states],
        compiler_params=_params("arbitrary"),
        name="s5_pair",
    )(u_rows, w_pair, toeplitz, vt_pair, a_step)


def _merge_kernel(xc_ref, xl_ref, mod_ref, hn_ref, hf_ref, hb_ref, ys_ref, wo_ref, gmh_ref, gsgu_ref,
                  wsgu_ref, bsgu_ref, wglu_ref, bglu_ref, wsc_ref, wum_ref, wug_ref, wus_ref,
                  wuc_ref, wout_ref, o_ref, ybuf_ref, *, ctx_len, tile0):
    t = pl.program_id(1) + tile0
    mod = mod_ref[0, 0]
    h = hn_ref[0]
    tm = h.shape[0]

    def proj(start, width):
        return lax.dot_general(h, wo_ref[0, start:start + width, :], CONTRACT_LAST,
                               preferred_element_type=F32)

    w = MLSTM_WIDTH
    z_all = proj(0, 6 * w)
    z_o, z_su, z_sv, z_cb, z_cc, z_cx = (z_all[:, i * w:(i + 1) * w] for i in range(6))
    ys = jax.nn.gelu(_from_group_rows(ys_ref, ybuf_ref))
    z_glu = jnp.dot(ys.astype(BF16), wglu_ref[...], preferred_element_type=F32)

    hm = hf_ref[0] + hb_ref[0]
    parts = []
    for hd in range(HEADS):
        hh = hm[:, hd * HEAD_DIM:(hd + 1) * HEAD_DIM]
        parts.append(hh * lax.rsqrt(jnp.mean(hh * hh, axis=-1, keepdims=True) + RMS_EPS))
    y_a = (jnp.concatenate(parts, axis=-1) * gmh_ref[...]) * _sigmoid(z_o)

    su = jax.nn.gelu(z_su)
    sv = jax.nn.gelu(z_sv)
    mu = jnp.mean(sv, axis=-1, keepdims=True)
    cen = sv - mu
    var = jnp.mean(cen * cen, axis=-1, keepdims=True)
    vn = (cen * lax.rsqrt(var + RMS_EPS) * gsgu_ref[...]).astype(BF16)
    bias = bsgu_ref[...]
    rows = []
    for n in range(tm // SGU_CHUNK):
        cols = []
        for gi in range(SGU_GROUPS):
            blk = vn[n * SGU_CHUNK:(n + 1) * SGU_CHUNK, gi * SGU_GROUP_DIM:(gi + 1) * SGU_GROUP_DIM]
            cols.append(jnp.dot(wsgu_ref[gi], blk, preferred_element_type=F32) + bias[:, gi:gi + 1])
        rows.append(jnp.concatenate(cols, axis=-1))
    y_b = su * jnp.concatenate(rows, axis=0)

    y_c = ys * _sigmoid(z_glu + bglu_ref[...])

    first, last = _row_edges(t, tm, ctx_len)
    y_d = z_cb * _conv3(z_cc * z_cx, wsc_ref[...], first, last)

    acc = None
    for j, (y, up_ref) in enumerate(zip((y_a, y_b, y_c, y_d), (wum_ref, wug_ref, wus_ref, wuc_ref))):
        gate = proj(6 * w + j * D_MODEL, D_MODEL)
        term = _sigmoid(gate) * jnp.dot(y.astype(BF16), up_ref[...], preferred_element_type=F32)
        acc = term if acc is None else acc + term
    mixed = jnp.dot(acc.astype(BF16), wout_ref[...], preferred_element_type=F32)
    o_ref[0] = _stream_tile(t, xc_ref, xl_ref) + mod[2:3] * mixed


def _merge(stream, s, mods, h_norm, h_fwd, h_bwd, y_s5, lw, layer, ctx_len, tile0):
    b, _, d = stream[0].shape
    tm = TOKEN_TILE
    tiles = s // tm
    stream_specs, stream_args = _stream_specs(stream, tile0)
    tok = lambda width: pl.BlockSpec((1, tm, width), lambda i, t: (i, t + tile0, 0))
    group_rows = pl.BlockSpec((S5_GROUPS, tm // S5_CHUNK, S5_ROW),
                              lambda i, t: (0, i * tiles + t + tile0, 0))
    weights = tuple(lw[name] for name in (
        'g_mh', 'g_sgu', 'w_sgu', 'b_sgu_t', 'w_glu', 'b_glu', 'w_sconv',
        'w_up_mlstm', 'w_up_sgu', 'w_up_s5', 'w_up_sconv', 'w_out'))
    w_t = lw['w_in_t']
    out_side = _row_block(w_t, layer, STATE_DIM, w_t.shape[1] - STATE_DIM)
    return pl.pallas_call(
        functools.partial(_merge_kernel, ctx_len=ctx_len, tile0=tile0),
        out_shape=jax.ShapeDtypeStruct((b, s - tile0 * tm, d), F32),
        grid=(b, tiles - tile0),
        in_specs=stream_specs
                 + [_mod_spec(d, layer, tile0), tok(d), tok(MLSTM_WIDTH), tok(MLSTM_WIDTH), group_rows, out_side]
                 + [_layer_resident(w, layer) for w in weights],
        out_specs=pl.BlockSpec((1, tm, d), lambda i, t: (i, t, 0)),
        scratch_shapes=[pltpu.VMEM((S5_WIDTH // LANES, tm, LANES), F32)],
        compiler_params=_params("arbitrary", "arbitrary"),
        name="merge",
    )(*stream_args, mods, h_norm, h_fwd, h_bwd, y_s5, w_t, *weights)


def _ffn_kernel(x_ref, mod_ref, gain_ref, wg_ref, wu_ref, wd_ref, gf_ref, o_ref, *, final):
    x = x_ref[0]
    mod = mod_ref[0, 0]
    h = _modulated_norm(x, gain_ref[...], mod[3:4], mod[4:5]).astype(BF16)
    a = jnp.dot(h, wg_ref[...], preferred_element_type=F32)
    b = jnp.dot(h, wu_ref[...], preferred_element_type=F32)
    hid = ((a * _sigmoid(a)) * b).astype(BF16)
    y = x + mod[5:6] * jnp.dot(hid, wd_ref[...], preferred_element_type=F32)
    if final:
        y = y * lax.rsqrt(jnp.mean(y * y, axis=-1, keepdims=True) + RMS_EPS) * gf_ref[...]
    o_ref[0] = y


def _ffn(xs, mods, lw, layer, g_final, tile0, final):
    b, s, d = xs.shape
    tm = 2 * TOKEN_TILE if tile0 > 0 and s % (2 * TOKEN_TILE) == 0 else TOKEN_TILE
    weights = tuple(lw[name] for name in ('gain_ffn', 'w_ffn_gate', 'w_ffn_up', 'w_ffn_down'))
    return pl.pallas_call(
        functools.partial(_ffn_kernel, final=final),
        out_shape=jax.ShapeDtypeStruct((b, s, d), F32),
        grid=(b, s // tm),
        in_specs=[pl.BlockSpec((1, tm, d), lambda i, t: (i, t, 0)), _mod_spec(d, layer, tile0)]
                 + [_layer_resident(w, layer) for w in weights] + [_resident((1, d))],
        out_specs=pl.BlockSpec((1, tm, d), lambda i, t: (i, t, 0)),
        compiler_params=_params("arbitrary", "arbitrary"),
        name="ffn_final" if final else "ffn",
    )(xs, mods, *weights, g_final)


def _stacked_weights(g_norm_mix, g_norm_ffn, w_in, b_gates, w_conv_qk, g_mh, g_sgu, w_sgu, b_sgu, w_glu,
                     b_glu, w_sconv, w_up_mlstm, w_up_sgu, w_up_s5, w_up_sconv, w_out,
                     w_ffn_gate, w_ffn_up, w_ffn_down):
    depth = w_in.shape[0]
    row = lambda a: a[:, None, :]
    return dict(
        gain_mix=row(g_norm_mix),
        gain_ffn=row(g_norm_ffn),
        w_in_t=jnp.swapaxes(w_in, 1, 2).astype(BF16),
        b_gates_t=jnp.broadcast_to(b_gates[:, :, None], (depth, 4 * HEADS, TOKEN_TILE)),
        w_conv=w_conv_qk.reshape(depth, 2 * CONV_K, MLSTM_WIDTH),
        g_mh=row(g_mh),
        g_sgu=row(g_sgu),
        w_sgu=w_sgu.astype(BF16),
        b_sgu_t=jnp.swapaxes(b_sgu, 1, 2),
        w_glu=w_glu.astype(BF16),
        b_glu=row(b_glu),
        w_sconv=w_sconv,
        w_up_mlstm=w_up_mlstm.astype(BF16),
        w_up_sgu=w_up_sgu.astype(BF16),
        w_up_s5=w_up_s5.astype(BF16),
        w_up_sconv=w_up_sconv.astype(BF16),
        w_out=w_out.astype(BF16),
        w_ffn_gate=w_ffn_gate.astype(BF16),
        w_ffn_up=w_ffn_up.astype(BF16),
        w_ffn_down=w_ffn_down.astype(BF16),
    )


def kernel(x, c, ctx, c_ctx, w_mod, b_mod, g_norm_mix, g_norm_ffn, w_in, b_gates, w_conv_qk, g_mh, g_sgu, w_sgu, b_sgu, s5_a_re, s5_a_im, s5_log_dt, s5_b_re, s5_b_im, s5_c_re, s5_c_im, s5_d, w_glu, b_glu, w_sconv, w_up_mlstm, w_up_sgu, w_up_s5, w_up_sconv, w_out, w_ffn_gate, w_ffn_up, w_ffn_down, g_final):
    batch, seq, d = x.shape
    ctx_len = ctx.shape[1]
    depth = w_mod.shape[0]
    assert d == D_MODEL and ctx_len == TOKEN_TILE and seq % TOKEN_TILE == 0 and batch == S5_BATCH
    ctx_tiles = ctx_len // TOKEN_TILE

    c_rows = jnp.zeros((8, d), F32).at[:batch].set(c).at[batch].set(c_ctx)
    mod_all = _modulation(c_rows, w_mod, b_mod).reshape(depth, 8, 6, d)
    stream = (ctx, x, ctx_tiles)
    stream_len = ctx_len + seq
    s5_ops = _s5_operators(s5_a_re, s5_a_im, s5_log_dt, s5_b_re, s5_b_im, s5_c_re, s5_c_im, s5_d)
    lw = _stacked_weights(g_norm_mix, g_norm_ffn, w_in, b_gates, w_conv_qk, g_mh, g_sgu, w_sgu, b_sgu,
                          w_glu, b_glu, w_sconv, w_up_mlstm, w_up_sgu, w_up_s5, w_up_sconv, w_out,
                          w_ffn_gate, w_ffn_up, w_ffn_down)

    mod_x = mod_all[:, :batch]
    mod_c = jnp.broadcast_to(mod_all[:, batch][:, None], mod_x.shape)
    mods = jnp.stack([mod_c, mod_x], axis=2)

    out = None
    for l in range(depth):
        last = l == depth - 1
        h_norm, qt, k, vt, gates_t, u_rows = _state_proj(stream, stream_len, mods, lw, l, ctx_len)
        h_fwd, h_bwd = _mlstm(qt, k, vt, gates_t, ctx_len)
        y_s5 = _s5_mixer(u_rows, s5_ops, l, stream_len // S5_CHUNK, ctx_len // S5_CHUNK)
        tile0 = ctx_tiles if last else 0
        x_mid = _merge(stream, stream_len, mods, h_norm, h_fwd, h_bwd, y_s5, lw, l, ctx_len, tile0)
        out = _ffn(x_mid, mods, lw, l, g_final.reshape(1, d), tile0, last)
        stream = (out, out, 0)
    return out
```

```python
import functools
import math

import jax
import jax.numpy as jnp
from jax import lax
from jax.experimental import pallas as pl
from jax.experimental.pallas import tpu as pltpu

F32 = jnp.float32
BF16 = jnp.bfloat16

D_MODEL = 1024
GRID_W = 64
N_BRANCH = 4
RMS_EPS = 1e-6
CONV_K = 3
HEADS = 4
HEAD_DIM = 128
MLSTM_WIDTH = HEADS * HEAD_DIM
MLSTM_CHUNK = 128
MLSTM_BATCH = 4
SGU_GROUPS = 4
SGU_GROUP_DIM = 128
SGU_WIDTH = SGU_GROUPS * SGU_GROUP_DIM
SGU_CHUNK = 128
S5_GROUP_DIM = 16
S5_GROUPS = 24
S5_WIDTH = S5_GROUPS * S5_GROUP_DIM
S5_STATE = 64
SCONV_WIDTH = 512
STATE_DIM = 3 * MLSTM_WIDTH + 4 * HEADS + S5_WIDTH
GATE_COL = 3 * MLSTM_WIDTH

LANES = 128
TOKEN_TILE = 256
S5_CHUNK = 16
S5_ROW = S5_CHUNK * S5_GROUP_DIM
S5_PAIR = 2 * S5_STATE
S5_BATCH = 4
U_COL = 2 * MLSTM_WIDTH
VMEM_LIMIT = 52 * 1024 * 1024
CONTRACT_LAST = (((1,), (1,)), ((), ()))


def _resident(shape):
    zeros = (0,) * len(shape)
    return pl.BlockSpec(shape, lambda *_: zeros, pipeline_mode=pl.Buffered(1))


def _layer_resident(stacked, layer):
    zeros = (0,) * (stacked.ndim - 1)
    return pl.BlockSpec((None,) + stacked.shape[1:], lambda *_: (layer,) + zeros,
                        pipeline_mode=pl.Buffered(1))


def _row_block(stacked, layer, start, size):
    cols = stacked.shape[-1]
    return pl.BlockSpec((pl.Element(1), pl.Element(size), pl.Element(cols)), lambda *_: (layer, start, 0),
                        pipeline_mode=pl.Buffered(1))


def _mod_spec(d, layer, tile0):
    return pl.BlockSpec((None, 1, 1, 6, d), lambda i, t: (layer, i, jnp.minimum(t + tile0, 1), 0, 0))


def _params(*semantics, flags=None):
    return pltpu.CompilerParams(dimension_semantics=semantics, vmem_limit_bytes=VMEM_LIMIT, flags=flags)


def _sigmoid(x):
    return 0.5 * jnp.tanh(0.5 * x) + 0.5


def _modulated_norm(x, gain, shift, scale):
    y = x * lax.rsqrt(jnp.mean(x * x, axis=-1, keepdims=True) + RMS_EPS) * gain
    return y * (1.0 + scale) + shift


def _conv3(a, w, first, last):
    n = a.shape[0]
    prev = jnp.where(first, 0.0, pltpu.roll(a, 1, 0))
    nxt = jnp.where(last, 0.0, pltpu.roll(a, n - 1, 0))
    return prev * w[0:1] + a * w[1:2] + nxt * w[2:3]


def _row_edges(tile_idx, n_rows, ctx_len):
    period = jnp.where(tile_idx == 0, ctx_len, GRID_W)
    pos = lax.broadcasted_iota(jnp.int32, (n_rows, 1), 0) & (period - 1)
    return pos == 0, pos == period - 1


def _mod_kernel(c_ref, w_ref, b_ref, o_ref):
    a = c_ref[...]
    a = a * _sigmoid(a)
    a_hi = a.astype(BF16)
    a_lo = (a - a_hi.astype(F32)).astype(BF16)
    w = w_ref[0]
    w_hi = w.astype(BF16)
    w_lo = (w - w_hi.astype(F32)).astype(BF16)
    acc = jnp.dot(a_hi, w_hi, preferred_element_type=F32)
    acc = acc + jnp.dot(a_lo, w_hi, preferred_element_type=F32)
    acc = acc + jnp.dot(a_hi, w_lo, preferred_element_type=F32)
    o_ref[0] = acc + b_ref[0]


def _modulation(c_rows, w_mod, b_mod):
    depth, d, n = w_mod.shape
    tn = 1536
    return pl.pallas_call(
        _mod_kernel,
        out_shape=jax.ShapeDtypeStruct((depth, 8, n), F32),
        grid=(depth, n // tn),
        in_specs=[
            pl.BlockSpec((8, d), lambda l, j: (0, 0)),
            pl.BlockSpec((1, d, tn), lambda l, j: (l, 0, j)),
            pl.BlockSpec((1, 1, tn), lambda l, j: (l, 0, j)),
        ],
        out_specs=pl.BlockSpec((1, 8, tn), lambda l, j: (l, 0, j)),
        compiler_params=_params("arbitrary", "arbitrary"),
        name="modulation",
    )(c_rows, w_mod, b_mod.reshape(depth, 1, n))


def _swap_lane_blocks(tiles):
    n = LANES // S5_GROUP_DIM
    tiles = list(tiles)
    block = lax.broadcasted_iota(jnp.int32, tiles[0].shape, 1) // S5_GROUP_DIM
    dist = n // 2
    while dist:
        keep = (block & dist) == 0
        for i in range(n):
            if i & dist:
                continue
            a, b = tiles[i], tiles[i + dist]
            tiles[i] = jnp.where(keep, a, pltpu.roll(b, dist * S5_GROUP_DIM, 1))
            tiles[i + dist] = jnp.where(keep, pltpu.roll(a, LANES - dist * S5_GROUP_DIM, 1), b)
        dist //= 2
    return tiles


def _to_group_rows(z, u_ref, buf_ref):
    n_chunks = z.shape[0] // S5_CHUNK
    per_block = LANES // S5_GROUP_DIM
    for gb in range(S5_WIDTH // LANES):
        buf_ref[gb] = z[:, gb * LANES:(gb + 1) * LANES]
    for gb in range(S5_WIDTH // LANES):
        for half in range(S5_CHUNK // per_block):
            by_token = [buf_ref[gb, pl.ds(half * per_block + s, n_chunks, stride=S5_CHUNK), :]
                        for s in range(per_block)]
            for gl, tile in enumerate(_swap_lane_blocks(by_token)):
                u_ref[gb * per_block + gl, :, half * LANES:(half + 1) * LANES] = tile


def _from_group_rows(y_ref, buf_ref):
    n_chunks = y_ref.shape[1]
    per_block = LANES // S5_GROUP_DIM
    for gb in range(S5_WIDTH // LANES):
        for half in range(S5_CHUNK // per_block):
            by_group = [y_ref[gb * per_block + gl, :, half * LANES:(half + 1) * LANES]
                        for gl in range(per_block)]
            for s, tile in enumerate(_swap_lane_blocks(by_group)):
                buf_ref[gb, pl.ds(half * per_block + s, n_chunks, stride=S5_CHUNK), :] = tile
    return jnp.concatenate([buf_ref[gb] for gb in range(S5_WIDTH // LANES)], axis=-1)


def _stream_specs(stream, tile0):
    ctx_src, lat_src, offset = stream
    d = ctx_src.shape[-1]
    specs = [pl.BlockSpec((1, TOKEN_TILE, d), lambda i, t: (i, 0, 0)),
             pl.BlockSpec((1, TOKEN_TILE, d), lambda i, t: (i, jnp.maximum(t + tile0 - offset, 0), 0))]
    return specs, (ctx_src, lat_src)


def _stream_tile(t, ctx_ref, lat_ref):
    return jnp.where(t == 0, ctx_ref[0], lat_ref[0])


def _state_proj_kernel(xc_ref, xl_ref, mod_ref, gain_ref, wqk_ref, wv_ref, wg_ref, wu_ref, bgt_ref, wc_ref,
                       eye_ref, hn_ref, qt_ref, k_ref, vt_ref, gt_ref, u_ref, ubuf_ref, *, ctx_len):
    t = pl.program_id(1)
    mod = mod_ref[0, 0]
    h = _modulated_norm(_stream_tile(t, xc_ref, xl_ref), gain_ref[...], mod[0:1], mod[1:2]).astype(BF16)
    hn_ref[0] = h
    z_qk = lax.dot_general(h, wqk_ref[0], CONTRACT_LAST, preferred_element_type=F32)
    z_u = lax.dot_general(h, wu_ref[0], CONTRACT_LAST, preferred_element_type=F32)
    vt_ref[0] = lax.dot_general(wv_ref[0], h, CONTRACT_LAST, preferred_element_type=F32).astype(BF16)
    gt_ref[0] = lax.dot_general(wg_ref[0], h, CONTRACT_LAST, preferred_element_type=F32) + bgt_ref[...]
    first, last = _row_edges(t, z_qk.shape[0], ctx_len)
    wc = wc_ref[...]
    w = MLSTM_WIDTH
    q = _conv3(z_qk[:, 0:w], wc[0:3], first, last)
    k = _conv3(z_qk[:, w:2 * w], wc[3:6], first, last)
    q = (q * _sigmoid(q)).astype(BF16)
    qt_ref[0] = lax.dot_general(eye_ref[...], q, CONTRACT_LAST, preferred_element_type=F32).astype(BF16)
    k_ref[0] = ((k * _sigmoid(k)) * (HEAD_DIM ** -0.5)).astype(BF16)
    _to_group_rows(z_u, u_ref, ubuf_ref)


def _state_proj(stream, s, mods, lw, layer, ctx_len):
    b, _, d = stream[0].shape
    tm = TOKEN_TILE
    tiles = s // tm
    stream_specs, stream_args = _stream_specs(stream, 0)
    tok = lambda width: pl.BlockSpec((1, tm, width), lambda i, t: (i, t, 0))
    chan = lambda height: pl.BlockSpec((1, height, tm), lambda i, t: (i, 0, t))
    w_t = lw['w_in_t']
    column_ranges = ((0, 2 * MLSTM_WIDTH), (2 * MLSTM_WIDTH, MLSTM_WIDTH), (GATE_COL, 4 * HEADS),
                     (GATE_COL + 4 * HEADS, S5_WIDTH))
    eye = jnp.eye(MLSTM_WIDTH, dtype=BF16)
    return pl.pallas_call(
        functools.partial(_state_proj_kernel, ctx_len=ctx_len),
        out_shape=(
            jax.ShapeDtypeStruct((b, s, d), BF16),
            jax.ShapeDtypeStruct((b, MLSTM_WIDTH, s), BF16),
            jax.ShapeDtypeStruct((b, s, MLSTM_WIDTH), BF16),
            jax.ShapeDtypeStruct((b, MLSTM_WIDTH, s), BF16),
            jax.ShapeDtypeStruct((b, 4 * HEADS, s), F32),
            jax.ShapeDtypeStruct((S5_GROUPS, b * s // S5_CHUNK, S5_ROW), F32),
        ),
        grid=(b, s // tm),
        in_specs=stream_specs + [_mod_spec(d, layer, 0), _layer_resident(lw['gain_mix'], layer)]
                 + [_row_block(w_t, layer, start, size) for start, size in column_ranges]
                 + [_layer_resident(lw['b_gates_t'], layer), _layer_resident(lw['w_conv'], layer),
                    _resident(eye.shape)],
        out_specs=(tok(d), chan(MLSTM_WIDTH), tok(MLSTM_WIDTH), chan(MLSTM_WIDTH), chan(4 * HEADS),
                   pl.BlockSpec((S5_GROUPS, tm // S5_CHUNK, S5_ROW), lambda i, t: (0, i * tiles + t, 0))),
        scratch_shapes=[pltpu.VMEM((S5_WIDTH // LANES, tm, LANES), F32)],
        compiler_params=_params("arbitrary", "arbitrary"),
        name="state_proj",
    )(*stream_args, mods, lw['gain_mix'], w_t, w_t, w_t, w_t, lw['b_gates_t'], lw['w_conv'], eye)


def _mlstm_chunk(direction, bi, qt_ref, k_ref, vt_ref, gt_ref, h_ref, ct_ref, n_ref, m_ref):
    lc = MLSTM_CHUNK
    sign = 1 - 2 * direction
    row = lax.broadcasted_iota(jnp.int32, (lc, lc), 0)
    col = lax.broadcasted_iota(jnp.int32, (lc, lc), 1)
    tri = (col - row) * sign >= 0
    tri_b = jnp.where(tri, 1.0, 0.0).astype(BF16)

    gt = gt_ref[bi]
    lf_t = jax.nn.log_sigmoid(gt)
    hi = lf_t.astype(BF16)
    lo = (lf_t - hi.astype(F32)).astype(BF16)
    bcum_t = (jnp.dot(hi, tri_b, preferred_element_type=F32)
              + jnp.dot(lo, tri_b, preferred_element_type=F32))
    last = lc - 1 if direction == 0 else 0

    qt = qt_ref[bi]
    k = k_ref[bi]
    vt = vt_ref[bi]
    heads = []
    for hd in range(HEADS):
        sl = slice(hd * HEAD_DIM, (hd + 1) * HEAD_DIM)
        b_row = bcum_t[HEADS + hd:HEADS + hd + 1, :]
        heads.append(dict(
            bi=bi, sl=sl, qt=qt[sl, :], k=k[:, sl], vt=vt[sl, :], tri=tri,
            a_row=gt[hd:hd + 1, :] - b_row, b_row=b_row, b_last=b_row[:, last:last + 1],
            m=m_ref[hd][:, 0:1], ct=ct_ref[hd][...], n=n_ref[hd][...],
            h_ref=h_ref, ct_ref=ct_ref[hd], n_ref=n_ref[hd], m_ref=m_ref[hd]))
    return heads


def _mlstm_scores(hd):
    hd['kq'] = jnp.dot(hd['k'], hd['qt'], preferred_element_type=F32)
    n_rows = jnp.broadcast_to(hd['n'], (8, HEAD_DIM)).astype(BF16)
    hd['qn'] = jnp.dot(n_rows, hd['qt'], preferred_element_type=F32)[0:1]
    hd['cq'] = jnp.dot(hd['ct'].astype(BF16), hd['qt'], preferred_element_type=F32)


def _mlstm_weights(hd):
    lc = MLSTM_CHUNK
    a_row, b_row, b_last, m = hd['a_row'], hd['b_row'], hd['b_last'], hd['m']
    m_new = b_last + jnp.maximum(m, jnp.max(a_row, axis=1, keepdims=True))
    w_row = jnp.exp(a_row + (b_last - m_new))
    hd['decay'] = jnp.exp(b_last + m - m_new)
    hd['m_new'] = m_new
    hd['vt_w'] = (hd['vt'].astype(F32) * w_row).astype(BF16)
    hd['w_rows'] = jnp.broadcast_to(w_row, (8, lc)).astype(BF16)

    a_bc = jnp.broadcast_to(a_row, (lc, lc)).T
    log_w = jnp.where(hd['tri'], a_bc + b_row, -jnp.inf)
    log_inter = b_row + m
    m_t = jnp.maximum(log_inter, jnp.max(log_w, axis=0, keepdims=True))
    hd['inter'] = jnp.exp(log_inter - m_t)
    s_t = hd['kq'] * jnp.exp(log_w - m_t)
    den = hd['inter'] * hd['qn'] + jnp.sum(s_t, axis=0, keepdims=True)
    hd['r_den'] = 1.0 / jnp.maximum(jnp.abs(den), jnp.exp(-m_t))
    hd['s_t'] = s_t.astype(BF16)


def _mlstm_outputs(hd):
    num = jnp.dot(hd['vt'], hd['s_t'], preferred_element_type=F32) + hd['inter'] * hd['cq']
    hd['h_ref'][hd['bi'], :, hd['sl']] = (num * hd['r_den']).T
    hd['ct_ref'][...] = hd['decay'] * hd['ct'] + jnp.dot(hd['vt_w'], hd['k'], preferred_element_type=F32)
    hd['n_ref'][...] = (hd['decay'] * hd['n']
                        + jnp.dot(hd['w_rows'], hd['k'], preferred_element_type=F32)[0:1])
    hd['m_ref'][...] = jnp.broadcast_to(hd['m_new'], (1, LANES))


def _mlstm_kernel(qtf_ref, kf_ref, vtf_ref, gtf_ref, qtb_ref, kb_ref, vtb_ref, gtb_ref,
                  hf_ref, hb_ref, *state_refs):
    per = MLSTM_BATCH * 2 * HEADS
    ct_refs, n_refs, m_refs = state_refs[:per], state_refs[per:2 * per], state_refs[2 * per:]

    @pl.when(pl.program_id(1) == 0)
    def _():
        for ref in state_refs:
            ref[...] = jnp.zeros_like(ref)

    heads = []
    for bi in range(MLSTM_BATCH):
        for direction, refs in enumerate(((qtf_ref, kf_ref, vtf_ref, gtf_ref, hf_ref),
                                          (qtb_ref, kb_ref, vtb_ref, gtb_ref, hb_ref))):
            own = slice((2 * bi + direction) * HEADS, (2 * bi + direction + 1) * HEADS)
            heads += _mlstm_chunk(direction, bi, *refs, ct_refs[own], n_refs[own], m_refs[own])
    for phase in (_mlstm_scores, _mlstm_weights, _mlstm_outputs):
        for hd in heads:
            phase(hd)


def _mlstm(qt, k, vt, gates_t, ctx_len):
    b, s, w = k.shape
    lc = MLSTM_CHUNK
    n_chunks = s // lc
    n_ctx = ctx_len // lc

    def back(j):
        return jnp.where(j < n_ctx, n_ctx - 1 - j, n_chunks + n_ctx - 1 - j)

    nb = MLSTM_BATCH

    def specs(direction, chunk):
        return [pl.BlockSpec((nb, w, lc), lambda i, j: (i, 0, chunk(j))),
                pl.BlockSpec((nb, lc, w), lambda i, j: (i, chunk(j), 0)),
                pl.BlockSpec((nb, w, lc), lambda i, j: (i, 0, chunk(j))),
                pl.BlockSpec((nb, 2 * HEADS, lc), lambda i, j: (i, direction, chunk(j)))]

    fwd = lambda j: j
    out = jax.ShapeDtypeStruct((b, s, w), F32)
    chains = nb * 2 * HEADS
    return pl.pallas_call(
        _mlstm_kernel,
        out_shape=(out, out),
        grid=(b // nb, n_chunks),
        in_specs=specs(0, fwd) + specs(1, back),
        out_specs=(pl.BlockSpec((nb, lc, w), lambda i, j: (i, j, 0)),
                   pl.BlockSpec((nb, lc, w), lambda i, j: (i, back(j), 0))),
        scratch_shapes=([pltpu.VMEM((HEAD_DIM, HEAD_DIM), F32)] * chains
                        + [pltpu.VMEM((1, HEAD_DIM), F32)] * chains
                        + [pltpu.VMEM((1, LANES), F32)] * chains),
        compiler_params=_params("arbitrary", "arbitrary"),
        name="mlstm",
    )(qt, k, vt, gates_t, qt, k, vt, gates_t)


def _s5_operators(a_re, a_im, log_dt, b_re, b_im, c_re, c_im, d_skip):
    hp = lax.Precision.HIGH
    lch = S5_CHUNK
    depth = a_re.shape[0]
    a_re = a_re.astype(F32)
    a_im = a_im.astype(F32)
    dt = jnp.exp(log_dt.astype(F32))[..., None]
    la_re, la_im = dt * a_re, dt * a_im
    mag = jnp.exp(la_re)
    ab_re, ab_im = mag * jnp.cos(la_im), mag * jnp.sin(la_im)
    nr, ni = ab_re - 1.0, ab_im
    den = a_re * a_re + a_im * a_im
    f_re = ((nr * a_re + ni * a_im) / den)[:, :, :, None, :]
    f_im = ((ni * a_re - nr * a_im) / den)[:, :, :, None, :]
    bt_re = jnp.swapaxes(b_re.astype(F32), -1, -2)[:, None]
    bt_im = jnp.swapaxes(b_im.astype(F32), -1, -2)[:, None]
    bb_re = f_re * bt_re - f_im * bt_im
    bb_im = f_re * bt_im + f_im * bt_re
    c_re = c_re.astype(F32)
    c_im = c_im.astype(F32)

    n = jnp.arange(-(lch - 1), lch + 1, dtype=F32)[:, None]
    mg = jnp.exp(n * la_re[:, :, :, None, :])
    ang = n * la_im[:, :, :, None, :]
    pw_re, pw_im = mg * jnp.cos(ang), mg * jnp.sin(ang)
    zero = lch - 1

    def powers(direction, first, step):
        start = zero + first
        stop = start + step * lch
        sl = slice(start, stop if stop >= 0 else None, step)
        return pw_re[:, direction, :, sl, None, :], pw_im[:, direction, :, sl, None, :]

    def times(x_re, x_im, p):
        return x_re * p[0] - x_im * p[1], x_re * p[1] + x_im * p[0]

    def rows256(x):
        return x.reshape(depth, S5_GROUPS, S5_ROW, S5_STATE)

    bbf = (bb_re[:, 0, :, None], bb_im[:, 0, :, None])
    bbb = (bb_re[:, 1, :, None], bb_im[:, 1, :, None])
    cc = (c_re[:, :, None], c_im[:, :, None])

    def response(bb, direction, sign):
        l_re, l_im = times(*bb, powers(direction, 0, -sign))
        r_re, r_im = times(*cc, powers(direction, 0, sign))
        left = jnp.concatenate([rows256(l_re), rows256(l_im)], axis=-1)
        right = jnp.concatenate([rows256(r_re), -rows256(r_im)], axis=-1)
        return jnp.einsum('lgap,lgbp->lgab', left, right, precision=hp)

    src = (jnp.arange(S5_ROW) // S5_GROUP_DIM)[:, None]
    tgt = (jnp.arange(S5_ROW) // S5_GROUP_DIM)[None, :]
    d_rows = jnp.tile(d_skip.astype(F32).reshape(depth, S5_GROUPS, 1, S5_GROUP_DIM), (1, 1, 1, lch))
    toeplitz = (jnp.where(tgt >= src, response(bbf, 0, 1), 0.0)
                + jnp.where(src >= tgt, response(bbb, 1, -1), 0.0)
                + jnp.eye(S5_ROW, dtype=F32) * d_rows)

    def state_map(x, p):
        re, im = times(*x, p)
        return rows256(re), rows256(im)

    w_parts = jnp.stack(state_map(bbf, powers(0, lch - 1, -1)) + state_map(bbb, powers(1, 0, 1)), axis=1)
    vf_re, vf_im = state_map(cc, powers(0, 1, 1))
    vb_re, vb_im = state_map(cc, powers(1, lch, -1))
    v_parts = jnp.stack([vf_re, -vf_im, vb_re, -vb_im], axis=1)

    al_re, al_im = pw_re[:, :, :, zero + lch], pw_im[:, :, :, zero + lch]
    def rows(a):
        a = a.reshape(depth, 2, 1, S5_GROUPS * S5_STATE)
        return jnp.broadcast_to(a, (depth, 2, S5_BATCH, S5_GROUPS * S5_STATE)).reshape(depth, 2 * S5_BATCH, -1)
    a_step = jnp.stack([rows(al_re), rows(al_im)], axis=1)
    return toeplitz.astype(BF16), w_parts, v_parts, a_step


def _s5_pair_matrix(parts_ref, re, im):
    z = jnp.zeros((S5_ROW, S5_STATE), F32)
    top = jnp.concatenate([parts_ref[re, 0], z, parts_ref[im, 0], z], axis=-1)
    bottom = jnp.concatenate([z, parts_ref[re, 1], z, parts_ref[im, 1]], axis=-1)
    return jnp.concatenate([top, bottom], axis=0).astype(BF16)


def _s5_chunk_state_kernel(u_ref, w_ref, x_ref, *, n_chunks):
    u = jnp.concatenate([u_ref[0], u_ref[1]], axis=-1).astype(BF16)
    w = jnp.concatenate([_s5_pair_matrix(w_ref, 0, 1), _s5_pair_matrix(w_ref, 2, 3)], axis=-1)
    x = jnp.dot(u, w, preferred_element_type=F32)
    for b in range(S5_BATCH):
        xb = x[b * n_chunks:(b + 1) * n_chunks]
        for direction in range(2):
            for part in range(2):
                lo = (2 * direction + part) * S5_PAIR
                x_ref[part, pl.ds(direction * S5_BATCH + b, n_chunks, stride=2 * S5_BATCH), :] = (
                    xb[:, lo:lo + S5_PAIR])


def _s5_scan_kernel(x_ref, a_ref, of_ref, ob_ref, *, n_chunks, n_ctx):
    tile = 2 * S5_BATCH
    n_slabs = x_ref.shape[0]
    a_re = a_ref[0]
    a_im = a_ref[1]
    fwd_rows = lax.broadcasted_iota(jnp.int32, (tile, S5_PAIR), 0) < S5_BATCH

    def body(i, state):
        cb = jnp.where(i < n_ctx, n_ctx - 1 - i, n_chunks + n_ctx - 1 - i)
        rf = pl.multiple_of(i * tile, tile)
        rb = pl.multiple_of(cb * tile, tile)
        new_state = []
        for p in range(n_slabs // 2):
            re, im = state[2 * p], state[2 * p + 1]
            for j, val in ((2 * p, re), (2 * p + 1, im)):
                of_ref[j, pl.ds(rf, tile), :] = val
                ob_ref[j, pl.ds(rb, tile), :] = val
            in_re = jnp.where(fwd_rows, x_ref[2 * p, pl.ds(rf, tile), :], x_ref[2 * p, pl.ds(rb, tile), :])
            in_im = jnp.where(fwd_rows, x_ref[2 * p + 1, pl.ds(rf, tile), :],
                              x_ref[2 * p + 1, pl.ds(rb, tile), :])
            ar, ai = a_re[:, p * S5_PAIR:(p + 1) * S5_PAIR], a_im[:, p * S5_PAIR:(p + 1) * S5_PAIR]
            new_state.append(ar * re - ai * im + in_re)
            new_state.append(ar * im + ai * re + in_im)
        return tuple(new_state)

    zero = jnp.zeros((tile, S5_PAIR), F32)
    lax.fori_loop(0, n_chunks, body, (zero,) * n_slabs, unroll=4)


def _s5_output_kernel(u_ref, t_ref, xf_ref, xb_ref, v_ref, y_ref, *, n_chunks):
    stride = 2 * S5_BATCH

    def batch_rows(ref, first_row):
        return jnp.concatenate(
            [jnp.concatenate([ref[part, pl.ds(first_row + b, n_chunks, stride=stride), :]
                              for part in range(2)], axis=-1) for b in range(S5_BATCH)], axis=0)

    xf = batch_rows(xf_ref, 0)
    xb = batch_rows(xb_ref, S5_BATCH)
    last = (((1,), (1,)), ((), ()))
    y = lax.dot_general(xf.astype(BF16), _s5_pair_matrix(v_ref, 0, 1), last, preferred_element_type=F32)
    y = y + lax.dot_general(xb.astype(BF16), _s5_pair_matrix(v_ref, 2, 3), last, preferred_element_type=F32)
    for a in range(2):
        within = jnp.dot(u_ref[a].astype(BF16), t_ref[a], preferred_element_type=F32)
        y_ref[a] = within + y[:, a * S5_ROW:(a + 1) * S5_ROW]


def _s5_pair_kernel(u_ref, w_ref, t_ref, v_ref, a_ref, y_ref, xc_ref, xf_ref, xb_ref, *, n_chunks, n_ctx):
    _s5_chunk_state_kernel(u_ref, w_ref, xc_ref, n_chunks=n_chunks)
    _s5_scan_kernel(xc_ref, a_ref, xf_ref, xb_ref, n_chunks=n_chunks, n_ctx=n_ctx)
    _s5_output_kernel(u_ref, t_ref, xf_ref, xb_ref, v_ref, y_ref, n_chunks=n_chunks)


def _s5_mixer(u_rows, ops, layer, n_chunks, n_ctx):
    toeplitz, w_parts, v_parts, a_step = ops
    g, rows, _ = u_rows.shape
    n_pairs = g // 2
    states = pltpu.VMEM((2, n_chunks * 2 * S5_BATCH, S5_PAIR), F32)
    parts = pl.BlockSpec((None, 4, 2, S5_ROW, S5_STATE), lambda i: (layer, 0, i, 0, 0))
    return pl.pallas_call(
        functools.partial(_s5_pair_kernel, n_chunks=n_chunks, n_ctx=n_ctx),
        out_shape=jax.ShapeDtypeStruct((g, rows, S5_ROW), F32),
        grid=(n_pairs,),
        in_specs=[pl.BlockSpec((2, rows, S5_ROW), lambda i: (i, 0, 0)),
                  parts,
                  pl.BlockSpec((None, 2, S5_ROW, S5_ROW), lambda i: (layer, i, 0, 0)),
                  parts,
                  pl.BlockSpec((None, 2, 2 * S5_BATCH, S5_PAIR), lambda i: (layer, 0, 0, i))],
        out_specs=pl.BlockSpec((2, rows, S5_ROW), lambda i: (i, 0, 0)),
        scratch_shapes=[states, states, states],
        compiler_params=_params("arbitrary"),
        name="s5_pair",
    )(u_rows, w_parts, toeplitz, v_parts, a_step)


def _merge_kernel(xc_ref, xl_ref, mod_ref, hn_ref, hf_ref, hb_ref, ys_ref, wo_ref, gmh_ref, gsgu_ref,
                  wsgu_ref, bsgu_ref, wglu_ref, bglu_ref, wsc_ref, wum_ref, wug_ref, wus_ref,
                  wuc_ref, wout_ref, o_ref, ybuf_ref, *, ctx_len, tile0):
    t = pl.program_id(1) + tile0
    mod = mod_ref[0, 0]
    h = hn_ref[0]
    tm = h.shape[0]

    def proj(start, width):
        return lax.dot_general(h, wo_ref[0, start:start + width, :], CONTRACT_LAST,
                               preferred_element_type=F32)

    w = MLSTM_WIDTH
    z_all = proj(0, 6 * w)
    z_o, z_su, z_sv, z_cb, z_cc, z_cx = (z_all[:, i * w:(i + 1) * w] for i in range(6))
    ys = jax.nn.gelu(_from_group_rows(ys_ref, ybuf_ref))
    z_glu = jnp.dot(ys.astype(BF16), wglu_ref[...], preferred_element_type=F32)

    hm = hf_ref[0] + hb_ref[0]
    parts = []
    for hd in range(HEADS):
        hh = hm[:, hd * HEAD_DIM:(hd + 1) * HEAD_DIM]
        parts.append(hh * lax.rsqrt(jnp.mean(hh * hh, axis=-1, keepdims=True) + RMS_EPS))
    y_a = (jnp.concatenate(parts, axis=-1) * gmh_ref[...]) * _sigmoid(z_o)

    su = jax.nn.gelu(z_su)
    sv = jax.nn.gelu(z_sv)
    mu = jnp.mean(sv, axis=-1, keepdims=True)
    cen = sv - mu
    var = jnp.mean(cen * cen, axis=-1, keepdims=True)
    vn = (cen * lax.rsqrt(var + RMS_EPS) * gsgu_ref[...]).astype(BF16)
    bias = bsgu_ref[...]
    rows = []
    for n in range(tm // SGU_CHUNK):
        cols = []
        for gi in range(SGU_GROUPS):
            blk = vn[n * SGU_CHUNK:(n + 1) * SGU_CHUNK, gi * SGU_GROUP_DIM:(gi + 1) * SGU_GROUP_DIM]
            cols.append(jnp.dot(wsgu_ref[gi], blk, preferred_element_type=F32) + bias[:, gi:gi + 1])
        rows.append(jnp.concatenate(cols, axis=-1))
    y_b = su * jnp.concatenate(rows, axis=0)

    y_c = ys * _sigmoid(z_glu + bglu_ref[...])

    first, last = _row_edges(t, tm, ctx_len)
    y_d = z_cb * _conv3(z_cc * z_cx, wsc_ref[...], first, last)

    acc = None
    for j, (y, up_ref) in enumerate(zip((y_a, y_b, y_c, y_d), (wum_ref, wug_ref, wus_ref, wuc_ref))):
        gate = proj(6 * w + j * D_MODEL, D_MODEL)
        term = _sigmoid(gate) * jnp.dot(y.astype(BF16), up_ref[...], preferred_element_type=F32)
        acc = term if acc is None else acc + term
    mixed = jnp.dot(acc.astype(BF16), wout_ref[...], preferred_element_type=F32)
    o_ref[0] = _stream_tile(t, xc_ref, xl_ref) + mod[2:3] * mixed


def _merge(stream, s, mods, h_norm, h_fwd, h_bwd, y_s5, lw, layer, ctx_len, tile0):
    b, _, d = stream[0].shape
    tm = TOKEN_TILE
    tiles = s // tm
    stream_specs, stream_args = _stream_specs(stream, tile0)
    tok = lambda width: pl.BlockSpec((1, tm, width), lambda i, t: (i, t + tile0, 0))
    group_rows = pl.BlockSpec((S5_GROUPS, tm // S5_CHUNK, S5_ROW),
                              lambda i, t: (0, i * tiles + t + tile0, 0))
    weights = tuple(lw[name] for name in (
        'g_mh', 'g_sgu', 'w_sgu', 'b_sgu_t', 'w_glu', 'b_glu', 'w_sconv',
        'w_up_mlstm', 'w_up_sgu', 'w_up_s5', 'w_up_sconv', 'w_out'))
    w_t = lw['w_in_t']
    out_side = _row_block(w_t, layer, STATE_DIM, w_t.shape[1] - STATE_DIM)
    return pl.pallas_call(
        functools.partial(_merge_kernel, ctx_len=ctx_len, tile0=tile0),
        out_shape=jax.ShapeDtypeStruct((b, s - tile0 * tm, d), F32),
        grid=(b, tiles - tile0),
        in_specs=stream_specs
                 + [_mod_spec(d, layer, tile0), tok(d), tok(MLSTM_WIDTH), tok(MLSTM_WIDTH), group_rows, out_side]
                 + [_layer_resident(w, layer) for w in weights],
        out_specs=pl.BlockSpec((1, tm, d), lambda i, t: (i, t, 0)),
        scratch_shapes=[pltpu.VMEM((S5_WIDTH // LANES, tm, LANES), F32)],
        compiler_params=_params("arbitrary", "arbitrary"),
        name="merge",
    )(*stream_args, mods, h_norm, h_fwd, h_bwd, y_s5, w_t, *weights)


def _ffn_kernel(x_ref, mod_ref, gain_ref, wg_ref, wu_ref, wd_ref, gf_ref, o_ref, *, final):
    x = x_ref[0]
    mod = mod_ref[0, 0]
    h = _modulated_norm(x, gain_ref[...], mod[3:4], mod[4:5]).astype(BF16)
    a = jnp.dot(h, wg_ref[...], preferred_element_type=F32)
    b = jnp.dot(h, wu_ref[...], preferred_element_type=F32)
    hid = ((a * _sigmoid(a)) * b).astype(BF16)
    y = x + mod[5:6] * jnp.dot(hid, wd_ref[...], preferred_element_type=F32)
    if final:
        y = y * lax.rsqrt(jnp.mean(y * y, axis=-1, keepdims=True) + RMS_EPS) * gf_ref[...]
    o_ref[0] = y


def _ffn(xs, mods, lw, layer, g_final, tile0, final):
    b, s, d = xs.shape
    tm = 2 * TOKEN_TILE if tile0 > 0 and s % (2 * TOKEN_TILE) == 0 else TOKEN_TILE
    weights = tuple(lw[name] for name in ('gain_ffn', 'w_ffn_gate', 'w_ffn_up', 'w_ffn_down'))
    return pl.pallas_call(
        functools.partial(_ffn_kernel, final=final),
        out_shape=jax.ShapeDtypeStruct((b, s, d), F32),
        grid=(b, s // tm),
        in_specs=[pl.BlockSpec((1, tm, d), lambda i, t: (i, t, 0)), _mod_spec(d, layer, tile0)]
                 + [_layer_resident(w, layer) for w in weights] + [_resident((1, d))],
        out_specs=pl.BlockSpec((1, tm, d), lambda i, t: (i, t, 0)),
        compiler_params=_params("arbitrary", "arbitrary"),
        name="ffn_final" if final else "ffn",
    )(xs, mods, *weights, g_final)


def _stacked_weights(g_norm_mix, g_norm_ffn, w_in, b_gates, w_conv_qk, g_mh, g_sgu, w_sgu, b_sgu, w_glu,
                     b_glu, w_sconv, w_up_mlstm, w_up_sgu, w_up_s5, w_up_sconv, w_out,
                     w_ffn_gate, w_ffn_up, w_ffn_down):
    depth = w_in.shape[0]
    row = lambda a: a[:, None, :]
    return dict(
        gain_mix=row(g_norm_mix),
        gain_ffn=row(g_norm_ffn),
        w_in_t=jnp.swapaxes(w_in, 1, 2).astype(BF16),
        b_gates_t=jnp.broadcast_to(b_gates[:, :, None], (depth, 4 * HEADS, TOKEN_TILE)),
        w_conv=w_conv_qk.reshape(depth, 2 * CONV_K, MLSTM_WIDTH),
        g_mh=row(g_mh),
        g_sgu=row(g_sgu),
        w_sgu=w_sgu.astype(BF16),
        b_sgu_t=jnp.swapaxes(b_sgu, 1, 2),
        w_glu=w_glu.astype(BF16),
        b_glu=row(b_glu),
        w_sconv=w_sconv,
        w_up_mlstm=w_up_mlstm.astype(BF16),
        w_up_sgu=w_up_sgu.astype(BF16),
        w_up_s5=w_up_s5.astype(BF16),
        w_up_sconv=w_up_sconv.astype(BF16),
        w_out=w_out.astype(BF16),
        w_ffn_gate=w_ffn_gate.astype(BF16),
        w_ffn_up=w_ffn_up.astype(BF16),
        w_ffn_down=w_ffn_down.astype(BF16),
    )


def kernel(x, c, ctx, c_ctx, w_mod, b_mod, g_norm_mix, g_norm_ffn, w_in, b_gates, w_conv_qk, g_mh, g_sgu, w_sgu, b_sgu, s5_a_re, s5_a_im, s5_log_dt, s5_b_re, s5_b_im, s5_c_re, s5_c_im, s5_d, w_glu, b_glu, w_sconv, w_up_mlstm, w_up_sgu, w_up_s5, w_up_sconv, w_out, w_ffn_gate, w_ffn_up, w_ffn_down, g_final):
    batch, seq, d = x.shape
    ctx_len = ctx.shape[1]
    depth = w_mod.shape[0]
    assert d == D_MODEL and ctx_len == TOKEN_TILE and seq % TOKEN_TILE == 0 and batch == S5_BATCH
    ctx_tiles = ctx_len // TOKEN_TILE

    c_rows = jnp.zeros((8, d), F32).at[:batch].set(c).at[batch].set(c_ctx)
    mod_all = _modulation(c_rows, w_mod, b_mod).reshape(depth, 8, 6, d)
    stream = (ctx, x, ctx_tiles)
    stream_len = ctx_len + seq
    s5_ops = _s5_operators(s5_a_re, s5_a_im, s5_log_dt, s5_b_re, s5_b_im, s5_c_re, s5_c_im, s5_d)
    lw = _stacked_weights(g_norm_mix, g_norm_ffn, w_in, b_gates, w_conv_qk, g_mh, g_sgu, w_sgu, b_sgu,
                          w_glu, b_glu, w_sconv, w_up_mlstm, w_up_sgu, w_up_s5, w_up_sconv, w_out,
                          w_ffn_gate, w_ffn_up, w_ffn_down)

    mod_x = mod_all[:, :batch]
    mod_c = jnp.broadcast_to(mod_all[:, batch][:, None], mod_x.shape)
    mods = jnp.stack([mod_c, mod_x], axis=2)

    out = None
    for l in range(depth):
        last = l == depth - 1
        h_norm, qt, k, vt, gates_t, u_rows = _state_proj(stream, stream_len, mods, lw, l, ctx_len)
        h_fwd, h_bwd = _mlstm(qt, k, vt, gates_t, ctx_len)
        y_s5 = _s5_mixer(u_rows, s5_ops, l, stream_len // S5_CHUNK, ctx_len // S5_CHUNK)
        tile0 = ctx_tiles if last else 0
        x_mid = _merge(stream, stream_len, mods, h_norm, h_fwd, h_bwd, y_s5, lw, l, ctx_len, tile0)
        out = _ffn(x_mid, mods, lw, l, g_final.reshape(1, d), tile0, last)
        stream = (out, out, 0)
    return out
```

```python
import functools

import jax
import jax.numpy as jnp
from jax import lax
from jax.experimental import pallas as pl
from jax.experimental.pallas import tpu as pltpu

F32 = jnp.float32
BF16 = jnp.bfloat16

D_MODEL = 1024
GRID_W = 64
RMS_EPS = 1e-6
CONV_K = 3
HEADS = 4
HEAD_DIM = 128
MLSTM_WIDTH = HEADS * HEAD_DIM
MLSTM_CHUNK = 128
MLSTM_BATCH = 4
SGU_GROUPS = 4
SGU_GROUP_DIM = 128
SGU_CHUNK = 128
S5_GROUP_DIM = 16
S5_GROUPS = 24
S5_WIDTH = S5_GROUPS * S5_GROUP_DIM
S5_STATE = 64
STATE_DIM = 3 * MLSTM_WIDTH + 4 * HEADS + S5_WIDTH
GATE_COL = 3 * MLSTM_WIDTH

LANES = 128
TOKEN_TILE = 256
S5_CHUNK = 16
S5_ROW = S5_CHUNK * S5_GROUP_DIM
S5_PAIR = 2 * S5_STATE
S5_BATCH = 4
V7X_VMEM_BYTES = 64 * 1024 * 1024
VMEM_LIMIT = V7X_VMEM_BYTES - 12 * 1024 * 1024
CONTRACT_LAST = (((1,), (1,)), ((), ()))


def _resident(shape):
    zeros = (0,) * len(shape)
    return pl.BlockSpec(shape, lambda *_: zeros, pipeline_mode=pl.Buffered(1))


def _layer_resident(stacked, layer):
    zeros = (0,) * (stacked.ndim - 1)
    return pl.BlockSpec((None,) + stacked.shape[1:], lambda *_: (layer,) + zeros,
                        pipeline_mode=pl.Buffered(1))


def _row_block(stacked, layer, start, size):
    cols = stacked.shape[-1]
    return pl.BlockSpec((pl.Element(1), pl.Element(size), pl.Element(cols)), lambda *_: (layer, start, 0),
                        pipeline_mode=pl.Buffered(1))


def _mod_spec(d, layer, tile0):
    return pl.BlockSpec((None, 1, 1, 6, d), lambda i, t: (layer, i, jnp.minimum(t + tile0, 1), 0, 0))


def _params(*semantics, flags=None):
    return pltpu.CompilerParams(dimension_semantics=semantics, vmem_limit_bytes=VMEM_LIMIT, flags=flags)


def _sigmoid(x):
    return 0.5 * jnp.tanh(0.5 * x) + 0.5


def _modulated_norm(x, gain, shift, scale):
    y = x * lax.rsqrt(jnp.mean(x * x, axis=-1, keepdims=True) + RMS_EPS) * gain
    return y * (1.0 + scale) + shift


def _conv3(a, w, first, last):
    n = a.shape[0]
    prev = jnp.where(first, 0.0, pltpu.roll(a, 1, 0))
    nxt = jnp.where(last, 0.0, pltpu.roll(a, n - 1, 0))
    return prev * w[0:1] + a * w[1:2] + nxt * w[2:3]


def _row_edges(tile_idx, n_rows, ctx_len):
    period = jnp.where(tile_idx == 0, ctx_len, GRID_W)
    pos = lax.broadcasted_iota(jnp.int32, (n_rows, 1), 0) & (period - 1)
    return pos == 0, pos == period - 1


def _mod_kernel(c_ref, w_ref, b_ref, o_ref):
    a = c_ref[...]
    a = a * _sigmoid(a)
    a_hi = a.astype(BF16)
    a_lo = (a - a_hi.astype(F32)).astype(BF16)
    w = w_ref[0]
    w_hi = w.astype(BF16)
    w_lo = (w - w_hi.astype(F32)).astype(BF16)
    acc = jnp.dot(a_hi, w_hi, preferred_element_type=F32)
    acc = acc + jnp.dot(a_lo, w_hi, preferred_element_type=F32)
    acc = acc + jnp.dot(a_hi, w_lo, preferred_element_type=F32)
    o_ref[0] = acc + b_ref[0]


def _modulation(c_rows, w_mod, b_mod):
    depth, d, n = w_mod.shape
    tn = 1536
    return pl.pallas_call(
        _mod_kernel,
        out_shape=jax.ShapeDtypeStruct((depth, 8, n), F32),
        grid=(depth, n // tn),
        in_specs=[
            pl.BlockSpec((8, d), lambda l, j: (0, 0)),
            pl.BlockSpec((1, d, tn), lambda l, j: (l, 0, j)),
            pl.BlockSpec((1, 1, tn), lambda l, j: (l, 0, j)),
        ],
        out_specs=pl.BlockSpec((1, 8, tn), lambda l, j: (l, 0, j)),
        compiler_params=_params("arbitrary", "arbitrary"),
        name="modulation",
    )(c_rows, w_mod, b_mod.reshape(depth, 1, n))


def _swap_lane_blocks(tiles):
    n = LANES // S5_GROUP_DIM
    tiles = list(tiles)
    block = lax.broadcasted_iota(jnp.int32, tiles[0].shape, 1) // S5_GROUP_DIM
    dist = n // 2
    while dist:
        keep = (block & dist) == 0
        for i in range(n):
            if i & dist:
                continue
            a, b = tiles[i], tiles[i + dist]
            tiles[i] = jnp.where(keep, a, pltpu.roll(b, dist * S5_GROUP_DIM, 1))
            tiles[i + dist] = jnp.where(keep, pltpu.roll(a, LANES - dist * S5_GROUP_DIM, 1), b)
        dist //= 2
    return tiles


def _to_group_rows(z, u_ref, buf_ref):
    n_chunks = z.shape[0] // S5_CHUNK
    per_block = LANES // S5_GROUP_DIM
    for gb in range(S5_WIDTH // LANES):
        buf_ref[gb] = z[:, gb * LANES:(gb + 1) * LANES]
    for gb in range(S5_WIDTH // LANES):
        for half in range(S5_CHUNK // per_block):
            by_token = [buf_ref[gb, pl.ds(half * per_block + s, n_chunks, stride=S5_CHUNK), :]
                        for s in range(per_block)]
            for gl, tile in enumerate(_swap_lane_blocks(by_token)):
                u_ref[gb * per_block + gl, :, half * LANES:(half + 1) * LANES] = tile


def _from_group_rows(y_ref, buf_ref):
    n_chunks = y_ref.shape[1]
    per_block = LANES // S5_GROUP_DIM
    for gb in range(S5_WIDTH // LANES):
        for half in range(S5_CHUNK // per_block):
            by_group = [y_ref[gb * per_block + gl, :, half * LANES:(half + 1) * LANES]
                        for gl in range(per_block)]
            for s, tile in enumerate(_swap_lane_blocks(by_group)):
                buf_ref[gb, pl.ds(half * per_block + s, n_chunks, stride=S5_CHUNK), :] = tile
    return jnp.concatenate([buf_ref[gb] for gb in range(S5_WIDTH // LANES)], axis=-1)


def _stream_specs(stream, tile0):
    ctx_src, lat_src, offset = stream
    d = ctx_src.shape[-1]
    specs = [pl.BlockSpec((1, TOKEN_TILE, d), lambda i, t: (i, 0, 0)),
             pl.BlockSpec((1, TOKEN_TILE, d), lambda i, t: (i, jnp.maximum(t + tile0 - offset, 0), 0))]
    return specs, (ctx_src, lat_src)


def _stream_tile(t, ctx_ref, lat_ref, one_source):
    return lat_ref[0] if one_source else jnp.where(t == 0, ctx_ref[0], lat_ref[0])


def _state_proj_kernel(xc_ref, xl_ref, mod_ref, gain_ref, wqk_ref, wv_ref, wg_ref, wu_ref, bgt_ref, wc_ref,
                       eye_ref, hn_ref, qt_ref, k_ref, vt_ref, gt_ref, u_ref, ubuf_ref, *, ctx_len, one_source):
    t = pl.program_id(1)
    mod = mod_ref[0, 0]
    x = _stream_tile(t, xc_ref, xl_ref, one_source)
    h = _modulated_norm(x, gain_ref[...], mod[0:1], mod[1:2]).astype(BF16)
    hn_ref[0] = h
    z_qk = lax.dot_general(h, wqk_ref[0], CONTRACT_LAST, preferred_element_type=F32)
    z_u = lax.dot_general(h, wu_ref[0], CONTRACT_LAST, preferred_element_type=F32)
    vt_ref[0] = lax.dot_general(wv_ref[0], h, CONTRACT_LAST, preferred_element_type=F32).astype(BF16)
    gt_ref[0] = lax.dot_general(wg_ref[0], h, CONTRACT_LAST, preferred_element_type=F32) + bgt_ref[...]
    first, last = _row_edges(t, z_qk.shape[0], ctx_len)
    wc = wc_ref[...]
    w = MLSTM_WIDTH
    q = _conv3(z_qk[:, 0:w], wc[0:3], first, last)
    k = _conv3(z_qk[:, w:2 * w], wc[3:6], first, last)
    q = (q * _sigmoid(q)).astype(BF16)
    qt_ref[0] = lax.dot_general(eye_ref[...], q, CONTRACT_LAST, preferred_element_type=F32).astype(BF16)
    k_ref[0] = ((k * _sigmoid(k)) * (HEAD_DIM ** -0.5)).astype(BF16)
    _to_group_rows(z_u, u_ref, ubuf_ref)


def _state_proj(stream, s, mods, lw, layer, ctx_len):
    b, _, d = stream[0].shape
    tm = TOKEN_TILE
    tiles = s // tm
    stream_specs, stream_args = _stream_specs(stream, 0)
    tok = lambda width: pl.BlockSpec((1, tm, width), lambda i, t: (i, t, 0))
    chan = lambda height: pl.BlockSpec((1, height, tm), lambda i, t: (i, 0, t))
    w_t = lw['w_in_t']
    column_ranges = ((0, 2 * MLSTM_WIDTH), (2 * MLSTM_WIDTH, MLSTM_WIDTH), (GATE_COL, 4 * HEADS),
                     (GATE_COL + 4 * HEADS, S5_WIDTH))
    eye = jnp.eye(MLSTM_WIDTH, dtype=BF16)
    return pl.pallas_call(
        functools.partial(_state_proj_kernel, ctx_len=ctx_len, one_source=stream[0] is stream[1]),
        out_shape=(
            jax.ShapeDtypeStruct((b, s, d), BF16),
            jax.ShapeDtypeStruct((b, MLSTM_WIDTH, s), BF16),
            jax.ShapeDtypeStruct((b, s, MLSTM_WIDTH), BF16),
            jax.ShapeDtypeStruct((b, MLSTM_WIDTH, s), BF16),
            jax.ShapeDtypeStruct((b, 4 * HEADS, s), F32),
            jax.ShapeDtypeStruct((S5_GROUPS, b * s // S5_CHUNK, S5_ROW), F32),
        ),
        grid=(b, s // tm),
        in_specs=stream_specs + [_mod_spec(d, layer, 0), _layer_resident(lw['gain_mix'], layer)]
                 + [_row_block(w_t, layer, start, size) for start, size in column_ranges]
                 + [_layer_resident(lw['b_gates_t'], layer), _layer_resident(lw['w_conv'], layer),
                    _resident(eye.shape)],
        out_specs=(tok(d), chan(MLSTM_WIDTH), tok(MLSTM_WIDTH), chan(MLSTM_WIDTH), chan(4 * HEADS),
                   pl.BlockSpec((S5_GROUPS, tm // S5_CHUNK, S5_ROW), lambda i, t: (0, i * tiles + t, 0))),
        scratch_shapes=[pltpu.VMEM((S5_WIDTH // LANES, tm, LANES), F32)],
        compiler_params=_params("arbitrary", "arbitrary"),
        name="state_proj",
    )(*stream_args, mods, lw['gain_mix'], w_t, w_t, w_t, w_t, lw['b_gates_t'], lw['w_conv'], eye)


def _mlstm_chunk(direction, bi, qt_ref, k_ref, vt_ref, gt_ref, h_ref, ct_ref, n_ref, m_ref):
    lc = MLSTM_CHUNK
    sign = 1 - 2 * direction
    row = lax.broadcasted_iota(jnp.int32, (lc, lc), 0)
    col = lax.broadcasted_iota(jnp.int32, (lc, lc), 1)
    tri = (col - row) * sign >= 0
    tri_b = jnp.where(tri, 1.0, 0.0).astype(BF16)

    gt = gt_ref[bi]
    lf_t = jax.nn.log_sigmoid(gt)
    hi = lf_t.astype(BF16)
    lo = (lf_t - hi.astype(F32)).astype(BF16)
    bcum_t = (jnp.dot(hi, tri_b, preferred_element_type=F32)
              + jnp.dot(lo, tri_b, preferred_element_type=F32))
    last = lc - 1 if direction == 0 else 0

    qt = qt_ref[bi]
    k = k_ref[bi]
    vt = vt_ref[bi]
    heads = []
    for hd in range(HEADS):
        sl = slice(hd * HEAD_DIM, (hd + 1) * HEAD_DIM)
        b_row = bcum_t[HEADS + hd:HEADS + hd + 1, :]
        heads.append(dict(
            bi=bi, sl=sl, qt=qt[sl, :], k=k[:, sl], vt=vt[sl, :], tri=tri,
            a_row=gt[hd:hd + 1, :] - b_row, b_row=b_row, b_last=b_row[:, last:last + 1],
            m=m_ref[hd][:, 0:1], ct=ct_ref[hd][...], n=n_ref[hd][...],
            h_ref=h_ref, ct_ref=ct_ref[hd], n_ref=n_ref[hd], m_ref=m_ref[hd]))
    return heads


def _mlstm_scores(hd):
    hd['kq'] = jnp.dot(hd['k'], hd['qt'], preferred_element_type=F32)
    n_rows = jnp.broadcast_to(hd['n'], (8, HEAD_DIM)).astype(BF16)
    hd['qn'] = jnp.dot(n_rows, hd['qt'], preferred_element_type=F32)[0:1]
    hd['cq'] = jnp.dot(hd['ct'].astype(BF16), hd['qt'], preferred_element_type=F32)


def _mlstm_weights(hd):
    lc = MLSTM_CHUNK
    a_row, b_row, b_last, m = hd['a_row'], hd['b_row'], hd['b_last'], hd['m']
    m_new = b_last + jnp.maximum(m, jnp.max(a_row, axis=1, keepdims=True))
    w_row = jnp.exp(a_row + (b_last - m_new))
    hd['decay'] = jnp.exp(b_last + m - m_new)
    hd['m_new'] = m_new
    hd['vt_w'] = (hd['vt'].astype(F32) * w_row).astype(BF16)
    hd['w_rows'] = jnp.broadcast_to(w_row, (8, lc)).astype(BF16)

    a_bc = jnp.broadcast_to(a_row, (lc, lc)).T
    log_w = jnp.where(hd['tri'], a_bc + b_row, -jnp.inf)
    log_inter = b_row + m
    m_t = jnp.maximum(log_inter, jnp.max(log_w, axis=0, keepdims=True))
    hd['inter'] = jnp.exp(log_inter - m_t)
    s_t = hd['kq'] * jnp.exp(log_w - m_t)
    den = hd['inter'] * hd['qn'] + jnp.sum(s_t, axis=0, keepdims=True)
    hd['r_den'] = 1.0 / jnp.maximum(jnp.abs(den), jnp.exp(-m_t))
    hd['s_t'] = s_t.astype(BF16)


def _mlstm_outputs(hd):
    num = jnp.dot(hd['vt'], hd['s_t'], preferred_element_type=F32) + hd['inter'] * hd['cq']
    hd['h_ref'][hd['bi'], :, hd['sl']] = (num * hd['r_den']).T
    hd['ct_ref'][...] = hd['decay'] * hd['ct'] + jnp.dot(hd['vt_w'], hd['k'], preferred_element_type=F32)
    hd['n_ref'][...] = (hd['decay'] * hd['n']
                        + jnp.dot(hd['w_rows'], hd['k'], preferred_element_type=F32)[0:1])
    hd['m_ref'][...] = jnp.broadcast_to(hd['m_new'], (1, LANES))


def _mlstm_kernel(qtf_ref, kf_ref, vtf_ref, gtf_ref, qtb_ref, kb_ref, vtb_ref, gtb_ref,
                  hf_ref, hb_ref, *state_refs):
    per = MLSTM_BATCH * 2 * HEADS
    ct_refs, n_refs, m_refs = state_refs[:per], state_refs[per:2 * per], state_refs[2 * per:]

    @pl.when(pl.program_id(1) == 0)
    def _():
        for ref in state_refs:
            ref[...] = jnp.zeros_like(ref)

    heads = []
    for bi in range(MLSTM_BATCH):
        for direction, refs in enumerate(((qtf_ref, kf_ref, vtf_ref, gtf_ref, hf_ref),
                                          (qtb_ref, kb_ref, vtb_ref, gtb_ref, hb_ref))):
            own = slice((2 * bi + direction) * HEADS, (2 * bi + direction + 1) * HEADS)
            heads += _mlstm_chunk(direction, bi, *refs, ct_refs[own], n_refs[own], m_refs[own])
    for phase in (_mlstm_scores, _mlstm_weights, _mlstm_outputs):
        for hd in heads:
            phase(hd)


def _mlstm(qt, k, vt, gates_t, ctx_len):
    b, s, w = k.shape
    lc = MLSTM_CHUNK
    n_chunks = s // lc
    n_ctx = ctx_len // lc

    def back(j):
        return jnp.where(j < n_ctx, n_ctx - 1 - j, n_chunks + n_ctx - 1 - j)

    nb = MLSTM_BATCH

    def specs(direction, chunk):
        return [pl.BlockSpec((nb, w, lc), lambda i, j: (i, 0, chunk(j))),
                pl.BlockSpec((nb, lc, w), lambda i, j: (i, chunk(j), 0)),
                pl.BlockSpec((nb, w, lc), lambda i, j: (i, 0, chunk(j))),
                pl.BlockSpec((nb, 2 * HEADS, lc), lambda i, j: (i, direction, chunk(j)))]

    fwd = lambda j: j
    out = jax.ShapeDtypeStruct((b, s, w), F32)
    chains = nb * 2 * HEADS
    return pl.pallas_call(
        _mlstm_kernel,
        out_shape=(out, out),
        grid=(b // nb, n_chunks),
        in_specs=specs(0, fwd) + specs(1, back),
        out_specs=(pl.BlockSpec((nb, lc, w), lambda i, j: (i, j, 0)),
                   pl.BlockSpec((nb, lc, w), lambda i, j: (i, back(j), 0))),
        scratch_shapes=([pltpu.VMEM((HEAD_DIM, HEAD_DIM), F32)] * chains
                        + [pltpu.VMEM((1, HEAD_DIM), F32)] * chains
                        + [pltpu.VMEM((1, LANES), F32)] * chains),
        compiler_params=_params("arbitrary", "arbitrary"),
        name="mlstm",
    )(qt, k, vt, gates_t, qt, k, vt, gates_t)


def _s5_operators(a_re, a_im, log_dt, b_re, b_im, c_re, c_im, d_skip):
    hp = lax.Precision.HIGH
    lch = S5_CHUNK
    depth = a_re.shape[0]
    a_re = a_re.astype(F32)
    a_im = a_im.astype(F32)
    dt = jnp.exp(log_dt.astype(F32))[..., None]
    la_re, la_im = dt * a_re, dt * a_im
    mag = jnp.exp(la_re)
    ab_re, ab_im = mag * jnp.cos(la_im), mag * jnp.sin(la_im)
    nr, ni = ab_re - 1.0, ab_im
    den = a_re * a_re + a_im * a_im
    f_re = ((nr * a_re + ni * a_im) / den)[:, :, :, None, :]
    f_im = ((ni * a_re - nr * a_im) / den)[:, :, :, None, :]
    bt_re = jnp.swapaxes(b_re.astype(F32), -1, -2)[:, None]
    bt_im = jnp.swapaxes(b_im.astype(F32), -1, -2)[:, None]
    bb_re = f_re * bt_re - f_im * bt_im
    bb_im = f_re * bt_im + f_im * bt_re
    c_re = c_re.astype(F32)
    c_im = c_im.astype(F32)

    n = jnp.arange(-(lch - 1), lch + 1, dtype=F32)[:, None]
    mg = jnp.exp(n * la_re[:, :, :, None, :])
    ang = n * la_im[:, :, :, None, :]
    pw_re, pw_im = mg * jnp.cos(ang), mg * jnp.sin(ang)
    zero = lch - 1

    def powers(direction, first, step):
        start = zero + first
        stop = start + step * lch
        sl = slice(start, stop if stop >= 0 else None, step)
        return pw_re[:, direction, :, sl, None, :], pw_im[:, direction, :, sl, None, :]

    def times(x_re, x_im, p):
        return x_re * p[0] - x_im * p[1], x_re * p[1] + x_im * p[0]

    def rows256(x):
        return x.reshape(depth, S5_GROUPS, S5_ROW, S5_STATE)

    bbf = (bb_re[:, 0, :, None], bb_im[:, 0, :, None])
    bbb = (bb_re[:, 1, :, None], bb_im[:, 1, :, None])
    cc = (c_re[:, :, None], c_im[:, :, None])

    def response(bb, direction, sign):
        l_re, l_im = times(*bb, powers(direction, 0, -sign))
        r_re, r_im = times(*cc, powers(direction, 0, sign))
        left = jnp.concatenate([rows256(l_re), rows256(l_im)], axis=-1)
        right = jnp.concatenate([rows256(r_re), -rows256(r_im)], axis=-1)
        return jnp.einsum('lgap,lgbp->lgab', left, right, precision=hp)

    src = (jnp.arange(S5_ROW) // S5_GROUP_DIM)[:, None]
    tgt = (jnp.arange(S5_ROW) // S5_GROUP_DIM)[None, :]
    d_rows = jnp.tile(d_skip.astype(F32).reshape(depth, S5_GROUPS, 1, S5_GROUP_DIM), (1, 1, 1, lch))
    toeplitz = (jnp.where(tgt >= src, response(bbf, 0, 1), 0.0)
                + jnp.where(src >= tgt, response(bbb, 1, -1), 0.0)
                + jnp.eye(S5_ROW, dtype=F32) * d_rows)

    def state_map(x, p):
        re, im = times(*x, p)
        return rows256(re), rows256(im)

    w_parts = jnp.stack(state_map(bbf, powers(0, lch - 1, -1)) + state_map(bbb, powers(1, 0, 1)), axis=1)
    vf_re, vf_im = state_map(cc, powers(0, 1, 1))
    vb_re, vb_im = state_map(cc, powers(1, lch, -1))
    v_parts = jnp.stack([vf_re, -vf_im, vb_re, -vb_im], axis=1)

    al_re, al_im = pw_re[:, :, :, zero + lch], pw_im[:, :, :, zero + lch]
    def rows(a):
        a = a.reshape(depth, 2, 1, S5_GROUPS * S5_STATE)
        return jnp.broadcast_to(a, (depth, 2, S5_BATCH, S5_GROUPS * S5_STATE)).reshape(depth, 2 * S5_BATCH, -1)
    a_step = jnp.stack([rows(al_re), rows(al_im)], axis=1)
    return toeplitz.astype(BF16), w_parts, v_parts, a_step


def _s5_pair_matrix(parts_ref, re, im):
    z = jnp.zeros((S5_ROW, S5_STATE), F32)
    top = jnp.concatenate([parts_ref[re, 0], z, parts_ref[im, 0], z], axis=-1)
    bottom = jnp.concatenate([z, parts_ref[re, 1], z, parts_ref[im, 1]], axis=-1)
    return jnp.concatenate([top, bottom], axis=0).astype(BF16)


def _s5_chunk_state_kernel(u_ref, w_ref, x_ref, *, n_chunks):
    u = jnp.concatenate([u_ref[0], u_ref[1]], axis=-1).astype(BF16)
    w = jnp.concatenate([_s5_pair_matrix(w_ref, 0, 1), _s5_pair_matrix(w_ref, 2, 3)], axis=-1)
    x = jnp.dot(u, w, preferred_element_type=F32)
    for b in range(S5_BATCH):
        xb = x[b * n_chunks:(b + 1) * n_chunks]
        for direction in range(2):
            for part in range(2):
                lo = (2 * direction + part) * S5_PAIR
                x_ref[part, pl.ds(direction * S5_BATCH + b, n_chunks, stride=2 * S5_BATCH), :] = (
                    xb[:, lo:lo + S5_PAIR])


def _s5_scan_kernel(x_ref, a_ref, of_ref, ob_ref, *, n_chunks, n_ctx):
    tile = 2 * S5_BATCH
    n_slabs = x_ref.shape[0]
    a_re = a_ref[0]
    a_im = a_ref[1]
    fwd_rows = lax.broadcasted_iota(jnp.int32, (tile, S5_PAIR), 0) < S5_BATCH

    def body(i, state):
        cb = jnp.where(i < n_ctx, n_ctx - 1 - i, n_chunks + n_ctx - 1 - i)
        rf = pl.multiple_of(i * tile, tile)
        rb = pl.multiple_of(cb * tile, tile)
        new_state = []
        for p in range(n_slabs // 2):
            re, im = state[2 * p], state[2 * p + 1]
            for j, val in ((2 * p, re), (2 * p + 1, im)):
                of_ref[j, pl.ds(rf, tile), :] = val
                ob_ref[j, pl.ds(rb, tile), :] = val
            in_re = jnp.where(fwd_rows, x_ref[2 * p, pl.ds(rf, tile), :], x_ref[2 * p, pl.ds(rb, tile), :])
            in_im = jnp.where(fwd_rows, x_ref[2 * p + 1, pl.ds(rf, tile), :],
                              x_ref[2 * p + 1, pl.ds(rb, tile), :])
            ar, ai = a_re[:, p * S5_PAIR:(p + 1) * S5_PAIR], a_im[:, p * S5_PAIR:(p + 1) * S5_PAIR]
            new_state.append(ar * re - ai * im + in_re)
            new_state.append(ar * im + ai * re + in_im)
        return tuple(new_state)

    zero = jnp.zeros((tile, S5_PAIR), F32)
    lax.fori_loop(0, n_chunks, body, (zero,) * n_slabs, unroll=4)


def _s5_output_kernel(u_ref, t_ref, xf_ref, xb_ref, v_ref, y_ref, *, n_chunks):
    stride = 2 * S5_BATCH

    def batch_rows(ref, first_row):
        return jnp.concatenate(
            [jnp.concatenate([ref[part, pl.ds(first_row + b, n_chunks, stride=stride), :]
                              for part in range(2)], axis=-1) for b in range(S5_BATCH)], axis=0)

    xf = batch_rows(xf_ref, 0)
    xb = batch_rows(xb_ref, S5_BATCH)
    last = (((1,), (1,)), ((), ()))
    y = lax.dot_general(xf.astype(BF16), _s5_pair_matrix(v_ref, 0, 1), last, preferred_element_type=F32)
    y = y + lax.dot_general(xb.astype(BF16), _s5_pair_matrix(v_ref, 2, 3), last, preferred_element_type=F32)
    for a in range(2):
        within = jnp.dot(u_ref[a].astype(BF16), t_ref[a], preferred_element_type=F32)
        y_ref[a] = within + y[:, a * S5_ROW:(a + 1) * S5_ROW]


def _s5_pair_kernel(u_ref, w_ref, t_ref, v_ref, a_ref, y_ref, xc_ref, xf_ref, xb_ref, *, n_chunks, n_ctx):
    _s5_chunk_state_kernel(u_ref, w_ref, xc_ref, n_chunks=n_chunks)
    _s5_scan_kernel(xc_ref, a_ref, xf_ref, xb_ref, n_chunks=n_chunks, n_ctx=n_ctx)
    _s5_output_kernel(u_ref, t_ref, xf_ref, xb_ref, v_ref, y_ref, n_chunks=n_chunks)


def _s5_mixer(u_rows, ops, layer, n_chunks, n_ctx):
    toeplitz, w_parts, v_parts, a_step = ops
    g, rows, _ = u_rows.shape
    n_pairs = g // 2
    states = pltpu.VMEM((2, n_chunks * 2 * S5_BATCH, S5_PAIR), F32)
    parts = pl.BlockSpec((None, 4, 2, S5_ROW, S5_STATE), lambda i: (layer, 0, i, 0, 0))
    return pl.pallas_call(
        functools.partial(_s5_pair_kernel, n_chunks=n_chunks, n_ctx=n_ctx),
        out_shape=jax.ShapeDtypeStruct((g, rows, S5_ROW), F32),
        grid=(n_pairs,),
        in_specs=[pl.BlockSpec((2, rows, S5_ROW), lambda i: (i, 0, 0)),
                  parts,
                  pl.BlockSpec((None, 2, S5_ROW, S5_ROW), lambda i: (layer, i, 0, 0)),
                  parts,
                  pl.BlockSpec((None, 2, 2 * S5_BATCH, S5_PAIR), lambda i: (layer, 0, 0, i))],
        out_specs=pl.BlockSpec((2, rows, S5_ROW), lambda i: (i, 0, 0)),
        scratch_shapes=[states, states, states],
        compiler_params=_params("arbitrary"),
        name="s5_pair",
    )(u_rows, w_parts, toeplitz, v_parts, a_step)


def _merge_kernel(xc_ref, xl_ref, mod_ref, hn_ref, hf_ref, hb_ref, ys_ref, wo_ref, gmh_ref, gsgu_ref,
                  wsgu_ref, bsgu_ref, wglu_ref, bglu_ref, wsc_ref, wum_ref, wug_ref, wus_ref,
                  wuc_ref, wout_ref, o_ref, ybuf_ref, *, ctx_len, tile0, one_source):
    t = pl.program_id(1) + tile0
    mod = mod_ref[0, 0]
    h = hn_ref[0]
    tm = h.shape[0]

    def proj(start, width):
        return lax.dot_general(h, wo_ref[0, start:start + width, :], CONTRACT_LAST,
                               preferred_element_type=F32)

    w = MLSTM_WIDTH
    z_all = proj(0, 6 * w)
    z_o, z_su, z_sv, z_cb, z_cc, z_cx = (z_all[:, i * w:(i + 1) * w] for i in range(6))
    ys = jax.nn.gelu(_from_group_rows(ys_ref, ybuf_ref))
    z_glu = jnp.dot(ys.astype(BF16), wglu_ref[...], preferred_element_type=F32)

    hm = hf_ref[0] + hb_ref[0]
    parts = []
    for hd in range(HEADS):
        hh = hm[:, hd * HEAD_DIM:(hd + 1) * HEAD_DIM]
        parts.append(hh * lax.rsqrt(jnp.mean(hh * hh, axis=-1, keepdims=True) + RMS_EPS))
    y_a = (jnp.concatenate(parts, axis=-1) * gmh_ref[...]) * _sigmoid(z_o)

    su = jax.nn.gelu(z_su)
    sv = jax.nn.gelu(z_sv)
    mu = jnp.mean(sv, axis=-1, keepdims=True)
    cen = sv - mu
    var = jnp.mean(cen * cen, axis=-1, keepdims=True)
    vn = (cen * lax.rsqrt(var + RMS_EPS) * gsgu_ref[...]).astype(BF16)
    bias = bsgu_ref[...]
    rows = []
    for n in range(tm // SGU_CHUNK):
        cols = []
        for gi in range(SGU_GROUPS):
            blk = vn[n * SGU_CHUNK:(n + 1) * SGU_CHUNK, gi * SGU_GROUP_DIM:(gi + 1) * SGU_GROUP_DIM]
            cols.append(jnp.dot(wsgu_ref[gi], blk, preferred_element_type=F32) + bias[:, gi:gi + 1])
        rows.append(jnp.concatenate(cols, axis=-1))
    y_b = su * jnp.concatenate(rows, axis=0)

    y_c = ys * _sigmoid(z_glu + bglu_ref[...])

    first, last = _row_edges(t, tm, ctx_len)
    y_d = z_cb * _conv3(z_cc * z_cx, wsc_ref[...], first, last)

    acc = None
    for j, (y, up_ref) in enumerate(zip((y_a, y_b, y_c, y_d), (wum_ref, wug_ref, wus_ref, wuc_ref))):
        gate = proj(6 * w + j * D_MODEL, D_MODEL)
        term = _sigmoid(gate) * jnp.dot(y.astype(BF16), up_ref[...], preferred_element_type=F32)
        acc = term if acc is None else acc + term
    mixed = jnp.dot(acc.astype(BF16), wout_ref[...], preferred_element_type=F32)
    o_ref[0] = _stream_tile(t, xc_ref, xl_ref, one_source) + mod[2:3] * mixed


def _merge(stream, s, mods, h_norm, h_fwd, h_bwd, y_s5, lw, layer, ctx_len, tile0):
    b, _, d = stream[0].shape
    tm = TOKEN_TILE
    tiles = s // tm
    stream_specs, stream_args = _stream_specs(stream, tile0)
    tok = lambda width: pl.BlockSpec((1, tm, width), lambda i, t: (i, t + tile0, 0))
    group_rows = pl.BlockSpec((S5_GROUPS, tm // S5_CHUNK, S5_ROW),
                              lambda i, t: (0, i * tiles + t + tile0, 0))
    weights = tuple(lw[name] for name in (
        'g_mh', 'g_sgu', 'w_sgu', 'b_sgu_t', 'w_glu', 'b_glu', 'w_sconv',
        'w_up_mlstm', 'w_up_sgu', 'w_up_s5', 'w_up_sconv', 'w_out'))
    w_t = lw['w_in_t']
    out_side = _row_block(w_t, layer, STATE_DIM, w_t.shape[1] - STATE_DIM)
    return pl.pallas_call(
        functools.partial(_merge_kernel, ctx_len=ctx_len, tile0=tile0, one_source=stream[0] is stream[1]),
        out_shape=jax.ShapeDtypeStruct((b, s - tile0 * tm, d), F32),
        grid=(b, tiles - tile0),
        in_specs=stream_specs
                 + [_mod_spec(d, layer, tile0), tok(d), tok(MLSTM_WIDTH), tok(MLSTM_WIDTH), group_rows, out_side]
                 + [_layer_resident(w, layer) for w in weights],
        out_specs=pl.BlockSpec((1, tm, d), lambda i, t: (i, t, 0)),
        scratch_shapes=[pltpu.VMEM((S5_WIDTH // LANES, tm, LANES), F32)],
        compiler_params=_params("arbitrary", "arbitrary"),
        name="merge",
    )(*stream_args, mods, h_norm, h_fwd, h_bwd, y_s5, w_t, *weights)


def _ffn_kernel(x_ref, mod_ref, gain_ref, wg_ref, wu_ref, wd_ref, gf_ref, o_ref, *, final):
    x = x_ref[0]
    mod = mod_ref[0, 0]
    h = _modulated_norm(x, gain_ref[...], mod[3:4], mod[4:5]).astype(BF16)
    a = jnp.dot(h, wg_ref[...], preferred_element_type=F32)
    b = jnp.dot(h, wu_ref[...], preferred_element_type=F32)
    hid = ((a * _sigmoid(a)) * b).astype(BF16)
    y = x + mod[5:6] * jnp.dot(hid, wd_ref[...], preferred_element_type=F32)
    if final:
        y = y * lax.rsqrt(jnp.mean(y * y, axis=-1, keepdims=True) + RMS_EPS) * gf_ref[...]
    o_ref[0] = y


def _ffn(xs, mods, lw, layer, g_final, tile0, final):
    b, s, d = xs.shape
    tm = 2 * TOKEN_TILE if tile0 > 0 and s % (2 * TOKEN_TILE) == 0 else TOKEN_TILE
    weights = tuple(lw[name] for name in ('gain_ffn', 'w_ffn_gate', 'w_ffn_up', 'w_ffn_down'))
    return pl.pallas_call(
        functools.partial(_ffn_kernel, final=final),
        out_shape=jax.ShapeDtypeStruct((b, s, d), F32),
        grid=(b, s // tm),
        in_specs=[pl.BlockSpec((1, tm, d), lambda i, t: (i, t, 0)), _mod_spec(d, layer, tile0)]
                 + [_layer_resident(w, layer) for w in weights] + [_resident((1, d))],
        out_specs=pl.BlockSpec((1, tm, d), lambda i, t: (i, t, 0)),
        compiler_params=_params("arbitrary", "arbitrary"),
        name="ffn_final" if final else "ffn",
    )(xs, mods, *weights, g_final)


def _stacked_weights(g_norm_mix, g_norm_ffn, w_in, b_gates, w_conv_qk, g_mh, g_sgu, w_sgu, b_sgu, w_glu,
                     b_glu, w_sconv, w_up_mlstm, w_up_sgu, w_up_s5, w_up_sconv, w_out,
                     w_ffn_gate, w_ffn_up, w_ffn_down):
    depth = w_in.shape[0]
    row = lambda a: a[:, None, :]
    return dict(
        gain_mix=row(g_norm_mix),
        gain_ffn=row(g_norm_ffn),
        w_in_t=jnp.swapaxes(w_in, 1, 2).astype(BF16),
        b_gates_t=jnp.broadcast_to(b_gates[:, :, None], (depth, 4 * HEADS, TOKEN_TILE)),
        w_conv=w_conv_qk.reshape(depth, 2 * CONV_K, MLSTM_WIDTH),
        g_mh=row(g_mh),
        g_sgu=row(g_sgu),
        w_sgu=w_sgu.astype(BF16),
        b_sgu_t=jnp.swapaxes(b_sgu, 1, 2),
        w_glu=w_glu.astype(BF16),
        b_glu=row(b_glu),
        w_sconv=w_sconv,
        w_up_mlstm=w_up_mlstm.astype(BF16),
        w_up_sgu=w_up_sgu.astype(BF16),
        w_up_s5=w_up_s5.astype(BF16),
        w_up_sconv=w_up_sconv.astype(BF16),
        w_out=w_out.astype(BF16),
        w_ffn_gate=w_ffn_gate.astype(BF16),
        w_ffn_up=w_ffn_up.astype(BF16),
        w_ffn_down=w_ffn_down.astype(BF16),
    )


def kernel(x, c, ctx, c_ctx, w_mod, b_mod, g_norm_mix, g_norm_ffn, w_in, b_gates, w_conv_qk, g_mh, g_sgu, w_sgu, b_sgu, s5_a_re, s5_a_im, s5_log_dt, s5_b_re, s5_b_im, s5_c_re, s5_c_im, s5_d, w_glu, b_glu, w_sconv, w_up_mlstm, w_up_sgu, w_up_s5, w_up_sconv, w_out, w_ffn_gate, w_ffn_up, w_ffn_down, g_final):
    batch, seq, d = x.shape
    ctx_len = ctx.shape[1]
    depth = w_mod.shape[0]
    assert d == D_MODEL and ctx_len == TOKEN_TILE and seq % TOKEN_TILE == 0 and batch == S5_BATCH
    ctx_tiles = ctx_len // TOKEN_TILE

    c_rows = jnp.zeros((8, d), F32).at[:batch].set(c).at[batch].set(c_ctx)
    mod_all = _modulation(c_rows, w_mod, b_mod).reshape(depth, 8, 6, d)
    stream = (ctx, x, ctx_tiles)
    stream_len = ctx_len + seq
    s5_ops = _s5_operators(s5_a_re, s5_a_im, s5_log_dt, s5_b_re, s5_b_im, s5_c_re, s5_c_im, s5_d)
    lw = _stacked_weights(g_norm_mix, g_norm_ffn, w_in, b_gates, w_conv_qk, g_mh, g_sgu, w_sgu, b_sgu,
                          w_glu, b_glu, w_sconv, w_up_mlstm, w_up_sgu, w_up_s5, w_up_sconv, w_out,
                          w_ffn_gate, w_ffn_up, w_ffn_down)

    mod_x = mod_all[:, :batch]
    mod_c = jnp.broadcast_to(mod_all[:, batch][:, None], mod_x.shape)
    mods = jnp.stack([mod_c, mod_x], axis=2)

    out = None
    for l in range(depth):
        last = l == depth - 1
        h_norm, qt, k, vt, gates_t, u_rows = _state_proj(stream, stream_len, mods, lw, l, ctx_len)
        h_fwd, h_bwd = _mlstm(qt, k, vt, gates_t, ctx_len)
        y_s5 = _s5_mixer(u_rows, s5_ops, l, stream_len // S5_CHUNK, ctx_len // S5_CHUNK)
        tile0 = ctx_tiles if last else 0
        x_mid = _merge(stream, stream_len, mods, h_norm, h_fwd, h_bwd, y_s5, lw, l, ctx_len, tile0)
        out = _ffn(x_mid, mods, lw, l, g_final.reshape(1, d), tile0, last)
        stream = (out, out, 0)
    return out
```

```python
import functools

import jax
import jax.numpy as jnp
from jax import lax
from jax.experimental import pallas as pl
from jax.experimental.pallas import tpu as pltpu

F32 = jnp.float32
BF16 = jnp.bfloat16

D_MODEL = 1024
GRID_W = 64
RMS_EPS = 1e-6
CONV_K = 3
HEADS = 4
HEAD_DIM = 128
MLSTM_WIDTH = HEADS * HEAD_DIM
MLSTM_CHUNK = 128
MLSTM_BATCH = 4
SGU_GROUPS = 4
SGU_GROUP_DIM = 128
SGU_CHUNK = 128
S5_GROUP_DIM = 16
S5_GROUPS = 24
S5_WIDTH = S5_GROUPS * S5_GROUP_DIM
S5_STATE = 64
STATE_DIM = 3 * MLSTM_WIDTH + 4 * HEADS + S5_WIDTH
GATE_COL = 3 * MLSTM_WIDTH

LANES = 128
TOKEN_TILE = 256
S5_CHUNK = 16
S5_ROW = S5_CHUNK * S5_GROUP_DIM
S5_PAIR = 2 * S5_STATE
S5_BATCH = 4
V7X_VMEM_BYTES = 64 * 1024 * 1024
VMEM_LIMIT = V7X_VMEM_BYTES - 12 * 1024 * 1024
CONTRACT_LAST = (((1,), (1,)), ((), ()))


def _resident(shape):
    zeros = (0,) * len(shape)
    return pl.BlockSpec(shape, lambda *_: zeros, pipeline_mode=pl.Buffered(1))


def _layer_resident(stacked, layer):
    zeros = (0,) * (stacked.ndim - 1)
    return pl.BlockSpec((None,) + stacked.shape[1:], lambda *_: (layer,) + zeros,
                        pipeline_mode=pl.Buffered(1))


def _row_block(stacked, layer, start, size):
    cols = stacked.shape[-1]
    return pl.BlockSpec((pl.Element(1), pl.Element(size), pl.Element(cols)), lambda *_: (layer, start, 0),
                        pipeline_mode=pl.Buffered(1))


def _mod_spec(d, layer, tile0):
    return pl.BlockSpec((None, 1, 1, 6, d), lambda i, t: (layer, i, jnp.minimum(t + tile0, 1), 0, 0))


def _params(*semantics, flags=None):
    return pltpu.CompilerParams(dimension_semantics=semantics, vmem_limit_bytes=VMEM_LIMIT, flags=flags)


def _sigmoid(x):
    return 0.5 * jnp.tanh(0.5 * x) + 0.5


def _modulated_norm(x, gain, shift, scale):
    y = x * lax.rsqrt(jnp.mean(x * x, axis=-1, keepdims=True) + RMS_EPS) * gain
    return y * (1.0 + scale) + shift


def _conv3(a, w, first, last):
    n = a.shape[0]
    prev = jnp.where(first, 0.0, pltpu.roll(a, 1, 0))
    nxt = jnp.where(last, 0.0, pltpu.roll(a, n - 1, 0))
    return prev * w[0:1] + a * w[1:2] + nxt * w[2:3]


def _conv3_lanes(a, w_ref, tile_idx, ctx_len):
    n = a.shape[1]
    period = jnp.where(tile_idx == 0, ctx_len, GRID_W)
    pos = lax.broadcasted_iota(jnp.int32, (1, n), 1) & (period - 1)
    prev = jnp.where(pos == 0, 0.0, pltpu.roll(a, 1, 1))
    nxt = jnp.where(pos == period - 1, 0.0, pltpu.roll(a, n - 1, 1))
    return prev * w_ref[0] + a * w_ref[1] + nxt * w_ref[2]


def _row_edges(tile_idx, n_rows, ctx_len):
    period = jnp.where(tile_idx == 0, ctx_len, GRID_W)
    pos = lax.broadcasted_iota(jnp.int32, (n_rows, 1), 0) & (period - 1)
    return pos == 0, pos == period - 1


def _mod_kernel(c_ref, w_ref, b_ref, o_ref):
    a = c_ref[...]
    a = a * _sigmoid(a)
    a_hi = a.astype(BF16)
    a_lo = (a - a_hi.astype(F32)).astype(BF16)
    w = w_ref[0]
    w_hi = w.astype(BF16)
    w_lo = (w - w_hi.astype(F32)).astype(BF16)
    acc = jnp.dot(a_hi, w_hi, preferred_element_type=F32)
    acc = acc + jnp.dot(a_lo, w_hi, preferred_element_type=F32)
    acc = acc + jnp.dot(a_hi, w_lo, preferred_element_type=F32)
    o_ref[0] = acc + b_ref[0]


def _modulation(c_rows, w_mod, b_mod):
    depth, d, n = w_mod.shape
    tn = 1536
    return pl.pallas_call(
        _mod_kernel,
        out_shape=jax.ShapeDtypeStruct((depth, 8, n), F32),
        grid=(depth, n // tn),
        in_specs=[
            pl.BlockSpec((8, d), lambda l, j: (0, 0)),
            pl.BlockSpec((1, d, tn), lambda l, j: (l, 0, j)),
            pl.BlockSpec((1, 1, tn), lambda l, j: (l, 0, j)),
        ],
        out_specs=pl.BlockSpec((1, 8, tn), lambda l, j: (l, 0, j)),
        compiler_params=_params("arbitrary", "arbitrary"),
        name="modulation",
    )(c_rows, w_mod, b_mod.reshape(depth, 1, n))


def _swap_lane_blocks(tiles):
    n = LANES // S5_GROUP_DIM
    tiles = list(tiles)
    block = lax.broadcasted_iota(jnp.int32, tiles[0].shape, 1) // S5_GROUP_DIM
    dist = n // 2
    while dist:
        keep = (block & dist) == 0
        for i in range(n):
            if i & dist:
                continue
            a, b = tiles[i], tiles[i + dist]
            tiles[i] = jnp.where(keep, a, pltpu.roll(b, dist * S5_GROUP_DIM, 1))
            tiles[i + dist] = jnp.where(keep, pltpu.roll(a, LANES - dist * S5_GROUP_DIM, 1), b)
        dist //= 2
    return tiles


def _to_group_rows(z, u_ref, buf_ref):
    n_chunks = z.shape[0] // S5_CHUNK
    per_block = LANES // S5_GROUP_DIM
    for gb in range(S5_WIDTH // LANES):
        buf_ref[gb] = z[:, gb * LANES:(gb + 1) * LANES]
    for gb in range(S5_WIDTH // LANES):
        for half in range(S5_CHUNK // per_block):
            by_token = [buf_ref[gb, pl.ds(half * per_block + s, n_chunks, stride=S5_CHUNK), :]
                        for s in range(per_block)]
            for gl, tile in enumerate(_swap_lane_blocks(by_token)):
                u_ref[gb * per_block + gl, :, half * LANES:(half + 1) * LANES] = tile


def _from_group_rows(y_ref, buf_ref):
    n_chunks = y_ref.shape[1]
    per_block = LANES // S5_GROUP_DIM
    for gb in range(S5_WIDTH // LANES):
        for half in range(S5_CHUNK // per_block):
            by_group = [y_ref[gb * per_block + gl, :, half * LANES:(half + 1) * LANES]
                        for gl in range(per_block)]
            for s, tile in enumerate(_swap_lane_blocks(by_group)):
                buf_ref[gb, pl.ds(half * per_block + s, n_chunks, stride=S5_CHUNK), :] = tile
    return jnp.concatenate([buf_ref[gb] for gb in range(S5_WIDTH // LANES)], axis=-1)


def _stream_specs(stream, tile0):
    ctx_src, lat_src, offset = stream
    d = ctx_src.shape[-1]
    specs = [pl.BlockSpec((1, TOKEN_TILE, d), lambda i, t: (i, 0, 0)),
             pl.BlockSpec((1, TOKEN_TILE, d), lambda i, t: (i, jnp.maximum(t + tile0 - offset, 0), 0))]
    return specs, (ctx_src, lat_src)


def _stream_tile(t, ctx_ref, lat_ref, one_source):
    return lat_ref[0] if one_source else jnp.where(t == 0, ctx_ref[0], lat_ref[0])


def _state_proj_kernel(xc_ref, xl_ref, mod_ref, gain_ref, wq_ref, wk_ref, wv_ref, wg_ref, wu_ref, bgt_ref,
                       wcq_ref, wck_ref, hn_ref, qt_ref, k_ref, vt_ref, gt_ref, u_ref, ubuf_ref,
                       *, ctx_len, one_source):
    t = pl.program_id(1)
    mod = mod_ref[0, 0]
    x = _stream_tile(t, xc_ref, xl_ref, one_source)
    h = _modulated_norm(x, gain_ref[...], mod[0:1], mod[1:2]).astype(BF16)
    hn_ref[0] = h
    z_qt = lax.dot_general(wq_ref[0], h, CONTRACT_LAST, preferred_element_type=F32)
    z_k = lax.dot_general(h, wk_ref[0], CONTRACT_LAST, preferred_element_type=F32)
    z_u = lax.dot_general(h, wu_ref[0], CONTRACT_LAST, preferred_element_type=F32)
    vt_ref[0] = lax.dot_general(wv_ref[0], h, CONTRACT_LAST, preferred_element_type=F32).astype(BF16)
    gt_ref[0] = lax.dot_general(wg_ref[0], h, CONTRACT_LAST, preferred_element_type=F32) + bgt_ref[...]
    qt = _conv3_lanes(z_qt, wcq_ref, t, ctx_len)
    qt_ref[0] = (qt * _sigmoid(qt)).astype(BF16)
    first, last = _row_edges(t, z_k.shape[0], ctx_len)
    k = _conv3(z_k, wck_ref[...], first, last)
    k_ref[0] = ((k * _sigmoid(k)) * (HEAD_DIM ** -0.5)).astype(BF16)
    _to_group_rows(z_u, u_ref, ubuf_ref)


def _state_proj(stream, s, mods, lw, layer, ctx_len):
    b, _, d = stream[0].shape
    tm = TOKEN_TILE
    tiles = s // tm
    stream_specs, stream_args = _stream_specs(stream, 0)
    tok = lambda width: pl.BlockSpec((1, tm, width), lambda i, t: (i, t, 0))
    chan = lambda height: pl.BlockSpec((1, height, tm), lambda i, t: (i, 0, t))
    w_t = lw['w_in_t']
    column_ranges = ((0, MLSTM_WIDTH), (MLSTM_WIDTH, MLSTM_WIDTH), (2 * MLSTM_WIDTH, MLSTM_WIDTH),
                     (GATE_COL, 4 * HEADS), (GATE_COL + 4 * HEADS, S5_WIDTH))
    return pl.pallas_call(
        functools.partial(_state_proj_kernel, ctx_len=ctx_len, one_source=stream[0] is stream[1]),
        out_shape=(
            jax.ShapeDtypeStruct((b, s, d), BF16),
            jax.ShapeDtypeStruct((b, MLSTM_WIDTH, s), BF16),
            jax.ShapeDtypeStruct((b, s, MLSTM_WIDTH), BF16),
            jax.ShapeDtypeStruct((b, MLSTM_WIDTH, s), BF16),
            jax.ShapeDtypeStruct((b, 4 * HEADS, s), F32),
            jax.ShapeDtypeStruct((S5_GROUPS, b * s // S5_CHUNK, S5_ROW), F32),
        ),
        grid=(b, s // tm),
        in_specs=stream_specs + [_mod_spec(d, layer, 0), _layer_resident(lw['gain_mix'], layer)]
                 + [_row_block(w_t, layer, start, size) for start, size in column_ranges]
                 + [_layer_resident(lw[name], layer) for name in ('b_gates_t', 'w_conv_q_t', 'w_conv_k')],
        out_specs=(tok(d), chan(MLSTM_WIDTH), tok(MLSTM_WIDTH), chan(MLSTM_WIDTH), chan(4 * HEADS),
                   pl.BlockSpec((S5_GROUPS, tm // S5_CHUNK, S5_ROW), lambda i, t: (0, i * tiles + t, 0))),
        scratch_shapes=[pltpu.VMEM((S5_WIDTH // LANES, tm, LANES), F32)],
        compiler_params=_params("arbitrary", "arbitrary"),
        name="state_proj",
    )(*stream_args, mods, lw['gain_mix'], w_t, w_t, w_t, w_t, w_t, lw['b_gates_t'], lw['w_conv_q_t'],
      lw['w_conv_k'])


def _mlstm_chunk(direction, bi, qt_ref, k_ref, vt_ref, gt_ref, h_ref, ct_ref, n_ref, m_ref):
    lc = MLSTM_CHUNK
    sign = 1 - 2 * direction
    row = lax.broadcasted_iota(jnp.int32, (lc, lc), 0)
    col = lax.broadcasted_iota(jnp.int32, (lc, lc), 1)
    tri = (col - row) * sign >= 0
    tri_b = jnp.where(tri, 1.0, 0.0).astype(BF16)

    gt = gt_ref[bi]
    lf_t = jax.nn.log_sigmoid(gt)
    hi = lf_t.astype(BF16)
    lo = (lf_t - hi.astype(F32)).astype(BF16)
    bcum_t = (jnp.dot(hi, tri_b, preferred_element_type=F32)
              + jnp.dot(lo, tri_b, preferred_element_type=F32))
    last = lc - 1 if direction == 0 else 0

    qt = qt_ref[bi]
    k = k_ref[bi]
    vt = vt_ref[bi]
    heads = []
    for hd in range(HEADS):
        sl = slice(hd * HEAD_DIM, (hd + 1) * HEAD_DIM)
        b_row = bcum_t[HEADS + hd:HEADS + hd + 1, :]
        heads.append(dict(
            bi=bi, sl=sl, qt=qt[sl, :], k=k[:, sl], vt=vt[sl, :], tri=tri,
            a_row=gt[hd:hd + 1, :] - b_row, b_row=b_row, b_last=b_row[:, last:last + 1],
            m=m_ref[hd][:, 0:1], ct=ct_ref[hd][...], n=n_ref[hd][...],
            h_ref=h_ref, ct_ref=ct_ref[hd], n_ref=n_ref[hd], m_ref=m_ref[hd]))
    return heads


def _mlstm_scores(hd):
    hd['kq'] = jnp.dot(hd['k'], hd['qt'], preferred_element_type=F32)
    n_rows = jnp.broadcast_to(hd['n'], (8, HEAD_DIM)).astype(BF16)
    hd['qn'] = jnp.dot(n_rows, hd['qt'], preferred_element_type=F32)[0:1]
    hd['cq'] = jnp.dot(hd['ct'].astype(BF16), hd['qt'], preferred_element_type=F32)


def _mlstm_weights(hd):
    lc = MLSTM_CHUNK
    a_row, b_row, b_last, m = hd['a_row'], hd['b_row'], hd['b_last'], hd['m']
    m_new = b_last + jnp.maximum(m, jnp.max(a_row, axis=1, keepdims=True))
    w_row = jnp.exp(a_row + (b_last - m_new))
    hd['decay'] = jnp.exp(b_last + m - m_new)
    hd['m_new'] = m_new
    hd['vt_w'] = (hd['vt'].astype(F32) * w_row).astype(BF16)
    hd['w_rows'] = jnp.broadcast_to(w_row, (8, lc)).astype(BF16)

    a_bc = jnp.broadcast_to(a_row, (lc, lc)).T
    log_w = jnp.where(hd['tri'], a_bc + b_row, -jnp.inf)
    log_inter = b_row + m
    m_t = jnp.maximum(log_inter, jnp.max(log_w, axis=0, keepdims=True))
    hd['inter'] = jnp.exp(log_inter - m_t)
    s_t = hd['kq'] * jnp.exp(log_w - m_t)
    den = hd['inter'] * hd['qn'] + jnp.sum(s_t, axis=0, keepdims=True)
    hd['r_den'] = 1.0 / jnp.maximum(jnp.abs(den), jnp.exp(-m_t))
    hd['s_t'] = s_t.astype(BF16)


def _mlstm_outputs(hd):
    num = jnp.dot(hd['vt'], hd['s_t'], preferred_element_type=F32) + hd['inter'] * hd['cq']
    hd['h_ref'][hd['bi'], :, hd['sl']] = (num * hd['r_den']).T
    hd['ct_ref'][...] = hd['decay'] * hd['ct'] + jnp.dot(hd['vt_w'], hd['k'], preferred_element_type=F32)
    hd['n_ref'][...] = (hd['decay'] * hd['n']
                        + jnp.dot(hd['w_rows'], hd['k'], preferred_element_type=F32)[0:1])
    hd['m_ref'][...] = jnp.broadcast_to(hd['m_new'], (1, LANES))


def _mlstm_kernel(qtf_ref, kf_ref, vtf_ref, gtf_ref, qtb_ref, kb_ref, vtb_ref, gtb_ref,
                  hf_ref, hb_ref, *state_refs):
    per = MLSTM_BATCH * 2 * HEADS
    ct_refs, n_refs, m_refs = state_refs[:per], state_refs[per:2 * per], state_refs[2 * per:]

    @pl.when(pl.program_id(1) == 0)
    def _():
        for ref in state_refs:
            ref[...] = jnp.zeros_like(ref)

    heads = []
    for bi in range(MLSTM_BATCH):
        for direction, refs in enumerate(((qtf_ref, kf_ref, vtf_ref, gtf_ref, hf_ref),
                                          (qtb_ref, kb_ref, vtb_ref, gtb_ref, hb_ref))):
            own = slice((2 * bi + direction) * HEADS, (2 * bi + direction + 1) * HEADS)
            heads += _mlstm_chunk(direction, bi, *refs, ct_refs[own], n_refs[own], m_refs[own])
    for phase in (_mlstm_scores, _mlstm_weights, _mlstm_outputs):
        for hd in heads:
            phase(hd)


def _mlstm(qt, k, vt, gates_t, ctx_len):
    b, s, w = k.shape
    lc = MLSTM_CHUNK
    n_chunks = s // lc
    n_ctx = ctx_len // lc

    def back(j):
        return jnp.where(j < n_ctx, n_ctx - 1 - j, n_chunks + n_ctx - 1 - j)

    nb = MLSTM_BATCH

    def specs(direction, chunk):
        return [pl.BlockSpec((nb, w, lc), lambda i, j: (i, 0, chunk(j))),
                pl.BlockSpec((nb, lc, w), lambda i, j: (i, chunk(j), 0)),
                pl.BlockSpec((nb, w, lc), lambda i, j: (i, 0, chunk(j))),
                pl.BlockSpec((nb, 2 * HEADS, lc), lambda i, j: (i, direction, chunk(j)))]

    fwd = lambda j: j
    out = jax.ShapeDtypeStruct((b, s, w), F32)
    chains = nb * 2 * HEADS
    return pl.pallas_call(
        _mlstm_kernel,
        out_shape=(out, out),
        grid=(b // nb, n_chunks),
        in_specs=specs(0, fwd) + specs(1, back),
        out_specs=(pl.BlockSpec((nb, lc, w), lambda i, j: (i, j, 0)),
                   pl.BlockSpec((nb, lc, w), lambda i, j: (i, back(j), 0))),
        scratch_shapes=([pltpu.VMEM((HEAD_DIM, HEAD_DIM), F32)] * chains
                        + [pltpu.VMEM((1, HEAD_DIM), F32)] * chains
                        + [pltpu.VMEM((1, LANES), F32)] * chains),
        compiler_params=_params("arbitrary", "arbitrary"),
        name="mlstm",
    )(qt, k, vt, gates_t, qt, k, vt, gates_t)


def _s5_operators(a_re, a_im, log_dt, b_re, b_im, c_re, c_im, d_skip):
    hp = lax.Precision.HIGH
    lch = S5_CHUNK
    depth = a_re.shape[0]
    a_re = a_re.astype(F32)
    a_im = a_im.astype(F32)
    dt = jnp.exp(log_dt.astype(F32))[..., None]
    la_re, la_im = dt * a_re, dt * a_im
    mag = jnp.exp(la_re)
    ab_re, ab_im = mag * jnp.cos(la_im), mag * jnp.sin(la_im)
    nr, ni = ab_re - 1.0, ab_im
    den = a_re * a_re + a_im * a_im
    f_re = ((nr * a_re + ni * a_im) / den)[:, :, :, None, :]
    f_im = ((ni * a_re - nr * a_im) / den)[:, :, :, None, :]
    bt_re = jnp.swapaxes(b_re.astype(F32), -1, -2)[:, None]
    bt_im = jnp.swapaxes(b_im.astype(F32), -1, -2)[:, None]
    bb_re = f_re * bt_re - f_im * bt_im
    bb_im = f_re * bt_im + f_im * bt_re
    c_re = c_re.astype(F32)
    c_im = c_im.astype(F32)

    n = jnp.arange(-(lch - 1), lch + 1, dtype=F32)[:, None]
    mg = jnp.exp(n * la_re[:, :, :, None, :])
    ang = n * la_im[:, :, :, None, :]
    pw_re, pw_im = mg * jnp.cos(ang), mg * jnp.sin(ang)
    zero = lch - 1

    def powers(direction, first, step):
        start = zero + first
        stop = start + step * lch
        sl = slice(start, stop if stop >= 0 else None, step)
        return pw_re[:, direction, :, sl, None, :], pw_im[:, direction, :, sl, None, :]

    def times(x_re, x_im, p):
        return x_re * p[0] - x_im * p[1], x_re * p[1] + x_im * p[0]

    def rows256(x):
        return x.reshape(depth, S5_GROUPS, S5_ROW, S5_STATE)

    bbf = (bb_re[:, 0, :, None], bb_im[:, 0, :, None])
    bbb = (bb_re[:, 1, :, None], bb_im[:, 1, :, None])
    cc = (c_re[:, :, None], c_im[:, :, None])

    def response(bb, direction, sign):
        l_re, l_im = times(*bb, powers(direction, 0, -sign))
        r_re, r_im = times(*cc, powers(direction, 0, sign))
        left = jnp.concatenate([rows256(l_re), rows256(l_im)], axis=-1)
        right = jnp.concatenate([rows256(r_re), -rows256(r_im)], axis=-1)
        return jnp.einsum('lgap,lgbp->lgab', left, right, precision=hp)

    src = (jnp.arange(S5_ROW) // S5_GROUP_DIM)[:, None]
    tgt = (jnp.arange(S5_ROW) // S5_GROUP_DIM)[None, :]
    d_rows = jnp.tile(d_skip.astype(F32).reshape(depth, S5_GROUPS, 1, S5_GROUP_DIM), (1, 1, 1, lch))
    toeplitz = (jnp.where(tgt >= src, response(bbf, 0, 1), 0.0)
                + jnp.where(src >= tgt, response(bbb, 1, -1), 0.0)
                + jnp.eye(S5_ROW, dtype=F32) * d_rows)

    def state_map(x, p):
        re, im = times(*x, p)
        return rows256(re), rows256(im)

    w_parts = jnp.stack(state_map(bbf, powers(0, lch - 1, -1)) + state_map(bbb, powers(1, 0, 1)), axis=1)
    vf_re, vf_im = state_map(cc, powers(0, 1, 1))
    vb_re, vb_im = state_map(cc, powers(1, lch, -1))
    v_parts = jnp.stack([vf_re, -vf_im, vb_re, -vb_im], axis=1)

    al_re, al_im = pw_re[:, :, :, zero + lch], pw_im[:, :, :, zero + lch]
    def rows(a):
        a = a.reshape(depth, 2, 1, S5_GROUPS * S5_STATE)
        return jnp.broadcast_to(a, (depth, 2, S5_BATCH, S5_GROUPS * S5_STATE)).reshape(depth, 2 * S5_BATCH, -1)
    a_step = jnp.stack([rows(al_re), rows(al_im)], axis=1)
    return toeplitz.astype(BF16), w_parts, v_parts, a_step


def _s5_pair_matrix(parts_ref, re, im):
    z = jnp.zeros((S5_ROW, S5_STATE), F32)
    top = jnp.concatenate([parts_ref[re, 0], z, parts_ref[im, 0], z], axis=-1)
    bottom = jnp.concatenate([z, parts_ref[re, 1], z, parts_ref[im, 1]], axis=-1)
    return jnp.concatenate([top, bottom], axis=0).astype(BF16)


def _s5_chunk_state_kernel(u_ref, w_ref, x_ref, *, n_chunks):
    u = jnp.concatenate([u_ref[0], u_ref[1]], axis=-1).astype(BF16)
    w = jnp.concatenate([_s5_pair_matrix(w_ref, 0, 1), _s5_pair_matrix(w_ref, 2, 3)], axis=-1)
    x = jnp.dot(u, w, preferred_element_type=F32)
    for b in range(S5_BATCH):
        xb = x[b * n_chunks:(b + 1) * n_chunks]
        for direction in range(2):
            for part in range(2):
                lo = (2 * direction + part) * S5_PAIR
                x_ref[part, pl.ds(direction * S5_BATCH + b, n_chunks, stride=2 * S5_BATCH), :] = (
                    xb[:, lo:lo + S5_PAIR])


def _s5_scan_kernel(x_ref, a_ref, of_ref, ob_ref, *, n_chunks, n_ctx):
    tile = 2 * S5_BATCH
    n_slabs = x_ref.shape[0]
    a_re = a_ref[0]
    a_im = a_ref[1]
    fwd_rows = lax.broadcasted_iota(jnp.int32, (tile, S5_PAIR), 0) < S5_BATCH

    def body(i, state):
        cb = jnp.where(i < n_ctx, n_ctx - 1 - i, n_chunks + n_ctx - 1 - i)
        rf = pl.multiple_of(i * tile, tile)
        rb = pl.multiple_of(cb * tile, tile)
        new_state = []
        for p in range(n_slabs // 2):
            re, im = state[2 * p], state[2 * p + 1]
            for j, val in ((2 * p, re), (2 * p + 1, im)):
                of_ref[j, pl.ds(rf, tile), :] = val
                ob_ref[j, pl.ds(rb, tile), :] = val
            in_re = jnp.where(fwd_rows, x_ref[2 * p, pl.ds(rf, tile), :], x_ref[2 * p, pl.ds(rb, tile), :])
            in_im = jnp.where(fwd_rows, x_ref[2 * p + 1, pl.ds(rf, tile), :],
                              x_ref[2 * p + 1, pl.ds(rb, tile), :])
            ar, ai = a_re[:, p * S5_PAIR:(p + 1) * S5_PAIR], a_im[:, p * S5_PAIR:(p + 1) * S5_PAIR]
            new_state.append(ar * re - ai * im + in_re)
            new_state.append(ar * im + ai * re + in_im)
        return tuple(new_state)

    zero = jnp.zeros((tile, S5_PAIR), F32)
    lax.fori_loop(0, n_chunks, body, (zero,) * n_slabs, unroll=4)


def _s5_output_kernel(u_ref, t_ref, xf_ref, xb_ref, v_ref, y_ref, *, n_chunks):
    stride = 2 * S5_BATCH

    def batch_rows(ref, first_row):
        return jnp.concatenate(
            [jnp.concatenate([ref[part, pl.ds(first_row + b, n_chunks, stride=stride), :]
                              for part in range(2)], axis=-1) for b in range(S5_BATCH)], axis=0)

    xf = batch_rows(xf_ref, 0)
    xb = batch_rows(xb_ref, S5_BATCH)
    last = (((1,), (1,)), ((), ()))
    y = lax.dot_general(xf.astype(BF16), _s5_pair_matrix(v_ref, 0, 1), last, preferred_element_type=F32)
    y = y + lax.dot_general(xb.astype(BF16), _s5_pair_matrix(v_ref, 2, 3), last, preferred_element_type=F32)
    for a in range(2):
        within = jnp.dot(u_ref[a].astype(BF16), t_ref[a], preferred_element_type=F32)
        y_ref[a] = within + y[:, a * S5_ROW:(a + 1) * S5_ROW]


def _s5_pair_kernel(u_ref, w_ref, t_ref, v_ref, a_ref, y_ref, xc_ref, xf_ref, xb_ref, *, n_chunks, n_ctx):
    _s5_chunk_state_kernel(u_ref, w_ref, xc_ref, n_chunks=n_chunks)
    _s5_scan_kernel(xc_ref, a_ref, xf_ref, xb_ref, n_chunks=n_chunks, n_ctx=n_ctx)
    _s5_output_kernel(u_ref, t_ref, xf_ref, xb_ref, v_ref, y_ref, n_chunks=n_chunks)


def _s5_mixer(u_rows, ops, layer, n_chunks, n_ctx):
    toeplitz, w_parts, v_parts, a_step = ops
    g, rows, _ = u_rows.shape
    n_pairs = g // 2
    states = pltpu.VMEM((2, n_chunks * 2 * S5_BATCH, S5_PAIR), F32)
    parts = pl.BlockSpec((None, 4, 2, S5_ROW, S5_STATE), lambda i: (layer, 0, i, 0, 0))
    return pl.pallas_call(
        functools.partial(_s5_pair_kernel, n_chunks=n_chunks, n_ctx=n_ctx),
        out_shape=jax.ShapeDtypeStruct((g, rows, S5_ROW), F32),
        grid=(n_pairs,),
        in_specs=[pl.BlockSpec((2, rows, S5_ROW), lambda i: (i, 0, 0)),
                  parts,
                  pl.BlockSpec((None, 2, S5_ROW, S5_ROW), lambda i: (layer, i, 0, 0)),
                  parts,
                  pl.BlockSpec((None, 2, 2 * S5_BATCH, S5_PAIR), lambda i: (layer, 0, 0, i))],
        out_specs=pl.BlockSpec((2, rows, S5_ROW), lambda i: (i, 0, 0)),
        scratch_shapes=[states, states, states],
        compiler_params=_params("arbitrary"),
        name="s5_pair",
    )(u_rows, w_parts, toeplitz, v_parts, a_step)


def _merge_kernel(xc_ref, xl_ref, mod_ref, hn_ref, hf_ref, hb_ref, ys_ref, wo_ref, gmh_ref, gsgu_ref,
                  wsgu_ref, bsgu_ref, wglu_ref, bglu_ref, wsc_ref, wum_ref, wug_ref, wus_ref,
                  wuc_ref, wout_ref, o_ref, ybuf_ref, *, ctx_len, tile0, one_source):
    t = pl.program_id(1) + tile0
    mod = mod_ref[0, 0]
    h = hn_ref[0]
    tm = h.shape[0]

    def proj(start, width):
        return lax.dot_general(h, wo_ref[0, start:start + width, :], CONTRACT_LAST,
                               preferred_element_type=F32)

    w = MLSTM_WIDTH
    z_all = proj(0, 6 * w)
    z_o, z_su, z_sv, z_cb, z_cc, z_cx = (z_all[:, i * w:(i + 1) * w] for i in range(6))
    ys = jax.nn.gelu(_from_group_rows(ys_ref, ybuf_ref))
    z_glu = jnp.dot(ys.astype(BF16), wglu_ref[...], preferred_element_type=F32)

    hm = hf_ref[0] + hb_ref[0]
    parts = []
    for hd in range(HEADS):
        hh = hm[:, hd * HEAD_DIM:(hd + 1) * HEAD_DIM]
        parts.append(hh * lax.rsqrt(jnp.mean(hh * hh, axis=-1, keepdims=True) + RMS_EPS))
    y_a = (jnp.concatenate(parts, axis=-1) * gmh_ref[...]) * _sigmoid(z_o)

    su = jax.nn.gelu(z_su)
    sv = jax.nn.gelu(z_sv)
    mu = jnp.mean(sv, axis=-1, keepdims=True)
    cen = sv - mu
    var = jnp.mean(cen * cen, axis=-1, keepdims=True)
    vn = (cen * lax.rsqrt(var + RMS_EPS) * gsgu_ref[...]).astype(BF16)
    bias = bsgu_ref[...]
    rows = []
    for n in range(tm // SGU_CHUNK):
        cols = []
        for gi in range(SGU_GROUPS):
            blk = vn[n * SGU_CHUNK:(n + 1) * SGU_CHUNK, gi * SGU_GROUP_DIM:(gi + 1) * SGU_GROUP_DIM]
            cols.append(jnp.dot(wsgu_ref[gi], blk, preferred_element_type=F32) + bias[:, gi:gi + 1])
        rows.append(jnp.concatenate(cols, axis=-1))
    y_b = su * jnp.concatenate(rows, axis=0)

    y_c = ys * _sigmoid(z_glu + bglu_ref[...])

    first, last = _row_edges(t, tm, ctx_len)
    y_d = z_cb * _conv3(z_cc * z_cx, wsc_ref[...], first, last)

    acc = None
    for j, (y, up_ref) in enumerate(zip((y_a, y_b, y_c, y_d), (wum_ref, wug_ref, wus_ref, wuc_ref))):
        gate = proj(6 * w + j * D_MODEL, D_MODEL)
        term = _sigmoid(gate) * jnp.dot(y.astype(BF16), up_ref[...], preferred_element_type=F32)
        acc = term if acc is None else acc + term
    mixed = jnp.dot(acc.astype(BF16), wout_ref[...], preferred_element_type=F32)
    o_ref[0] = _stream_tile(t, xc_ref, xl_ref, one_source) + mod[2:3] * mixed


def _merge(stream, s, mods, h_norm, h_fwd, h_bwd, y_s5, lw, layer, ctx_len, tile0):
    b, _, d = stream[0].shape
    tm = TOKEN_TILE
    tiles = s // tm
    stream_specs, stream_args = _stream_specs(stream, tile0)
    tok = lambda width: pl.BlockSpec((1, tm, width), lambda i, t: (i, t + tile0, 0))
    group_rows = pl.BlockSpec((S5_GROUPS, tm // S5_CHUNK, S5_ROW),
                              lambda i, t: (0, i * tiles + t + tile0, 0))
    weights = tuple(lw[name] for name in (
        'g_mh', 'g_sgu', 'w_sgu', 'b_sgu_t', 'w_glu', 'b_glu', 'w_sconv',
        'w_up_mlstm', 'w_up_sgu', 'w_up_s5', 'w_up_sconv', 'w_out'))
    w_t = lw['w_in_t']
    out_side = _row_block(w_t, layer, STATE_DIM, w_t.shape[1] - STATE_DIM)
    return pl.pallas_call(
        functools.partial(_merge_kernel, ctx_len=ctx_len, tile0=tile0, one_source=stream[0] is stream[1]),
        out_shape=jax.ShapeDtypeStruct((b, s - tile0 * tm, d), F32),
        grid=(b, tiles - tile0),
        in_specs=stream_specs
                 + [_mod_spec(d, layer, tile0), tok(d), tok(MLSTM_WIDTH), tok(MLSTM_WIDTH), group_rows, out_side]
                 + [_layer_resident(w, layer) for w in weights],
        out_specs=pl.BlockSpec((1, tm, d), lambda i, t: (i, t, 0)),
        scratch_shapes=[pltpu.VMEM((S5_WIDTH // LANES, tm, LANES), F32)],
        compiler_params=_params("arbitrary", "arbitrary"),
        name="merge",
    )(*stream_args, mods, h_norm, h_fwd, h_bwd, y_s5, w_t, *weights)


def _ffn_kernel(x_ref, mod_ref, gain_ref, wg_ref, wu_ref, wd_ref, gf_ref, o_ref, *, final):
    x = x_ref[0]
    mod = mod_ref[0, 0]
    h = _modulated_norm(x, gain_ref[...], mod[3:4], mod[4:5]).astype(BF16)
    a = jnp.dot(h, wg_ref[...], preferred_element_type=F32)
    b = jnp.dot(h, wu_ref[...], preferred_element_type=F32)
    hid = ((a * _sigmoid(a)) * b).astype(BF16)
    y = x + mod[5:6] * jnp.dot(hid, wd_ref[...], preferred_element_type=F32)
    if final:
        y = y * lax.rsqrt(jnp.mean(y * y, axis=-1, keepdims=True) + RMS_EPS) * gf_ref[...]
    o_ref[0] = y


def _ffn(xs, mods, lw, layer, g_final, tile0, final):
    b, s, d = xs.shape
    tm = 2 * TOKEN_TILE if tile0 > 0 and s % (2 * TOKEN_TILE) == 0 else TOKEN_TILE
    weights = tuple(lw[name] for name in ('gain_ffn', 'w_ffn_gate', 'w_ffn_up', 'w_ffn_down'))
    return pl.pallas_call(
        functools.partial(_ffn_kernel, final=final),
        out_shape=jax.ShapeDtypeStruct((b, s, d), F32),
        grid=(b, s // tm),
        in_specs=[pl.BlockSpec((1, tm, d), lambda i, t: (i, t, 0)), _mod_spec(d, layer, tile0)]
                 + [_layer_resident(w, layer) for w in weights] + [_resident((1, d))],
        out_specs=pl.BlockSpec((1, tm, d), lambda i, t: (i, t, 0)),
        compiler_params=_params("arbitrary", "arbitrary"),
        name="ffn_final" if final else "ffn",
    )(xs, mods, *weights, g_final)


def _stacked_weights(g_norm_mix, g_norm_ffn, w_in, b_gates, w_conv_qk, g_mh, g_sgu, w_sgu, b_sgu, w_glu,
                     b_glu, w_sconv, w_up_mlstm, w_up_sgu, w_up_s5, w_up_sconv, w_out,
                     w_ffn_gate, w_ffn_up, w_ffn_down):
    depth = w_in.shape[0]
    row = lambda a: a[:, None, :]
    return dict(
        gain_mix=row(g_norm_mix),
        gain_ffn=row(g_norm_ffn),
        w_in_t=jnp.swapaxes(w_in, 1, 2).astype(BF16),
        b_gates_t=jnp.broadcast_to(b_gates[:, :, None], (depth, 4 * HEADS, TOKEN_TILE)),
        w_conv_q_t=jnp.broadcast_to(w_conv_qk[:, 0, :, :, None], (depth, CONV_K, MLSTM_WIDTH, TOKEN_TILE)),
        w_conv_k=w_conv_qk[:, 1],
        g_mh=row(g_mh),
        g_sgu=row(g_sgu),
        w_sgu=w_sgu.astype(BF16),
        b_sgu_t=jnp.swapaxes(b_sgu, 1, 2),
        w_glu=w_glu.astype(BF16),
        b_glu=row(b_glu),
        w_sconv=w_sconv,
        w_up_mlstm=w_up_mlstm.astype(BF16),
        w_up_sgu=w_up_sgu.astype(BF16),
        w_up_s5=w_up_s5.astype(BF16),
        w_up_sconv=w_up_sconv.astype(BF16),
        w_out=w_out.astype(BF16),
        w_ffn_gate=w_ffn_gate.astype(BF16),
        w_ffn_up=w_ffn_up.astype(BF16),
        w_ffn_down=w_ffn_down.astype(BF16),
    )


def kernel(x, c, ctx, c_ctx, w_mod, b_mod, g_norm_mix, g_norm_ffn, w_in, b_gates, w_conv_qk, g_mh, g_sgu, w_sgu, b_sgu, s5_a_re, s5_a_im, s5_log_dt, s5_b_re, s5_b_im, s5_c_re, s5_c_im, s5_d, w_glu, b_glu, w_sconv, w_up_mlstm, w_up_sgu, w_up_s5, w_up_sconv, w_out, w_ffn_gate, w_ffn_up, w_ffn_down, g_final):
    batch, seq, d = x.shape
    ctx_len = ctx.shape[1]
    depth = w_mod.shape[0]
    assert d == D_MODEL and ctx_len == TOKEN_TILE and seq % TOKEN_TILE == 0 and batch == S5_BATCH
    ctx_tiles = ctx_len // TOKEN_TILE

    c_rows = jnp.zeros((8, d), F32).at[:batch].set(c).at[batch].set(c_ctx)
    mod_all = _modulation(c_rows, w_mod, b_mod).reshape(depth, 8, 6, d)
    stream = (ctx, x, ctx_tiles)
    stream_len = ctx_len + seq
    s5_ops = _s5_operators(s5_a_re, s5_a_im, s5_log_dt, s5_b_re, s5_b_im, s5_c_re, s5_c_im, s5_d)
    lw = _stacked_weights(g_norm_mix, g_norm_ffn, w_in, b_gates, w_conv_qk, g_mh, g_sgu, w_sgu, b_sgu,
                          w_glu, b_glu, w_sconv, w_up_mlstm, w_up_sgu, w_up_s5, w_up_sconv, w_out,
                          w_ffn_gate, w_ffn_up, w_ffn_down)

    mod_x = mod_all[:, :batch]
    mod_c = jnp.broadcast_to(mod_all[:, batch][:, None], mod_x.shape)
    mods = jnp.stack([mod_c, mod_x], axis=2)

    out = None
    for l in range(depth):
        last = l == depth - 1
        h_norm, qt, k, vt, gates_t, u_rows = _state_proj(stream, stream_len, mods, lw, l, ctx_len)
        h_fwd, h_bwd = _mlstm(qt, k, vt, gates_t, ctx_len)
        y_s5 = _s5_mixer(u_rows, s5_ops, l, stream_len // S5_CHUNK, ctx_len // S5_CHUNK)
        tile0 = ctx_tiles if last else 0
        x_mid = _merge(stream, stream_len, mods, h_norm, h_fwd, h_bwd, y_s5, lw, l, ctx_len, tile0)
        out = _ffn(x_mid, mods, lw, l, g_final.reshape(1, d), tile0, last)
        stream = (out, out, 0)
    return out
```

```python
import functools

import jax
import jax.numpy as jnp
from jax import lax
from jax.experimental import pallas as pl
from jax.experimental.pallas import tpu as pltpu

F32 = jnp.float32
BF16 = jnp.bfloat16

D_MODEL = 1024
GRID_W = 64
RMS_EPS = 1e-6
CONV_K = 3
HEADS = 4
HEAD_DIM = 128
MLSTM_WIDTH = HEADS * HEAD_DIM
MLSTM_CHUNK = 128
MLSTM_BATCH = 4
SGU_GROUPS = 4
SGU_GROUP_DIM = 128
SGU_CHUNK = 128
S5_GROUP_DIM = 16
S5_GROUPS = 24
S5_WIDTH = S5_GROUPS * S5_GROUP_DIM
S5_STATE = 64
STATE_DIM = 3 * MLSTM_WIDTH + 4 * HEADS + S5_WIDTH
GATE_COL = 3 * MLSTM_WIDTH

LANES = 128
TOKEN_TILE = 256
S5_CHUNK = 16
S5_ROW = S5_CHUNK * S5_GROUP_DIM
S5_PAIR = 2 * S5_STATE
S5_BATCH = 4
V7X_VMEM_BYTES = 64 * 1024 * 1024
VMEM_LIMIT = V7X_VMEM_BYTES - 12 * 1024 * 1024
CONTRACT_LAST = (((1,), (1,)), ((), ()))


def _resident(shape):
    zeros = (0,) * len(shape)
    return pl.BlockSpec(shape, lambda *_: zeros, pipeline_mode=pl.Buffered(1))


def _layer_resident(stacked, layer):
    zeros = (0,) * (stacked.ndim - 1)
    return pl.BlockSpec((None,) + stacked.shape[1:], lambda *_: (layer,) + zeros,
                        pipeline_mode=pl.Buffered(1))


def _row_block(stacked, layer, start, size):
    cols = stacked.shape[-1]
    return pl.BlockSpec((pl.Element(1), pl.Element(size), pl.Element(cols)), lambda *_: (layer, start, 0),
                        pipeline_mode=pl.Buffered(1))


def _mod_spec(d, layer, tile0):
    return pl.BlockSpec((None, 1, 1, 6, d), lambda i, t: (layer, i, jnp.minimum(t + tile0, 1), 0, 0))


def _params(*semantics, flags=None):
    return pltpu.CompilerParams(dimension_semantics=semantics, vmem_limit_bytes=VMEM_LIMIT, flags=flags)


def _sigmoid(x):
    return 0.5 * jnp.tanh(0.5 * x) + 0.5


def _modulated_norm(x, gain, shift, scale):
    y = x * lax.rsqrt(jnp.mean(x * x, axis=-1, keepdims=True) + RMS_EPS) * gain
    return y * (1.0 + scale) + shift


def _conv3(a, w, first, last):
    n = a.shape[0]
    prev = jnp.where(first, 0.0, pltpu.roll(a, 1, 0))
    nxt = jnp.where(last, 0.0, pltpu.roll(a, n - 1, 0))
    return prev * w[0:1] + a * w[1:2] + nxt * w[2:3]


def _row_edges(tile_idx, n_rows, ctx_len):
    period = jnp.where(tile_idx == 0, ctx_len, GRID_W)
    pos = lax.broadcasted_iota(jnp.int32, (n_rows, 1), 0) & (period - 1)
    return pos == 0, pos == period - 1


def _mod_kernel(c_ref, w_ref, b_ref, o_ref):
    a = c_ref[...]
    a = a * _sigmoid(a)
    a_hi = a.astype(BF16)
    a_lo = (a - a_hi.astype(F32)).astype(BF16)
    w = w_ref[0]
    w_hi = w.astype(BF16)
    w_lo = (w - w_hi.astype(F32)).astype(BF16)
    acc = jnp.dot(a_hi, w_hi, preferred_element_type=F32)
    acc = acc + jnp.dot(a_lo, w_hi, preferred_element_type=F32)
    acc = acc + jnp.dot(a_hi, w_lo, preferred_element_type=F32)
    o_ref[0] = acc + b_ref[0]


def _modulation(c_rows, w_mod, b_mod):
    depth, d, n = w_mod.shape
    tn = 1536
    return pl.pallas_call(
        _mod_kernel,
        out_shape=jax.ShapeDtypeStruct((depth, 8, n), F32),
        grid=(depth, n // tn),
        in_specs=[
            pl.BlockSpec((8, d), lambda l, j: (0, 0)),
            pl.BlockSpec((1, d, tn), lambda l, j: (l, 0, j)),
            pl.BlockSpec((1, 1, tn), lambda l, j: (l, 0, j)),
        ],
        out_specs=pl.BlockSpec((1, 8, tn), lambda l, j: (l, 0, j)),
        compiler_params=_params("arbitrary", "arbitrary"),
        name="modulation",
    )(c_rows, w_mod, b_mod.reshape(depth, 1, n))


def _swap_lane_blocks(tiles):
    n = LANES // S5_GROUP_DIM
    tiles = list(tiles)
    block = lax.broadcasted_iota(jnp.int32, tiles[0].shape, 1) // S5_GROUP_DIM
    dist = n // 2
    while dist:
        keep = (block & dist) == 0
        for i in range(n):
            if i & dist:
                continue
            a, b = tiles[i], tiles[i + dist]
            tiles[i] = jnp.where(keep, a, pltpu.roll(b, dist * S5_GROUP_DIM, 1))
            tiles[i + dist] = jnp.where(keep, pltpu.roll(a, LANES - dist * S5_GROUP_DIM, 1), b)
        dist //= 2
    return tiles


def _to_group_rows(z, u_ref, buf_ref):
    n_chunks = z.shape[0] // S5_CHUNK
    per_block = LANES // S5_GROUP_DIM
    for gb in range(S5_WIDTH // LANES):
        buf_ref[gb] = z[:, gb * LANES:(gb + 1) * LANES]
    for gb in range(S5_WIDTH // LANES):
        for half in range(S5_CHUNK // per_block):
            by_token = [buf_ref[gb, pl.ds(half * per_block + s, n_chunks, stride=S5_CHUNK), :]
                        for s in range(per_block)]
            for gl, tile in enumerate(_swap_lane_blocks(by_token)):
                u_ref[0, gb * per_block + gl, :, half * LANES:(half + 1) * LANES] = tile


def _from_group_rows(y_ref, buf_ref):
    n_chunks = y_ref.shape[2]
    per_block = LANES // S5_GROUP_DIM
    for gb in range(S5_WIDTH // LANES):
        for half in range(S5_CHUNK // per_block):
            by_group = [y_ref[0, gb * per_block + gl, :, half * LANES:(half + 1) * LANES]
                        for gl in range(per_block)]
            for s, tile in enumerate(_swap_lane_blocks(by_group)):
                buf_ref[gb, pl.ds(half * per_block + s, n_chunks, stride=S5_CHUNK), :] = tile
    return jnp.concatenate([buf_ref[gb] for gb in range(S5_WIDTH // LANES)], axis=-1)


def _stream_specs(stream, tile0):
    ctx_src, lat_src, offset = stream
    d = ctx_src.shape[-1]
    specs = [pl.BlockSpec((1, TOKEN_TILE, d), lambda i, t: (i, 0, 0)),
             pl.BlockSpec((1, TOKEN_TILE, d), lambda i, t: (i, jnp.maximum(t + tile0 - offset, 0), 0))]
    return specs, (ctx_src, lat_src)


def _stream_tile(t, ctx_ref, lat_ref, one_source):
    return lat_ref[0] if one_source else jnp.where(t == 0, ctx_ref[0], lat_ref[0])


def _state_proj_kernel(xc_ref, xl_ref, mod_ref, gain_ref, wqk_ref, wv_ref, wg_ref, wu_ref, bgt_ref, wc_ref,
                       eye_ref, hn_ref, qt_ref, k_ref, vt_ref, gt_ref, u_ref, ubuf_ref, *, ctx_len, one_source):
    t = pl.program_id(1)
    mod = mod_ref[0, 0]
    x = _stream_tile(t, xc_ref, xl_ref, one_source)
    h = _modulated_norm(x, gain_ref[...], mod[0:1], mod[1:2]).astype(BF16)
    hn_ref[0] = h
    z_qk = lax.dot_general(h, wqk_ref[0], CONTRACT_LAST, preferred_element_type=F32)
    z_u = lax.dot_general(h, wu_ref[0], CONTRACT_LAST, preferred_element_type=F32)
    vt_ref[0] = lax.dot_general(wv_ref[0], h, CONTRACT_LAST, preferred_element_type=F32).astype(BF16)
    gt_ref[0] = lax.dot_general(wg_ref[0], h, CONTRACT_LAST, preferred_element_type=F32) + bgt_ref[...]
    first, last = _row_edges(t, z_qk.shape[0], ctx_len)
    wc = wc_ref[...]
    w = MLSTM_WIDTH
    q = _conv3(z_qk[:, 0:w], wc[0:3], first, last)
    k = _conv3(z_qk[:, w:2 * w], wc[3:6], first, last)
    q = (q * _sigmoid(q)).astype(BF16)
    qt_ref[0] = lax.dot_general(eye_ref[...], q, CONTRACT_LAST, preferred_element_type=F32).astype(BF16)
    k_ref[0] = ((k * _sigmoid(k)) * (HEAD_DIM ** -0.5)).astype(BF16)
    _to_group_rows(z_u, u_ref, ubuf_ref)


def _state_proj(stream, s, mods, lw, layer, ctx_len):
    b, _, d = stream[0].shape
    tm = TOKEN_TILE
    tiles = s // tm
    stream_specs, stream_args = _stream_specs(stream, 0)
    tok = lambda width: pl.BlockSpec((1, tm, width), lambda i, t: (i, t, 0))
    chan = lambda height: pl.BlockSpec((1, height, tm), lambda i, t: (i, 0, t))
    w_t = lw['w_in_t']
    column_ranges = ((0, 2 * MLSTM_WIDTH), (2 * MLSTM_WIDTH, MLSTM_WIDTH), (GATE_COL, 4 * HEADS),
                     (GATE_COL + 4 * HEADS, S5_WIDTH))
    eye = jnp.eye(MLSTM_WIDTH, dtype=BF16)
    return pl.pallas_call(
        functools.partial(_state_proj_kernel, ctx_len=ctx_len, one_source=stream[0] is stream[1]),
        out_shape=(
            jax.ShapeDtypeStruct((b, s, d), BF16),
            jax.ShapeDtypeStruct((b, MLSTM_WIDTH, s), BF16),
            jax.ShapeDtypeStruct((b, s, MLSTM_WIDTH), BF16),
            jax.ShapeDtypeStruct((b, MLSTM_WIDTH, s), BF16),
            jax.ShapeDtypeStruct((b, 4 * HEADS, s), F32),
            jax.ShapeDtypeStruct((b * tiles, S5_GROUPS, tm // S5_CHUNK, S5_ROW), F32),
        ),
        grid=(b, s // tm),
        in_specs=stream_specs + [_mod_spec(d, layer, 0), _layer_resident(lw['gain_mix'], layer)]
                 + [_row_block(w_t, layer, start, size) for start, size in column_ranges]
                 + [_layer_resident(lw['b_gates_t'], layer), _layer_resident(lw['w_conv'], layer),
                    _resident(eye.shape)],
        out_specs=(tok(d), chan(MLSTM_WIDTH), tok(MLSTM_WIDTH), chan(MLSTM_WIDTH), chan(4 * HEADS),
                   pl.BlockSpec((1, S5_GROUPS, tm // S5_CHUNK, S5_ROW), lambda i, t: (i * tiles + t, 0, 0, 0))),
        scratch_shapes=[pltpu.VMEM((S5_WIDTH // LANES, tm, LANES), F32)],
        compiler_params=_params("arbitrary", "arbitrary"),
        name="state_proj",
    )(*stream_args, mods, lw['gain_mix'], w_t, w_t, w_t, w_t, lw['b_gates_t'], lw['w_conv'], eye)


def _mlstm_chunk(direction, bi, qt_ref, k_ref, vt_ref, gt_ref, h_ref, ct_ref, n_ref, m_ref):
    lc = MLSTM_CHUNK
    sign = 1 - 2 * direction
    row = lax.broadcasted_iota(jnp.int32, (lc, lc), 0)
    col = lax.broadcasted_iota(jnp.int32, (lc, lc), 1)
    tri = (col - row) * sign >= 0
    tri_b = jnp.where(tri, 1.0, 0.0).astype(BF16)

    gt = gt_ref[bi]
    lf_t = jax.nn.log_sigmoid(gt)
    hi = lf_t.astype(BF16)
    lo = (lf_t - hi.astype(F32)).astype(BF16)
    bcum_t = (jnp.dot(hi, tri_b, preferred_element_type=F32)
              + jnp.dot(lo, tri_b, preferred_element_type=F32))
    last = lc - 1 if direction == 0 else 0

    qt = qt_ref[bi]
    k = k_ref[bi]
    vt = vt_ref[bi]
    heads = []
    for hd in range(HEADS):
        sl = slice(hd * HEAD_DIM, (hd + 1) * HEAD_DIM)
        b_row = bcum_t[HEADS + hd:HEADS + hd + 1, :]
        heads.append(dict(
            bi=bi, sl=sl, qt=qt[sl, :], k=k[:, sl], vt=vt[sl, :], tri=tri,
            a_row=gt[hd:hd + 1, :] - b_row, b_row=b_row, b_last=b_row[:, last:last + 1],
            m=m_ref[hd][:, 0:1], ct=ct_ref[hd][...], n=n_ref[hd][...],
            h_ref=h_ref, ct_ref=ct_ref[hd], n_ref=n_ref[hd], m_ref=m_ref[hd]))
    return heads


def _mlstm_scores(hd):
    hd['kq'] = jnp.dot(hd['k'], hd['qt'], preferred_element_type=F32)
    n_rows = jnp.broadcast_to(hd['n'], (8, HEAD_DIM)).astype(BF16)
    hd['qn'] = jnp.dot(n_rows, hd['qt'], preferred_element_type=F32)[0:1]
    hd['cq'] = jnp.dot(hd['ct'].astype(BF16), hd['qt'], preferred_element_type=F32)


def _mlstm_weights(hd):
    lc = MLSTM_CHUNK
    a_row, b_row, b_last, m = hd['a_row'], hd['b_row'], hd['b_last'], hd['m']
    m_new = b_last + jnp.maximum(m, jnp.max(a_row, axis=1, keepdims=True))
    w_row = jnp.exp(a_row + (b_last - m_new))
    hd['decay'] = jnp.exp(b_last + m - m_new)
    hd['m_new'] = m_new
    hd['vt_w'] = (hd['vt'].astype(F32) * w_row).astype(BF16)
    hd['w_rows'] = jnp.broadcast_to(w_row, (8, lc)).astype(BF16)

    a_bc = jnp.broadcast_to(a_row, (lc, lc)).T
    log_w = jnp.where(hd['tri'], a_bc + b_row, -jnp.inf)
    log_inter = b_row + m
    m_t = jnp.maximum(log_inter, jnp.max(log_w, axis=0, keepdims=True))
    hd['inter'] = jnp.exp(log_inter - m_t)
    s_t = hd['kq'] * jnp.exp(log_w - m_t)
    den = hd['inter'] * hd['qn'] + jnp.sum(s_t, axis=0, keepdims=True)
    hd['r_den'] = 1.0 / jnp.maximum(jnp.abs(den), jnp.exp(-m_t))
    hd['s_t'] = s_t.astype(BF16)


def _mlstm_outputs(hd):
    num = jnp.dot(hd['vt'], hd['s_t'], preferred_element_type=F32) + hd['inter'] * hd['cq']
    hd['h_ref'][hd['bi'], :, hd['sl']] = (num * hd['r_den']).T
    hd['ct_ref'][...] = hd['decay'] * hd['ct'] + jnp.dot(hd['vt_w'], hd['k'], preferred_element_type=F32)
    hd['n_ref'][...] = (hd['decay'] * hd['n']
                        + jnp.dot(hd['w_rows'], hd['k'], preferred_element_type=F32)[0:1])
    hd['m_ref'][...] = jnp.broadcast_to(hd['m_new'], (1, LANES))


def _mlstm_kernel(qtf_ref, kf_ref, vtf_ref, gtf_ref, qtb_ref, kb_ref, vtb_ref, gtb_ref,
                  hf_ref, hb_ref, *state_refs):
    per = MLSTM_BATCH * 2 * HEADS
    ct_refs, n_refs, m_refs = state_refs[:per], state_refs[per:2 * per], state_refs[2 * per:]

    @pl.when(pl.program_id(1) == 0)
    def _():
        for ref in state_refs:
            ref[...] = jnp.zeros_like(ref)

    heads = []
    for bi in range(MLSTM_BATCH):
        for direction, refs in enumerate(((qtf_ref, kf_ref, vtf_ref, gtf_ref, hf_ref),
                                          (qtb_ref, kb_ref, vtb_ref, gtb_ref, hb_ref))):
            own = slice((2 * bi + direction) * HEADS, (2 * bi + direction + 1) * HEADS)
            heads += _mlstm_chunk(direction, bi, *refs, ct_refs[own], n_refs[own], m_refs[own])
    for phase in (_mlstm_scores, _mlstm_weights, _mlstm_outputs):
        for hd in heads:
            phase(hd)


def _mlstm(qt, k, vt, gates_t, ctx_len):
    b, s, w = k.shape
    lc = MLSTM_CHUNK
    n_chunks = s // lc
    n_ctx = ctx_len // lc

    def back(j):
        return jnp.where(j < n_ctx, n_ctx - 1 - j, n_chunks + n_ctx - 1 - j)

    nb = MLSTM_BATCH

    def specs(direction, chunk):
        return [pl.BlockSpec((nb, w, lc), lambda i, j: (i, 0, chunk(j))),
                pl.BlockSpec((nb, lc, w), lambda i, j: (i, chunk(j), 0)),
                pl.BlockSpec((nb, w, lc), lambda i, j: (i, 0, chunk(j))),
                pl.BlockSpec((nb, 2 * HEADS, lc), lambda i, j: (i, direction, chunk(j)))]

    fwd = lambda j: j
    out = jax.ShapeDtypeStruct((b, s, w), F32)
    chains = nb * 2 * HEADS
    return pl.pallas_call(
        _mlstm_kernel,
        out_shape=(out, out),
        grid=(b // nb, n_chunks),
        in_specs=specs(0, fwd) + specs(1, back),
        out_specs=(pl.BlockSpec((nb, lc, w), lambda i, j: (i, j, 0)),
                   pl.BlockSpec((nb, lc, w), lambda i, j: (i, back(j), 0))),
        scratch_shapes=([pltpu.VMEM((HEAD_DIM, HEAD_DIM), F32)] * chains
                        + [pltpu.VMEM((1, HEAD_DIM), F32)] * chains
                        + [pltpu.VMEM((1, LANES), F32)] * chains),
        compiler_params=_params("arbitrary", "arbitrary"),
        name="mlstm",
    )(qt, k, vt, gates_t, qt, k, vt, gates_t)


def _s5_operators(a_re, a_im, log_dt, b_re, b_im, c_re, c_im, d_skip):
    hp = lax.Precision.HIGH
    lch = S5_CHUNK
    depth = a_re.shape[0]
    a_re = a_re.astype(F32)
    a_im = a_im.astype(F32)
    dt = jnp.exp(log_dt.astype(F32))[..., None]
    la_re, la_im = dt * a_re, dt * a_im
    mag = jnp.exp(la_re)
    ab_re, ab_im = mag * jnp.cos(la_im), mag * jnp.sin(la_im)
    nr, ni = ab_re - 1.0, ab_im
    den = a_re * a_re + a_im * a_im
    f_re = ((nr * a_re + ni * a_im) / den)[:, :, :, None, :]
    f_im = ((ni * a_re - nr * a_im) / den)[:, :, :, None, :]
    bt_re = jnp.swapaxes(b_re.astype(F32), -1, -2)[:, None]
    bt_im = jnp.swapaxes(b_im.astype(F32), -1, -2)[:, None]
    bb_re = f_re * bt_re - f_im * bt_im
    bb_im = f_re * bt_im + f_im * bt_re
    c_re = c_re.astype(F32)
    c_im = c_im.astype(F32)

    n = jnp.arange(-(lch - 1), lch + 1, dtype=F32)[:, None]
    mg = jnp.exp(n * la_re[:, :, :, None, :])
    ang = n * la_im[:, :, :, None, :]
    pw_re, pw_im = mg * jnp.cos(ang), mg * jnp.sin(ang)
    zero = lch - 1

    def powers(direction, first, step):
        start = zero + first
        stop = start + step * lch
        sl = slice(start, stop if stop >= 0 else None, step)
        return pw_re[:, direction, :, sl, None, :], pw_im[:, direction, :, sl, None, :]

    def times(x_re, x_im, p):
        return x_re * p[0] - x_im * p[1], x_re * p[1] + x_im * p[0]

    def rows256(x):
        return x.reshape(depth, S5_GROUPS, S5_ROW, S5_STATE)

    bbf = (bb_re[:, 0, :, None], bb_im[:, 0, :, None])
    bbb = (bb_re[:, 1, :, None], bb_im[:, 1, :, None])
    cc = (c_re[:, :, None], c_im[:, :, None])

    def response(bb, direction, sign):
        l_re, l_im = times(*bb, powers(direction, 0, -sign))
        r_re, r_im = times(*cc, powers(direction, 0, sign))
        left = jnp.concatenate([rows256(l_re), rows256(l_im)], axis=-1)
        right = jnp.concatenate([rows256(r_re), -rows256(r_im)], axis=-1)
        return jnp.einsum('lgap,lgbp->lgab', left, right, precision=hp)

    src = (jnp.arange(S5_ROW) // S5_GROUP_DIM)[:, None]
    tgt = (jnp.arange(S5_ROW) // S5_GROUP_DIM)[None, :]
    d_rows = jnp.tile(d_skip.astype(F32).reshape(depth, S5_GROUPS, 1, S5_GROUP_DIM), (1, 1, 1, lch))
    toeplitz = (jnp.where(tgt >= src, response(bbf, 0, 1), 0.0)
                + jnp.where(src >= tgt, response(bbb, 1, -1), 0.0)
                + jnp.eye(S5_ROW, dtype=F32) * d_rows)

    def state_map(x, p):
        re, im = times(*x, p)
        return rows256(re), rows256(im)

    w_parts = jnp.stack(state_map(bbf, powers(0, lch - 1, -1)) + state_map(bbb, powers(1, 0, 1)), axis=1)
    vf_re, vf_im = state_map(cc, powers(0, 1, 1))
    vb_re, vb_im = state_map(cc, powers(1, lch, -1))
    v_parts = jnp.stack([vf_re, -vf_im, vb_re, -vb_im], axis=1)

    al_re, al_im = pw_re[:, :, :, zero + lch], pw_im[:, :, :, zero + lch]
    def rows(a):
        a = a.reshape(depth, 2, 1, S5_GROUPS * S5_STATE)
        return jnp.broadcast_to(a, (depth, 2, S5_BATCH, S5_GROUPS * S5_STATE)).reshape(depth, 2 * S5_BATCH, -1)
    a_step = jnp.stack([rows(al_re), rows(al_im)], axis=1)
    return toeplitz.astype(BF16), w_parts, v_parts, a_step


def _pair_rows(rows_ref, a):
    return rows_ref[:, a].reshape(-1, S5_ROW)


def _s5_pair_matrix(parts_ref, re, im):
    z = jnp.zeros((S5_ROW, S5_STATE), F32)
    top = jnp.concatenate([parts_ref[re, 0], z, parts_ref[im, 0], z], axis=-1)
    bottom = jnp.concatenate([z, parts_ref[re, 1], z, parts_ref[im, 1]], axis=-1)
    return jnp.concatenate([top, bottom], axis=0).astype(BF16)


def _s5_chunk_state_kernel(u_ref, w_ref, x_ref, *, n_chunks):
    u = jnp.concatenate([_pair_rows(u_ref, 0), _pair_rows(u_ref, 1)], axis=-1).astype(BF16)
    w = jnp.concatenate([_s5_pair_matrix(w_ref, 0, 1), _s5_pair_matrix(w_ref, 2, 3)], axis=-1)
    x = jnp.dot(u, w, preferred_element_type=F32)
    for b in range(S5_BATCH):
        xb = x[b * n_chunks:(b + 1) * n_chunks]
        for direction in range(2):
            for part in range(2):
                lo = (2 * direction + part) * S5_PAIR
                x_ref[part, pl.ds(direction * S5_BATCH + b, n_chunks, stride=2 * S5_BATCH), :] = (
                    xb[:, lo:lo + S5_PAIR])


def _s5_scan_kernel(x_ref, a_ref, of_ref, ob_ref, *, n_chunks, n_ctx):
    tile = 2 * S5_BATCH
    n_slabs = x_ref.shape[0]
    a_re = a_ref[0]
    a_im = a_ref[1]
    fwd_rows = lax.broadcasted_iota(jnp.int32, (tile, S5_PAIR), 0) < S5_BATCH

    def body(i, state):
        cb = jnp.where(i < n_ctx, n_ctx - 1 - i, n_chunks + n_ctx - 1 - i)
        rf = pl.multiple_of(i * tile, tile)
        rb = pl.multiple_of(cb * tile, tile)
        new_state = []
        for p in range(n_slabs // 2):
            re, im = state[2 * p], state[2 * p + 1]
            for j, val in ((2 * p, re), (2 * p + 1, im)):
                of_ref[j, pl.ds(rf, tile), :] = val
                ob_ref[j, pl.ds(rb, tile), :] = val
            in_re = jnp.where(fwd_rows, x_ref[2 * p, pl.ds(rf, tile), :], x_ref[2 * p, pl.ds(rb, tile), :])
            in_im = jnp.where(fwd_rows, x_ref[2 * p + 1, pl.ds(rf, tile), :],
                              x_ref[2 * p + 1, pl.ds(rb, tile), :])
            ar, ai = a_re[:, p * S5_PAIR:(p + 1) * S5_PAIR], a_im[:, p * S5_PAIR:(p + 1) * S5_PAIR]
            new_state.append(ar * re - ai * im + in_re)
            new_state.append(ar * im + ai * re + in_im)
        return tuple(new_state)

    zero = jnp.zeros((tile, S5_PAIR), F32)
    lax.fori_loop(0, n_chunks, body, (zero,) * n_slabs, unroll=4)


def _s5_output_kernel(u_ref, t_ref, xf_ref, xb_ref, v_ref, y_ref, *, n_chunks):
    stride = 2 * S5_BATCH

    def batch_rows(ref, first_row):
        return jnp.concatenate(
            [jnp.concatenate([ref[part, pl.ds(first_row + b, n_chunks, stride=stride), :]
                              for part in range(2)], axis=-1) for b in range(S5_BATCH)], axis=0)

    xf = batch_rows(xf_ref, 0)
    xb = batch_rows(xb_ref, S5_BATCH)
    last = (((1,), (1,)), ((), ()))
    y = lax.dot_general(xf.astype(BF16), _s5_pair_matrix(v_ref, 0, 1), last, preferred_element_type=F32)
    y = y + lax.dot_general(xb.astype(BF16), _s5_pair_matrix(v_ref, 2, 3), last, preferred_element_type=F32)
    for a in range(2):
        within = jnp.dot(_pair_rows(u_ref, a).astype(BF16), t_ref[a], preferred_element_type=F32)
        y_ref[:, a] = (within + y[:, a * S5_ROW:(a + 1) * S5_ROW]).reshape(y_ref.shape[0], -1, S5_ROW)


def _s5_pair_kernel(u_ref, w_ref, t_ref, v_ref, a_ref, y_ref, xc_ref, xf_ref, xb_ref, *, n_chunks, n_ctx):
    _s5_chunk_state_kernel(u_ref, w_ref, xc_ref, n_chunks=n_chunks)
    _s5_scan_kernel(xc_ref, a_ref, xf_ref, xb_ref, n_chunks=n_chunks, n_ctx=n_ctx)
    _s5_output_kernel(u_ref, t_ref, xf_ref, xb_ref, v_ref, y_ref, n_chunks=n_chunks)


def _s5_mixer(u_rows, ops, layer, n_chunks, n_ctx):
    toeplitz, w_parts, v_parts, a_step = ops
    tiles, g, per_tile, _ = u_rows.shape
    n_pairs = g // 2
    states = pltpu.VMEM((2, n_chunks * 2 * S5_BATCH, S5_PAIR), F32)
    parts = pl.BlockSpec((None, 4, 2, S5_ROW, S5_STATE), lambda i: (layer, 0, i, 0, 0))
    pair_rows = pl.BlockSpec((tiles, 2, per_tile, S5_ROW), lambda i: (0, i, 0, 0))
    return pl.pallas_call(
        functools.partial(_s5_pair_kernel, n_chunks=n_chunks, n_ctx=n_ctx),
        out_shape=jax.ShapeDtypeStruct(u_rows.shape, F32),
        grid=(n_pairs,),
        in_specs=[pair_rows,
                  parts,
                  pl.BlockSpec((None, 2, S5_ROW, S5_ROW), lambda i: (layer, i, 0, 0)),
                  parts,
                  pl.BlockSpec((None, 2, 2 * S5_BATCH, S5_PAIR), lambda i: (layer, 0, 0, i))],
        out_specs=pair_rows,
        scratch_shapes=[states, states, states],
        compiler_params=_params("arbitrary"),
        name="s5_pair",
    )(u_rows, w_parts, toeplitz, v_parts, a_step)


def _merge_kernel(xc_ref, xl_ref, mod_ref, hn_ref, hf_ref, hb_ref, ys_ref, wo_ref, gmh_ref, gsgu_ref,
                  wsgu_ref, bsgu_ref, wglu_ref, bglu_ref, wsc_ref, wum_ref, wug_ref, wus_ref,
                  wuc_ref, wout_ref, o_ref, ybuf_ref, *, ctx_len, tile0, one_source):
    t = pl.program_id(1) + tile0
    mod = mod_ref[0, 0]
    h = hn_ref[0]
    tm = h.shape[0]

    def proj(start, width):
        return lax.dot_general(h, wo_ref[0, start:start + width, :], CONTRACT_LAST,
                               preferred_element_type=F32)

    w = MLSTM_WIDTH
    z_all = proj(0, 6 * w)
    z_o, z_su, z_sv, z_cb, z_cc, z_cx = (z_all[:, i * w:(i + 1) * w] for i in range(6))
    ys = jax.nn.gelu(_from_group_rows(ys_ref, ybuf_ref))
    z_glu = jnp.dot(ys.astype(BF16), wglu_ref[...], preferred_element_type=F32)

    hm = hf_ref[0] + hb_ref[0]
    parts = []
    for hd in range(HEADS):
        hh = hm[:, hd * HEAD_DIM:(hd + 1) * HEAD_DIM]
        parts.append(hh * lax.rsqrt(jnp.mean(hh * hh, axis=-1, keepdims=True) + RMS_EPS))
    y_a = (jnp.concatenate(parts, axis=-1) * gmh_ref[...]) * _sigmoid(z_o)

    su = jax.nn.gelu(z_su)
    sv = jax.nn.gelu(z_sv)
    mu = jnp.mean(sv, axis=-1, keepdims=True)
    cen = sv - mu
    var = jnp.mean(cen * cen, axis=-1, keepdims=True)
    vn = (cen * lax.rsqrt(var + RMS_EPS) * gsgu_ref[...]).astype(BF16)
    bias = bsgu_ref[...]
    rows = []
    for n in range(tm // SGU_CHUNK):
        cols = []
        for gi in range(SGU_GROUPS):
            blk = vn[n * SGU_CHUNK:(n + 1) * SGU_CHUNK, gi * SGU_GROUP_DIM:(gi + 1) * SGU_GROUP_DIM]
            cols.append(jnp.dot(wsgu_ref[gi], blk, preferred_element_type=F32) + bias[:, gi:gi + 1])
        rows.append(jnp.concatenate(cols, axis=-1))
    y_b = su * jnp.concatenate(rows, axis=0)

    y_c = ys * _sigmoid(z_glu + bglu_ref[...])

    first, last = _row_edges(t, tm, ctx_len)
    y_d = z_cb * _conv3(z_cc * z_cx, wsc_ref[...], first, last)

    acc = None
    for j, (y, up_ref) in enumerate(zip((y_a, y_b, y_c, y_d), (wum_ref, wug_ref, wus_ref, wuc_ref))):
        gate = proj(6 * w + j * D_MODEL, D_MODEL)
        term = _sigmoid(gate) * jnp.dot(y.astype(BF16), up_ref[...], preferred_element_type=F32)
        acc = term if acc is None else acc + term
    mixed = jnp.dot(acc.astype(BF16), wout_ref[...], preferred_element_type=F32)
    o_ref[0] = _stream_tile(t, xc_ref, xl_ref, one_source) + mod[2:3] * mixed


def _merge(stream, s, mods, h_norm, h_fwd, h_bwd, y_s5, lw, layer, ctx_len, tile0):
    b, _, d = stream[0].shape
    tm = TOKEN_TILE
    tiles = s // tm
    stream_specs, stream_args = _stream_specs(stream, tile0)
    tok = lambda width: pl.BlockSpec((1, tm, width), lambda i, t: (i, t + tile0, 0))
    group_rows = pl.BlockSpec((1, S5_GROUPS, tm // S5_CHUNK, S5_ROW),
                              lambda i, t: (i * tiles + t + tile0, 0, 0, 0))
    weights = tuple(lw[name] for name in (
        'g_mh', 'g_sgu', 'w_sgu', 'b_sgu_t', 'w_glu', 'b_glu', 'w_sconv',
        'w_up_mlstm', 'w_up_sgu', 'w_up_s5', 'w_up_sconv', 'w_out'))
    w_t = lw['w_in_t']
    out_side = _row_block(w_t, layer, STATE_DIM, w_t.shape[1] - STATE_DIM)
    return pl.pallas_call(
        functools.partial(_merge_kernel, ctx_len=ctx_len, tile0=tile0, one_source=stream[0] is stream[1]),
        out_shape=jax.ShapeDtypeStruct((b, s - tile0 * tm, d), F32),
        grid=(b, tiles - tile0),
        in_specs=stream_specs
                 + [_mod_spec(d, layer, tile0), tok(d), tok(MLSTM_WIDTH), tok(MLSTM_WIDTH), group_rows, out_side]
                 + [_layer_resident(w, layer) for w in weights],
        out_specs=pl.BlockSpec((1, tm, d), lambda i, t: (i, t, 0)),
        scratch_shapes=[pltpu.VMEM((S5_WIDTH // LANES, tm, LANES), F32)],
        compiler_params=_params("arbitrary", "arbitrary"),
        name="merge",
    )(*stream_args, mods, h_norm, h_fwd, h_bwd, y_s5, w_t, *weights)


def _ffn_kernel(x_ref, mod_ref, gain_ref, wg_ref, wu_ref, wd_ref, gf_ref, o_ref, *, final):
    x = x_ref[0]
    mod = mod_ref[0, 0]
    h = _modulated_norm(x, gain_ref[...], mod[3:4], mod[4:5]).astype(BF16)
    a = jnp.dot(h, wg_ref[...], preferred_element_type=F32)
    b = jnp.dot(h, wu_ref[...], preferred_element_type=F32)
    hid = ((a * _sigmoid(a)) * b).astype(BF16)
    y = x + mod[5:6] * jnp.dot(hid, wd_ref[...], preferred_element_type=F32)
    if final:
        y = y * lax.rsqrt(jnp.mean(y * y, axis=-1, keepdims=True) + RMS_EPS) * gf_ref[...]
    o_ref[0] = y


def _ffn(xs, mods, lw, layer, g_final, tile0, final):
    b, s, d = xs.shape
    tm = 2 * TOKEN_TILE if tile0 > 0 and s % (2 * TOKEN_TILE) == 0 else TOKEN_TILE
    weights = tuple(lw[name] for name in ('gain_ffn', 'w_ffn_gate', 'w_ffn_up', 'w_ffn_down'))
    return pl.pallas_call(
        functools.partial(_ffn_kernel, final=final),
        out_shape=jax.ShapeDtypeStruct((b, s, d), F32),
        grid=(b, s // tm),
        in_specs=[pl.BlockSpec((1, tm, d), lambda i, t: (i, t, 0)), _mod_spec(d, layer, tile0)]
                 + [_layer_resident(w, layer) for w in weights] + [_resident((1, d))],
        out_specs=pl.BlockSpec((1, tm, d), lambda i, t: (i, t, 0)),
        compiler_params=_params("arbitrary", "arbitrary"),
        name="ffn_final" if final else "ffn",
    )(xs, mods, *weights, g_final)


def _stacked_weights(g_norm_mix, g_norm_ffn, w_in, b_gates, w_conv_qk, g_mh, g_sgu, w_sgu, b_sgu, w_glu,
                     b_glu, w_sconv, w_up_mlstm, w_up_sgu, w_up_s5, w_up_sconv, w_out,
                     w_ffn_gate, w_ffn_up, w_ffn_down):
    depth = w_in.shape[0]
    row = lambda a: a[:, None, :]
    return dict(
        gain_mix=row(g_norm_mix),
        gain_ffn=row(g_norm_ffn),
        w_in_t=jnp.swapaxes(w_in, 1, 2).astype(BF16),
        b_gates_t=jnp.broadcast_to(b_gates[:, :, None], (depth, 4 * HEADS, TOKEN_TILE)),
        w_conv=w_conv_qk.reshape(depth, 2 * CONV_K, MLSTM_WIDTH),
        g_mh=row(g_mh),
        g_sgu=row(g_sgu),
        w_sgu=w_sgu.astype(BF16),
        b_sgu_t=jnp.swapaxes(b_sgu, 1, 2),
        w_glu=w_glu.astype(BF16),
        b_glu=row(b_glu),
        w_sconv=w_sconv,
        w_up_mlstm=w_up_mlstm.astype(BF16),
        w_up_sgu=w_up_sgu.astype(BF16),
        w_up_s5=w_up_s5.astype(BF16),
        w_up_sconv=w_up_sconv.astype(BF16),
        w_out=w_out.astype(BF16),
        w_ffn_gate=w_ffn_gate.astype(BF16),
        w_ffn_up=w_ffn_up.astype(BF16),
        w_ffn_down=w_ffn_down.astype(BF16),
    )


def kernel(x, c, ctx, c_ctx, w_mod, b_mod, g_norm_mix, g_norm_ffn, w_in, b_gates, w_conv_qk, g_mh, g_sgu, w_sgu, b_sgu, s5_a_re, s5_a_im, s5_log_dt, s5_b_re, s5_b_im, s5_c_re, s5_c_im, s5_d, w_glu, b_glu, w_sconv, w_up_mlstm, w_up_sgu, w_up_s5, w_up_sconv, w_out, w_ffn_gate, w_ffn_up, w_ffn_down, g_final):
    batch, seq, d = x.shape
    ctx_len = ctx.shape[1]
    depth = w_mod.shape[0]
    assert d == D_MODEL and ctx_len == TOKEN_TILE and seq % TOKEN_TILE == 0 and batch == S5_BATCH
    ctx_tiles = ctx_len // TOKEN_TILE

    c_rows = jnp.zeros((8, d), F32).at[:batch].set(c).at[batch].set(c_ctx)
    mod_all = _modulation(c_rows, w_mod, b_mod).reshape(depth, 8, 6, d)
    stream = (ctx, x, ctx_tiles)
    stream_len = ctx_len + seq
    s5_ops = _s5_operators(s5_a_re, s5_a_im, s5_log_dt, s5_b_re, s5_b_im, s5_c_re, s5_c_im, s5_d)
    lw = _stacked_weights(g_norm_mix, g_norm_ffn, w_in, b_gates, w_conv_qk, g_mh, g_sgu, w_sgu, b_sgu,
                          w_glu, b_glu, w_sconv, w_up_mlstm, w_up_sgu, w_up_s5, w_up_sconv, w_out,
                          w_ffn_gate, w_ffn_up, w_ffn_down)

    mod_x = mod_all[:, :batch]
    mod_c = jnp.broadcast_to(mod_all[:, batch][:, None], mod_x.shape)
    mods = jnp.stack([mod_c, mod_x], axis=2)

    out = None
    for l in range(depth):
        last = l == depth - 1
        h_norm, qt, k, vt, gates_t, u_rows = _state_proj(stream, stream_len, mods, lw, l, ctx_len)
        h_fwd, h_bwd = _mlstm(qt, k, vt, gates_t, ctx_len)
        y_s5 = _s5_mixer(u_rows, s5_ops, l, stream_len // S5_CHUNK, ctx_len // S5_CHUNK)
        tile0 = ctx_tiles if last else 0
        x_mid = _merge(stream, stream_len, mods, h_norm, h_fwd, h_bwd, y_s5, lw, l, ctx_len, tile0)
        out = _ffn(x_mid, mods, lw, l, g_final.reshape(1, d), tile0, last)
        stream = (out, out, 0)
    return out
```
